```python
import jax, jax.numpy as jnp
from jax import lax
import numpy as np

D_MODEL = 1024
BATCH = 2
SEQ = 16384
DEPTH = 1
DEC_BATCH = 16
DEC_SEQ = 32
PAST_LEN = 1024

CHUNK = 64
POOL_WIDTH = D_MODEL // 2
POOL_GROUPS = 4
POOL_GROUP_DIM = POOL_WIDTH // POOL_GROUPS
POOL_WINDOWS = (2, 4, 8, 16)
POOL_HIST = max(POOL_WINDOWS) - 1
GMLP_WIDTH = D_MODEL // 2
GMLP_HEADS = 4
GMLP_HEAD_DIM = GMLP_WIDTH // GMLP_HEADS
GMLP_CHUNK = 128
N_BRANCH = 2
IN_COLS = POOL_WIDTH + 2 * GMLP_WIDTH + N_BRANCH * D_MODEL
N_GROUPS = 4
EXPERTS_PER_GROUP = 4
N_EXPERTS = N_GROUPS * EXPERTS_PER_GROUP
TOP_K = 2
D_EXPERT = D_MODEL // 2
DISPATCH_BLOCK = 256
EPS = 1e-6

kernel_name = "hybrid_pool_gmlp_hmoe_stream_step"


def rmsnorm(x, g):
    xf = x.astype(jnp.float32)
    y = xf * lax.rsqrt(jnp.mean(xf * xf, axis=-1, keepdims=True) + EPS) * g.astype(jnp.float32)
    return y.astype(x.dtype)


def pool_mixer(a, hist, pos0, w_pool, pool_scale):
    B, L, _ = a.shape
    ext = jnp.concatenate([hist, a], axis=1)
    cs = jnp.cumsum(ext.astype(jnp.float32), axis=1)
    cs = jnp.pad(cs, ((0, 0), (1, 0), (0, 0)))
    pos = pos0 + jnp.arange(L, dtype=jnp.int32)
    outs = []
    for k, w in enumerate(POOL_WINDOWS):
        sl = slice(k * POOL_GROUP_DIM, (k + 1) * POOL_GROUP_DIM)
        s = cs[:, POOL_HIST + 1:POOL_HIST + 1 + L, sl] - cs[:, POOL_HIST + 1 - w:POOL_HIST + 1 - w + L, sl]
        cnt = jnp.minimum(pos + 1, w).astype(jnp.float32)
        outs.append(s / cnt[None, :, None])
    pooled = jnp.concatenate(outs, axis=-1)
    d = (pooled - a.astype(jnp.float32)).astype(a.dtype).reshape(B, L, POOL_GROUPS, POOL_GROUP_DIM)
    m = jnp.einsum('blgc,gcd->blgd', d, w_pool).reshape(B, L, POOL_WIDTH) * pool_scale
    return m, ext[:, -POOL_HIST:]


def gmlp_mixer(u, v, g_v, w_spatial, b_spatial):
    B, L, _ = v.shape
    vn = rmsnorm(v, g_v)
    Lp = -(-L // GMLP_CHUNK) * GMLP_CHUNK
    vp = jnp.pad(vn, ((0, 0), (0, Lp - L), (0, 0)))
    vp = vp.reshape(B, Lp // GMLP_CHUNK, GMLP_CHUNK, GMLP_HEADS, GMLP_HEAD_DIM)
    mask = jnp.tril(jnp.ones((GMLP_CHUNK, GMLP_CHUNK), w_spatial.dtype))
    ws = w_spatial * mask
    s = jnp.einsum('hij,bcjhd->bcihd', ws, vp) + b_spatial.T[None, None, :, :, None]
    s = s.reshape(B, Lp, GMLP_WIDTH)[:, :L]
    return u * s, vn


def hier_moe(x2d, w_group, b_group, w_router, b_router, w_gate, w_up, w_down):
    T, D = x2d.shape
    xf = x2d.astype(jnp.float32)
    gl = xf @ w_group.astype(jnp.float32) + b_group.astype(jnp.float32)
    gp = jax.nn.softmax(gl, axis=-1)
    g = jnp.argmax(gl, axis=-1).astype(jnp.int32)
    p_g = jnp.take_along_axis(gp, g[:, None], axis=1)[:, 0]
    el = (xf @ w_router.astype(jnp.float32) + b_router.astype(jnp.float32)).reshape(T, N_GROUPS, EXPERTS_PER_GROUP)
    el_g = jnp.take_along_axis(el, g[:, None, None], axis=1)[:, 0]
    top_v, top_i = lax.top_k(el_g, TOP_K)
    gate = jax.nn.softmax(top_v, axis=-1) * p_g[:, None]
    eid = (g[:, None] * EXPERTS_PER_GROUP + top_i).reshape(-1).astype(jnp.int32)
    tok = jnp.repeat(jnp.arange(T, dtype=jnp.int32), TOP_K)
    wts = gate.reshape(-1)
    A = T * TOP_K
    order = jnp.argsort(eid)
    e_s, t_s, w_s = eid[order], tok[order], wts[order]
    counts = jnp.bincount(eid, length=N_EXPERTS).astype(jnp.int32)
    starts = jnp.cumsum(counts) - counts
    padded = (counts + DISPATCH_BLOCK - 1) // DISPATCH_BLOCK * DISPATCH_BLOCK
    pends = jnp.cumsum(padded)
    pstarts = pends - padded
    dest = pstarts[e_s] + jnp.arange(A, dtype=jnp.int32) - starts[e_s]
    NB = -(-A // DISPATCH_BLOCK) + N_EXPERTS
    slot_tok = jnp.zeros((NB * DISPATCH_BLOCK,), jnp.int32).at[dest].set(t_s)
    blk_e = jnp.minimum(jnp.searchsorted(pends, jnp.arange(NB, dtype=jnp.int32) * DISPATCH_BLOCK, side='right'),
                        N_EXPERTS - 1).astype(jnp.int32)
    xb = x2d[slot_tok].reshape(NB, DISPATCH_BLOCK, D)

    def expert_block(args):
        xb_i, e = args
        h = jax.nn.silu(xb_i @ w_gate[e]) * (xb_i @ w_up[e])
        return h @ w_down[e]

    yb = lax.map(expert_block, (xb, blk_e)).reshape(NB * DISPATCH_BLOCK, D)
    return jnp.zeros_like(x2d).at[t_s].add(yb[dest] * w_s[:, None].astype(x2d.dtype))


def trunk_layer(x, hist, pos0, g_mix, w_in, w_pool, pool_scale, g_v, w_spatial, b_spatial,
                w_up_pool, w_up_gmlp, w_out, g_ffn, w_group, b_group, w_router, b_router,
                w_gate, w_up, w_down):
    B, L, D = x.shape
    xn = rmsnorm(x, g_mix)
    z = xn @ w_in
    a, u, v, ga, gb = jnp.split(z, [POOL_WIDTH, POOL_WIDTH + GMLP_WIDTH, POOL_WIDTH + 2 * GMLP_WIDTH,
                                    POOL_WIDTH + 2 * GMLP_WIDTH + D_MODEL], axis=-1)
    pa, new_hist = pool_mixer(a, hist, pos0, w_pool, pool_scale)
    pb, vn = gmlp_mixer(u, v, g_v, w_spatial, b_spatial)
    merged = jax.nn.sigmoid(ga) * (pa @ w_up_pool) + jax.nn.sigmoid(gb) * (pb @ w_up_gmlp)
    h = x + merged @ w_out
    hn = rmsnorm(h, g_ffn).reshape(B * L, D)
    y = h + hier_moe(hn, w_group, b_group, w_router, b_router, w_gate, w_up, w_down).reshape(B, L, D)
    return y, new_hist, vn


def setup_inputs(seed: int = 0) -> dict:
    key = jax.random.key(seed)
    ks = jax.random.split(key, 24)
    f32 = jnp.float32
    nrm = lambda k, s, sc: jax.random.normal(k, s, f32) * sc
    return {
        "x_prompt": nrm(ks[0], (BATCH, SEQ, D_MODEL), 1.0),
        "x_sample": nrm(ks[1], (DEC_BATCH, DEC_SEQ, D_MODEL), 1.0),
        "state_pool": nrm(ks[2], (DEPTH, DEC_BATCH, POOL_HIST, POOL_WIDTH), 0.5),
        "g_mix": 1.0 + nrm(ks[3], (DEPTH, D_MODEL), 0.02),
        "w_in": nrm(ks[4], (DEPTH, D_MODEL, IN_COLS), D_MODEL ** -0.5),
        "w_pool": nrm(ks[5], (DEPTH, POOL_GROUPS, POOL_GROUP_DIM, POOL_GROUP_DIM), POOL_GROUP_DIM ** -0.5),
        "pool_scale": 1.0 + nrm(ks[6], (DEPTH, POOL_WIDTH), 0.1),
        "g_v": 1.0 + nrm(ks[7], (DEPTH, GMLP_WIDTH), 0.02),
        "w_spatial": nrm(ks[8], (DEPTH, GMLP_HEADS, GMLP_CHUNK, GMLP_CHUNK), GMLP_CHUNK ** -0.5),
        "b_spatial": 1.0 + nrm(ks[9], (DEPTH, GMLP_HEADS, GMLP_CHUNK), 0.1),
        "w_up_pool": nrm(ks[10], (DEPTH, POOL_WIDTH, D_MODEL), POOL_WIDTH ** -0.5),
        "w_up_gmlp": nrm(ks[11], (DEPTH, GMLP_WIDTH, D_MODEL), GMLP_WIDTH ** -0.5),
        "w_out": nrm(ks[12], (DEPTH, D_MODEL, D_MODEL), D_MODEL ** -0.5),
        "g_ffn": 1.0 + nrm(ks[13], (DEPTH, D_MODEL), 0.02),
        "w_group": nrm(ks[14], (DEPTH, D_MODEL, N_GROUPS), D_MODEL ** -0.5),
        "b_group": nrm(ks[15], (DEPTH, N_GROUPS), 0.01),
        "w_router": nrm(ks[16], (DEPTH, D_MODEL, N_EXPERTS), D_MODEL ** -0.5),
        "b_router": nrm(ks[17], (DEPTH, N_EXPERTS), 0.01),
        "w_gate": nrm(ks[18], (DEPTH, N_EXPERTS, D_MODEL, D_EXPERT), D_MODEL ** -0.5),
        "w_up": nrm(ks[19], (DEPTH, N_EXPERTS, D_MODEL, D_EXPERT), D_MODEL ** -0.5),
        "w_down": nrm(ks[20], (DEPTH, N_EXPERTS, D_EXPERT, D_MODEL), D_EXPERT ** -0.5),
        "g_final": 1.0 + nrm(ks[21], (D_MODEL,), 0.02),
    }


def reference(x_prompt, x_sample, state_pool, g_mix, w_in, w_pool, pool_scale, g_v, w_spatial,
              b_spatial, w_up_pool, w_up_gmlp, w_out, g_ffn, w_group, b_group, w_router, b_router,
              w_gate, w_up, w_down, g_final):
    hp, hs = x_prompt, x_sample
    pool_p, pool_s, gv_s = [], [], []
    for l in range(DEPTH):
        lw = (g_mix[l], w_in[l], w_pool[l], pool_scale[l], g_v[l], w_spatial[l], b_spatial[l],
              w_up_pool[l], w_up_gmlp[l], w_out[l], g_ffn[l], w_group[l], b_group[l],
              w_router[l], b_router[l], w_gate[l], w_up[l], w_down[l])
        zero_hist = jnp.zeros((hp.shape[0], POOL_HIST, POOL_WIDTH), hp.dtype)
        hp, hist_p, _ = trunk_layer(hp, zero_hist, 0, *lw)
        hs, hist_s, vn_s = trunk_layer(hs, state_pool[l], PAST_LEN, *lw)
        pool_p.append(hist_p)
        pool_s.append(hist_s)
        gv_s.append(vn_s)
    y_prompt = rmsnorm(hp, g_final)
    y_sample = rmsnorm(hs, g_final)
    new_pool_prompt = jnp.stack(pool_p, axis=0)
    new_pool_sample = jnp.stack(pool_s, axis=0)
    new_gmlp_v_sample = jnp.stack(gv_s, axis=0)
    return (y_prompt, y_sample, new_pool_prompt, new_pool_sample, new_gmlp_v_sample)
```

```python
import functools

import jax
import jax.numpy as jnp
from jax import lax
from jax.experimental import pallas as pl
from jax.experimental.pallas import tpu as pltpu

F32 = jnp.float32
BF16 = jnp.bfloat16
I32 = jnp.int32

D_MODEL = 1024
SEQ = 16384
BATCH = 2
DEC_BATCH = 16
DEC_SEQ = 32
PAST_LEN = 1024
POOL_WIDTH = 512
POOL_GROUP_DIM = 128
POOL_WINDOWS = (2, 4, 8, 16)
HIST_ROWS = 16
GMLP_WIDTH = 512
GMLP_HEADS = 4
GMLP_HEAD_DIM = 128
GMLP_CHUNK = 128
N_GROUPS = 4
EXPERTS_PER_GROUP = 4
N_EXPERTS = 16
D_EXPERT = 512
EPS = 1e-6

LANES = 128
SUBLANES = 8
TL = 512
BM = 512
T_PROMPT = BATCH * SEQ
T_SAMPLE = DEC_BATCH * DEC_SEQ
T_TOTAL = T_PROMPT + T_SAMPLE
NT_PROMPT = T_PROMPT // TL
TILES_PER_SEQ = SEQ // TL
NT_TOTAL = T_TOTAL // TL
CAP = T_TOTAL + BM
CAP_BLOCKS = CAP // BM
ROW_W = D_MODEL + LANES
NB_MAX = 2 * T_TOTAL // BM + N_EXPERTS
VMEM_LIMIT = 56 * 1024 * 1024

REC_E0, REC_E1, REC_D0, REC_D1 = 0, 1, 2, 3
REC_ROWS = 8
PAY_W0, PAY_W1, PAY_E0 = 0, 1, 2


def _dot(a, b):
    return jnp.dot(a, b, preferred_element_type=F32)


def _rms(x, g):
    return x * lax.rsqrt(jnp.mean(x * x, axis=-1, keepdims=True) + EPS) * g


def _split_bf16(x):
    hi = x.astype(BF16)
    lo = (x - hi.astype(F32)).astype(BF16)
    return hi, lo


def _row_copy(src_ref, s, dst_ref, d, sem):
    return pltpu.make_async_copy(src_ref.at[pl.ds(s, 1)], dst_ref.at[pl.ds(d, 1)], sem)


def _pool_windows(load_shifted, a, pos, wpool_ref, pscale_ref):
    outs = []
    for g, w in enumerate(POOL_WINDOWS):
        cols = slice(g * POOL_GROUP_DIM, (g + 1) * POOL_GROUP_DIM)
        acc = a[:, cols]
        for k in range(1, w):
            acc = acc + load_shifted(k, g)
        cnt = jnp.minimum(pos + 1, w).astype(F32)
        d = (acc / cnt - a[:, cols]).astype(BF16)
        outs.append(_dot(d, wpool_ref[g]))
    return jnp.concatenate(outs, axis=-1) * pscale_ref[...]


def _mixer_tail(x, pa, pb, ga, gb, wupp_ref, wupg_ref, wout_ref, gffn_ref):
    merged = (jax.nn.sigmoid(ga) * _dot(pa.astype(BF16), wupp_ref[...])
              + jax.nn.sigmoid(gb) * _dot(pb.astype(BF16), wupg_ref[...]))
    h = x + _dot(merged.astype(BF16), wout_ref[...])
    return h, _rms(h, gffn_ref[...])


def _init_ltri(ltri_ref):
    r = lax.broadcasted_iota(I32, (TL, TL), 0)
    c = lax.broadcasted_iota(I32, (TL, TL), 1)
    ltri_ref[...] = jnp.where(c < r, 1.0, 0.0).astype(BF16)


def _route_and_scatter(hn, wrt_ref, brt_ref, ltri_ref, run_ref, row_ref, recv_ref, recs_ref,
                       rec_out_ref, xb_ref, sem):
    hi, lo = _split_bf16(hn)
    whi, wlo = _split_bf16(wrt_ref[...])
    lg = _dot(hi, whi) + _dot(lo, whi) + _dot(hi, wlo) + brt_ref[...]

    lane = lax.broadcasted_iota(I32, lg.shape, 1)
    neg = jnp.float32(-jnp.inf)
    big = jnp.int32(1 << 20)

    def first_argmax(v, vmax):
        return jnp.min(jnp.where(v == vmax, lane, big), axis=-1, keepdims=True)

    gmask = lane < N_GROUPS
    glm = jnp.where(gmask, lg, neg)
    gmax = jnp.max(glm, axis=-1, keepdims=True)
    grp = first_argmax(glm, gmax)
    p_g = 1.0 / jnp.sum(jnp.where(gmask, jnp.exp(lg - gmax), 0.0), axis=-1, keepdims=True)

    lo_lane = N_GROUPS + EXPERTS_PER_GROUP * grp
    elm = jnp.where((lane >= lo_lane) & (lane < lo_lane + EXPERTS_PER_GROUP), lg, neg)
    v1 = jnp.max(elm, axis=-1, keepdims=True)
    i1 = first_argmax(elm, v1)
    elm2 = jnp.where(lane == i1, neg, elm)
    v2 = jnp.max(elm2, axis=-1, keepdims=True)
    i2 = first_argmax(elm2, v2)
    ex2 = jnp.exp(v2 - v1)
    inv = 1.0 / (1.0 + ex2)
    w0 = inv * p_g
    w1 = ex2 * inv * p_g
    e0 = i1 - N_GROUPS
    e1 = i2 - N_GROUPS

    sel0 = lane == e0
    sel1 = lane == e1
    onehot = jnp.where(sel0 | sel1, 1.0, 0.0)
    before = _dot(ltri_ref[...], onehot.astype(BF16)) + run_ref[...]
    r0 = jnp.sum(jnp.where(sel0, before, 0.0), axis=-1, keepdims=True)
    r1 = jnp.sum(jnp.where(sel1, before, 0.0), axis=-1, keepdims=True)
    run_ref[...] = run_ref[...] + jnp.sum(onehot, axis=0, keepdims=True)
    e0f = e0.astype(F32)
    e1f = e1.astype(F32)
    d0 = e0f * CAP + r0
    d1 = e1f * CAP + r1

    rec = jnp.where(lane == REC_E0, e0f,
                    jnp.where(lane == REC_E1, e1f,
                              jnp.where(lane == REC_D0, d0,
                                        jnp.where(lane == REC_D1, d1, 0.0))))
    rec_t = rec.T[0:REC_ROWS, :].astype(I32)
    recv_ref[...] = rec_t
    rec_out_ref[...] = rec_t

    row_ref[:, 0:D_MODEL] = hn
    row_ref[:, D_MODEL:ROW_W] = jnp.where(lane == PAY_W0, w0,
                                          jnp.where(lane == PAY_W1, w1,
                                                    jnp.where(lane == PAY_E0, e0f, 0.0)))

    pltpu.sync_copy(recv_ref, recs_ref)

    def issue(t, c):
        _row_copy(row_ref, t, xb_ref, recs_ref[REC_D0, t], sem).start()
        _row_copy(row_ref, t, xb_ref, recs_ref[REC_D1, t], sem).start()
        return c

    lax.fori_loop(0, TL, issue, 0)

    def drain(t, c):
        _row_copy(row_ref, t, xb_ref, recs_ref[REC_D0, t], sem).wait()
        _row_copy(row_ref, t, xb_ref, recs_ref[REC_D1, t], sem).wait()
        return c

    lax.fori_loop(0, TL, drain, 0)


def _mixer_prompt_kernel(x_ref, gmix_ref, win_ref, wpool_ref, pscale_ref, gv_ref, ws_ref, bsp_ref,
                         wupp_ref, wupg_ref, wout_ref, gffn_ref, wrt_ref, brt_ref,
                         h_ref, rec_out_ref, pool_ref, cnt_ref, xb_ref,
                         aext_ref, s_ref, row_ref, ltri_ref, run_ref, recv_ref, recs_ref, sem):
    b = pl.program_id(0)
    j = pl.program_id(1)

    @pl.when((b == 0) & (j == 0))
    def _():
        _init_ltri(ltri_ref)
        run_ref[...] = jnp.zeros_like(run_ref)

    x = x_ref[...]
    xn = _rms(x, gmix_ref[...]).astype(BF16)
    a = _dot(xn, win_ref[:, 0:512])
    u = _dot(xn, win_ref[:, 512:1024])
    v = _dot(xn, win_ref[:, 1024:1536])
    ga = _dot(xn, win_ref[:, 1536:2560])
    gb = _dot(xn, win_ref[:, 2560:3584])

    @pl.when(j == 0)
    def _():
        aext_ref[0:HIST_ROWS, :] = jnp.zeros((HIST_ROWS, POOL_WIDTH), F32)

    @pl.when(j > 0)
    def _():
        aext_ref[0:HIST_ROWS, :] = aext_ref[TL:TL + HIST_ROWS, :]

    aext_ref[HIST_ROWS:HIST_ROWS + TL, :] = a
    pool_ref[...] = a[TL - HIST_ROWS:TL, :]
    pos = j * TL + lax.broadcasted_iota(I32, (TL, 1), 0)

    def load_shifted(k, g):
        return aext_ref[HIST_ROWS - k:HIST_ROWS - k + TL, g * POOL_GROUP_DIM:(g + 1) * POOL_GROUP_DIM]

    pa = _pool_windows(load_shifted, a, pos, wpool_ref, pscale_ref)

    vn = _rms(v, gv_ref[...]).astype(BF16)
    rr = lax.broadcasted_iota(I32, (GMLP_CHUNK, GMLP_CHUNK), 0)
    cc = lax.broadcasted_iota(I32, (GMLP_CHUNK, GMLP_CHUNK), 1)
    for hd in range(GMLP_HEADS):
        cols = slice(hd * GMLP_HEAD_DIM, (hd + 1) * GMLP_HEAD_DIM)
        wsm = jnp.where(cc <= rr, ws_ref[hd], 0.0).astype(BF16)
        for c in range(TL // GMLP_CHUNK):
            rows = slice(c * GMLP_CHUNK, (c + 1) * GMLP_CHUNK)
            s_ref[rows, cols] = _dot(wsm, vn[rows, cols]) + bsp_ref[rows, hd:hd + 1]
    pb = u * s_ref[...]

    h, hn = _mixer_tail(x, pa, pb, ga, gb, wupp_ref, wupg_ref, wout_ref, gffn_ref)
    h_ref[...] = h
    _route_and_scatter(hn, wrt_ref, brt_ref, ltri_ref, run_ref, row_ref, recv_ref, recs_ref,
                       rec_out_ref, xb_ref, sem)
    cnt_ref[...] = run_ref[...]


def _mixer_sample_kernel(x_ref, hist_ref, run0_ref, gmix_ref, win_ref, wpool_ref, pscale_ref, gv_ref,
                         ws_ref, bsp_ref, wupp_ref, wupg_ref, wout_ref, gffn_ref, wrt_ref, brt_ref,
                         xb_in_ref,
                         h_ref, rec_out_ref, pool_ref, vn_ref, cnt_ref, xb_ref,
                         aext_ref, row_ref, ltri_ref, run_ref, recv_ref, recs_ref, cntv_ref, cnts_ref,
                         sem):
    del xb_in_ref
    _init_ltri(ltri_ref)
    run_ref[...] = run0_ref[...]

    x = x_ref[...]
    xn = _rms(x, gmix_ref[...]).astype(BF16)
    a = _dot(xn, win_ref[:, 0:512])
    u = _dot(xn, win_ref[:, 512:1024])
    v = _dot(xn, win_ref[:, 1024:1536])
    ga = _dot(xn, win_ref[:, 1536:2560])
    gb = _dot(xn, win_ref[:, 2560:3584])

    a3 = a.reshape(DEC_BATCH, DEC_SEQ, POOL_WIDTH)
    aext_ref[:, 0:HIST_ROWS, :] = hist_ref[...]
    aext_ref[:, HIST_ROWS:HIST_ROWS + DEC_SEQ, :] = a3
    pool_ref[...] = a3[:, DEC_SEQ - HIST_ROWS:DEC_SEQ, :]
    row = lax.broadcasted_iota(I32, (TL, 1), 0)
    pos = PAST_LEN + row % DEC_SEQ

    def load_shifted(k, g):
        sl = aext_ref[:, HIST_ROWS - k:HIST_ROWS - k + DEC_SEQ, g * POOL_GROUP_DIM:(g + 1) * POOL_GROUP_DIM]
        return sl.reshape(TL, POOL_GROUP_DIM)

    pa = _pool_windows(load_shifted, a, pos, wpool_ref, pscale_ref)

    vnf = _rms(v, gv_ref[...])
    vn_ref[...] = vnf
    vn = vnf.astype(BF16)
    rsel = (lax.broadcasted_iota(I32, (TL, GMLP_CHUNK), 1)
            == lax.broadcasted_iota(I32, (TL, GMLP_CHUNK), 0) % DEC_SEQ)
    rsel_b = jnp.where(rsel, 1.0, 0.0).astype(BF16)
    csel = (lax.broadcasted_iota(I32, (GMLP_CHUNK, TL), 0)
            == lax.broadcasted_iota(I32, (GMLP_CHUNK, TL), 1) % DEC_SEQ)
    csel_b = jnp.where(csel, 1.0, 0.0).astype(BF16)
    rr = lax.broadcasted_iota(I32, (TL, TL), 0)
    cc = lax.broadcasted_iota(I32, (TL, TL), 1)
    keep = (rr // DEC_SEQ == cc // DEC_SEQ) & (cc <= rr)
    s_parts = []
    for hd in range(GMLP_HEADS):
        cols = slice(hd * GMLP_HEAD_DIM, (hd + 1) * GMLP_HEAD_DIM)
        wrow = _dot(rsel_b, ws_ref[hd].astype(BF16)).astype(BF16)
        wfull = _dot(wrow, csel_b)
        wblk = jnp.where(keep, wfull, 0.0).astype(BF16)
        s_parts.append(_dot(wblk, vn[:, cols]) + bsp_ref[:, hd:hd + 1])
    pb = u * jnp.concatenate(s_parts, axis=-1)

    h, hn = _mixer_tail(x, pa, pb, ga, gb, wupp_ref, wupg_ref, wout_ref, gffn_ref)
    h_ref[...] = h
    _route_and_scatter(hn, wrt_ref, brt_ref, ltri_ref, run_ref, row_ref, recv_ref, recs_ref,
                       rec_out_ref, xb_ref, sem)
    cnt_ref[...] = run_ref[...]

    cntv_ref[...] = jnp.broadcast_to(run_ref[...], (REC_ROWS, LANES)).astype(I32)
    pltpu.sync_copy(cntv_ref, cnts_ref)
    row_ref[...] = jnp.zeros_like(row_ref)
    for e in range(N_EXPERTS):
        start = e * CAP + cnts_ref[0, e]
        for r in range(SUBLANES):
            cp = _row_copy(row_ref, r, xb_ref, start + r, sem)
            cp.start()
            cp.wait()
        aligned = pl.multiple_of(jnp.bitwise_and(start + (SUBLANES - 1), -SUBLANES), SUBLANES)
        cp = pltpu.make_async_copy(row_ref, xb_ref.at[pl.ds(aligned, BM)], sem)
        cp.start()
        cp.wait()


def _experts_kernel(blk_e_ref, blk_row_ref, nvalid_ref, xb_ref, wg_ref, wu_ref, wd_ref, yb_ref):
    i = pl.program_id(0)

    @pl.when(i < nvalid_ref[0])
    def _():
        blk = xb_ref[...]
        x = blk[:, 0:D_MODEL].astype(BF16)
        pay = blk[:, D_MODEL:ROW_W]
        hg = _dot(x, wg_ref[...])
        hu = _dot(x, wu_ref[...])
        act = (hg * jax.nn.sigmoid(hg)) * hu
        y = _dot(act.astype(BF16), wd_ref[...])
        gate = jnp.where(pay[:, PAY_E0:PAY_E0 + 1].astype(I32) == blk_e_ref[i],
                         pay[:, PAY_W0:PAY_W0 + 1], pay[:, PAY_W1:PAY_W1 + 1])
        yb_ref[...] = y * gate


def _combine_kernel(rec_ref, hp_ref, hs_ref, gfin_ref, yb_ref, yp_ref, ys_ref, buf_ref, sem):
    i = pl.program_id(0)

    def issue(t, c):
        _row_copy(yb_ref, rec_ref[REC_D0, t], buf_ref.at[0], t, sem).start()
        _row_copy(yb_ref, rec_ref[REC_D1, t], buf_ref.at[1], t, sem).start()
        return c

    lax.fori_loop(0, TL, issue, 0)

    def drain(t, c):
        _row_copy(yb_ref, rec_ref[REC_D0, t], buf_ref.at[0], t, sem).wait()
        _row_copy(yb_ref, rec_ref[REC_D1, t], buf_ref.at[1], t, sem).wait()
        return c

    lax.fori_loop(0, TL, drain, 0)
    moe = buf_ref[0] + buf_ref[1]

    @pl.when(i < NT_PROMPT)
    def _():
        yp_ref[...] = _rms(hp_ref[...] + moe, gfin_ref[...])

    @pl.when(i == NT_PROMPT)
    def _():
        ys_ref[...] = _rms(hs_ref[...] + moe, gfin_ref[...])


def _const_spec(shape):
    zeros = (0,) * len(shape)
    return pl.BlockSpec(shape, lambda *_: zeros, pipeline_mode=pl.Buffered(1))


def kernel(x_prompt, x_sample, state_pool, g_mix, w_in, w_pool, pool_scale, g_v, w_spatial, b_spatial,
           w_up_pool, w_up_gmlp, w_out, g_ffn, w_group, b_group, w_router, b_router, w_gate, w_up,
           w_down, g_final):
    gmix = g_mix[0][None, :]
    win = w_in[0].astype(BF16)
    wpool = w_pool[0].astype(BF16)
    pscale = pool_scale[0][None, :]
    gv = g_v[0][None, :]
    ws = w_spatial[0]
    bsp_t = b_spatial[0].T
    wupp = w_up_pool[0].astype(BF16)
    wupg = w_up_gmlp[0].astype(BF16)
    wout = w_out[0].astype(BF16)
    gffn = g_ffn[0][None, :]
    n_route = N_GROUPS + N_EXPERTS
    wrt = jnp.pad(jnp.concatenate([w_group[0], w_router[0]], axis=1), ((0, 0), (0, LANES - n_route)))
    brt = jnp.pad(jnp.concatenate([b_group[0], b_router[0]]), (0, LANES - n_route))[None, :]
    wg = w_gate[0].astype(BF16)
    wu = w_up[0].astype(BF16)
    wd = w_down[0].astype(BF16)
    gfin = g_final[None, :]
    bsp_prompt = jnp.tile(bsp_t, (TL // GMLP_CHUNK, 1))
    bsp_sample = jnp.tile(bsp_t[:DEC_SEQ], (DEC_BATCH, 1))
    hist = jnp.pad(state_pool[0], ((0, 0), (1, 0), (0, 0)))

    weight_specs = [
        _const_spec((1, D_MODEL)),
        _const_spec((D_MODEL, 3584)),
        _const_spec((4, POOL_GROUP_DIM, POOL_GROUP_DIM)),
        _const_spec((1, POOL_WIDTH)),
        _const_spec((1, GMLP_WIDTH)),
        _const_spec((GMLP_HEADS, GMLP_CHUNK, GMLP_CHUNK)),
        _const_spec((TL, GMLP_HEADS)),
        _const_spec((POOL_WIDTH, D_MODEL)),
        _const_spec((GMLP_WIDTH, D_MODEL)),
        _const_spec((D_MODEL, D_MODEL)),
        _const_spec((1, D_MODEL)),
        _const_spec((D_MODEL, LANES)),
        _const_spec((1, LANES)),
    ]
    any_spec = pl.BlockSpec(memory_space=pl.ANY)
    route_scratch = [
        pltpu.VMEM((TL, ROW_W), F32),
        pltpu.VMEM((TL, TL), BF16),
        pltpu.VMEM((1, LANES), F32),
        pltpu.VMEM((REC_ROWS, TL), I32),
        pltpu.SMEM((REC_ROWS, TL), I32),
    ]

    h_p, rec_p, pool_p, cnt_p, xb = pl.pallas_call(
        _mixer_prompt_kernel,
        grid=(BATCH, TILES_PER_SEQ),
        in_specs=[pl.BlockSpec((None, TL, D_MODEL), lambda b, j: (b, j, 0))] + weight_specs,
        out_specs=[
            pl.BlockSpec((None, TL, D_MODEL), lambda b, j: (b, j, 0)),
            pl.BlockSpec((None, REC_ROWS, TL), lambda b, j: (b * TILES_PER_SEQ + j, 0, 0)),
            pl.BlockSpec((None, HIST_ROWS, POOL_WIDTH), lambda b, j: (b, 0, 0)),
            pl.BlockSpec((1, LANES), lambda b, j: (0, 0)),
            any_spec,
        ],
        out_shape=[
            jax.ShapeDtypeStruct((BATCH, SEQ, D_MODEL), F32),
            jax.ShapeDtypeStruct((NT_PROMPT, REC_ROWS, TL), I32),
            jax.ShapeDtypeStruct((BATCH, HIST_ROWS, POOL_WIDTH), F32),
            jax.ShapeDtypeStruct((1, LANES), F32),
            jax.ShapeDtypeStruct((N_EXPERTS * CAP, ROW_W), F32),
        ],
        scratch_shapes=[
            pltpu.VMEM((HIST_ROWS + TL, POOL_WIDTH), F32),
            pltpu.VMEM((TL, GMLP_WIDTH), F32),
        ] + route_scratch + [pltpu.SemaphoreType.DMA],
        compiler_params=pltpu.CompilerParams(
            dimension_semantics=("arbitrary", "arbitrary"), vmem_limit_bytes=VMEM_LIMIT),
        name="mixer_prompt",
    )(x_prompt, gmix, win, wpool, pscale, gv, ws, bsp_prompt, wupp, wupg, wout, gffn, wrt, brt)

    n_in_sample = 17
    h_s, rec_s, pool_s, vn_s, cnt, xb = pl.pallas_call(
        _mixer_sample_kernel,
        grid=(1,),
        in_specs=[
            _const_spec((TL, D_MODEL)),
            _const_spec((DEC_BATCH, HIST_ROWS, POOL_WIDTH)),
            _const_spec((1, LANES)),
        ] + weight_specs + [any_spec],
        out_specs=[
            _const_spec((TL, D_MODEL)),
            pl.BlockSpec((None, REC_ROWS, TL), lambda i: (0, 0, 0)),
            _const_spec((DEC_BATCH, HIST_ROWS, POOL_WIDTH)),
            _const_spec((TL, GMLP_WIDTH)),
            _const_spec((1, LANES)),
            any_spec,
        ],
        out_shape=[
            jax.ShapeDtypeStruct((T_SAMPLE, D_MODEL), F32),
            jax.ShapeDtypeStruct((1, REC_ROWS, TL), I32),
            jax.ShapeDtypeStruct((DEC_BATCH, HIST_ROWS, POOL_WIDTH), F32),
            jax.ShapeDtypeStruct((T_SAMPLE, GMLP_WIDTH), F32),
            jax.ShapeDtypeStruct((1, LANES), F32),
            jax.ShapeDtypeStruct((N_EXPERTS * CAP, ROW_W), F32),
        ],
        scratch_shapes=[
            pltpu.VMEM((DEC_BATCH, HIST_ROWS + DEC_SEQ, POOL_WIDTH), F32),
        ] + route_scratch + [
            pltpu.VMEM((REC_ROWS, LANES), I32),
            pltpu.SMEM((REC_ROWS, LANES), I32),
            pltpu.SemaphoreType.DMA,
        ],
        input_output_aliases={n_in_sample - 1: 5},
        compiler_params=pltpu.CompilerParams(
            dimension_semantics=("arbitrary",), vmem_limit_bytes=VMEM_LIMIT),
        name="mixer_sample",
    )(x_sample.reshape(T_SAMPLE, D_MODEL), hist, cnt_p, gmix, win, wpool, pscale, gv, ws, bsp_sample,
      wupp, wupg, wout, gffn, wrt, brt, xb)

    counts = cnt[0, :N_EXPERTS].astype(I32)
    nblk = (counts + BM - 1) // BM
    bend = jnp.cumsum(nblk)
    bstart = bend - nblk
    nvalid = bend[-1]
    step = jnp.minimum(jnp.arange(NB_MAX, dtype=I32), nvalid - 1)
    blk_e = jnp.minimum(jnp.searchsorted(bend, step, side="right"), N_EXPERTS - 1).astype(I32)
    blk_row = (blk_e * CAP_BLOCKS + step - bstart[blk_e]).astype(I32)

    yb = pl.pallas_call(
        _experts_kernel,
        grid_spec=pltpu.PrefetchScalarGridSpec(
            num_scalar_prefetch=3,
            grid=(NB_MAX,),
            in_specs=[
                pl.BlockSpec((BM, ROW_W), lambda i, be, br, nv: (br[i], 0)),
                pl.BlockSpec((None, D_MODEL, D_EXPERT), lambda i, be, br, nv: (be[i], 0, 0)),
                pl.BlockSpec((None, D_MODEL, D_EXPERT), lambda i, be, br, nv: (be[i], 0, 0)),
                pl.BlockSpec((None, D_EXPERT, D_MODEL), lambda i, be, br, nv: (be[i], 0, 0)),
            ],
            out_specs=pl.BlockSpec((BM, D_MODEL), lambda i, be, br, nv: (br[i], 0)),
        ),
        out_shape=jax.ShapeDtypeStruct((N_EXPERTS * CAP, D_MODEL), F32),
        compiler_params=pltpu.CompilerParams(
            dimension_semantics=("arbitrary",), vmem_limit_bytes=VMEM_LIMIT),
        name="experts",
    )(blk_e, blk_row, nvalid[None], xb, wg, wu, wd)

    rec = jnp.concatenate([rec_p, rec_s], axis=0)
    last_p = NT_PROMPT - 1

    def hp_map(i):
        ip = jnp.minimum(i, last_p)
        return (ip // TILES_PER_SEQ, ip % TILES_PER_SEQ, 0)

    y_p, y_s = pl.pallas_call(
        _combine_kernel,
        grid=(NT_TOTAL,),
        in_specs=[
            pl.BlockSpec((None, REC_ROWS, TL), lambda i: (i, 0, 0), memory_space=pltpu.SMEM),
            pl.BlockSpec((None, TL, D_MODEL), hp_map),
            pl.BlockSpec((TL, D_MODEL), lambda i: (0, 0)),
            pl.BlockSpec((1, D_MODEL), lambda i: (0, 0)),
            any_spec,
        ],
        out_specs=[
            pl.BlockSpec((None, TL, D_MODEL), hp_map),
            pl.BlockSpec((TL, D_MODEL), lambda i: (0, 0)),
        ],
        out_shape=[
            jax.ShapeDtypeStruct((BATCH, SEQ, D_MODEL), F32),
            jax.ShapeDtypeStruct((T_SAMPLE, D_MODEL), F32),
        ],
        scratch_shapes=[pltpu.VMEM((2, TL, D_MODEL), F32), pltpu.SemaphoreType.DMA],
        compiler_params=pltpu.CompilerParams(
            dimension_semantics=("arbitrary",), vmem_limit_bytes=VMEM_LIMIT),
        name="combine",
    )(rec, h_p, h_s, gfin, yb)

    y_prompt = y_p
    y_sample = y_s.reshape(DEC_BATCH, DEC_SEQ, D_MODEL)
    new_pool_prompt = pool_p[None, :, 1:, :]
    new_pool_sample = pool_s[None, :, 1:, :]
    new_gmlp_v_sample = vn_s.reshape(1, DEC_BATCH, DEC_SEQ, GMLP_WIDTH)
    return (y_prompt, y_sample, new_pool_prompt, new_pool_sample, new_gmlp_v_sample)
```

```python
import jax
import jax.numpy as jnp
from jax import lax
from jax.experimental import pallas as pl
from jax.experimental.pallas import tpu as pltpu

F32 = jnp.float32
BF16 = jnp.bfloat16
I32 = jnp.int32

D_MODEL = 1024
SEQ = 16384
BATCH = 2
DEC_BATCH = 16
DEC_SEQ = 32
PAST_LEN = 1024
POOL_WIDTH = 512
POOL_GROUP_DIM = 128
POOL_WINDOWS = (2, 4, 8, 16)
HIST_ROWS = 16
GMLP_WIDTH = 512
GMLP_HEADS = 4
GMLP_HEAD_DIM = 128
GMLP_CHUNK = 128
IN_COLS = POOL_WIDTH + 2 * GMLP_WIDTH + 2 * D_MODEL
N_GROUPS = 4
EXPERTS_PER_GROUP = 4
N_EXPERTS = 16
D_EXPERT = 512
EPS = 1e-6

LANES = 128
SUBLANES = 8
assert D_MODEL == SUBLANES * LANES
TL = 512
BM = 512
T_PROMPT = BATCH * SEQ
T_SAMPLE = DEC_BATCH * DEC_SEQ
T_TOTAL = T_PROMPT + T_SAMPLE
NT_PROMPT = T_PROMPT // TL
TILES_PER_SEQ = SEQ // TL
NT_TOTAL = T_TOTAL // TL
assert T_SAMPLE == TL
N_ASSIGN = 2 * T_TOTAL
NB_MIN = N_ASSIGN // BM
NB_MAX = (N_ASSIGN + N_EXPERTS * (BM - 1)) // BM
N_SLOTS = NB_MAX * BM
ZERO_ROWS = BM // 2
DMA_UNROLL = 8
VMEM_LIMIT = 56 * 1024 * 1024

REC_E0, REC_E1, REC_R0, REC_R1 = 0, 1, 2, 3
REC_ROWS = 8
PAY_W0, PAY_W1 = 0, 1


def _dot(a, b):
    return jnp.dot(a, b, preferred_element_type=F32)


def _rms(x, g):
    return x * lax.rsqrt(jnp.mean(x * x, axis=-1, keepdims=True) + EPS) * g


def _split_bf16(x):
    hi = x.astype(BF16)
    lo = (x - hi.astype(F32)).astype(BF16)
    return hi, lo


def _row_copy(src_ref, s, dst_ref, d, sem):
    return pltpu.make_async_copy(src_ref.at[pl.ds(s, 1)], dst_ref.at[pl.ds(d, 1)], sem)


def _pool_windows(load_shifted, a, pos, wpool_ref, pscale_ref):
    outs = []
    for g, w in enumerate(POOL_WINDOWS):
        cols = slice(g * POOL_GROUP_DIM, (g + 1) * POOL_GROUP_DIM)
        acc = a[:, cols]
        for k in range(1, w):
            acc = acc + load_shifted(k, g)
        cnt = jnp.minimum(pos + 1, w).astype(F32)
        d = (acc / cnt - a[:, cols]).astype(BF16)
        outs.append(_dot(d, wpool_ref[g]))
    return jnp.concatenate(outs, axis=-1) * pscale_ref[...]


def _mixer_tail(x, pa, pb, ga, gb, wupp_ref, wupg_ref, wout_ref, gffn_ref):
    merged = (jax.nn.sigmoid(ga) * _dot(pa.astype(BF16), wupp_ref[...])
              + jax.nn.sigmoid(gb) * _dot(pb.astype(BF16), wupg_ref[...]))
    h = x + _dot(merged.astype(BF16), wout_ref[...])
    return h, _rms(h, gffn_ref[...])


def _init_ltri(ltri_ref):
    r = lax.broadcasted_iota(I32, (TL, TL), 0)
    c = lax.broadcasted_iota(I32, (TL, TL), 1)
    ltri_ref[...] = jnp.where(c < r, 1.0, 0.0).astype(BF16)


def _route(hn, wrt_ref, brt_ref, ltri_ref, run_ref, rec_out_ref, pay_out_ref):
    hi, lo = _split_bf16(hn)
    whi, wlo = _split_bf16(wrt_ref[...])
    lg = _dot(hi, whi) + _dot(lo, whi) + _dot(hi, wlo) + brt_ref[...]

    lane = lax.broadcasted_iota(I32, lg.shape, 1)
    neg = jnp.float32(-jnp.inf)
    big = jnp.int32(1 << 20)

    def first_argmax(v, vmax):
        return jnp.min(jnp.where(v == vmax, lane, big), axis=-1, keepdims=True)

    gmask = lane < N_GROUPS
    glm = jnp.where(gmask, lg, neg)
    gmax = jnp.max(glm, axis=-1, keepdims=True)
    grp = first_argmax(glm, gmax)
    p_g = 1.0 / jnp.sum(jnp.where(gmask, jnp.exp(lg - gmax), 0.0), axis=-1, keepdims=True)

    lo_lane = N_GROUPS + EXPERTS_PER_GROUP * grp
    elm = jnp.where((lane >= lo_lane) & (lane < lo_lane + EXPERTS_PER_GROUP), lg, neg)
    v1 = jnp.max(elm, axis=-1, keepdims=True)
    i1 = first_argmax(elm, v1)
    elm2 = jnp.where(lane == i1, neg, elm)
    v2 = jnp.max(elm2, axis=-1, keepdims=True)
    i2 = first_argmax(elm2, v2)
    ex2 = jnp.exp(v2 - v1)
    inv = 1.0 / (1.0 + ex2)
    w0 = inv * p_g
    w1 = ex2 * inv * p_g
    e0 = i1 - N_GROUPS
    e1 = i2 - N_GROUPS

    sel0 = lane == e0
    sel1 = lane == e1
    onehot = jnp.where(sel0 | sel1, 1.0, 0.0)
    before = _dot(ltri_ref[...], onehot.astype(BF16)) + run_ref[...]
    r0 = jnp.sum(jnp.where(sel0, before, 0.0), axis=-1, keepdims=True)
    r1 = jnp.sum(jnp.where(sel1, before, 0.0), axis=-1, keepdims=True)
    run_ref[...] = run_ref[...] + jnp.sum(onehot, axis=0, keepdims=True)

    rec = jnp.where(lane == REC_E0, e0.astype(F32),
                    jnp.where(lane == REC_E1, e1.astype(F32),
                              jnp.where(lane == REC_R0, r0,
                                        jnp.where(lane == REC_R1, r1, 0.0))))
    rec_out_ref[...] = rec.T[0:REC_ROWS, :].astype(I32)
    pay_out_ref[...] = jnp.where(lane == PAY_W0, w0, jnp.where(lane == PAY_W1, w1, 0.0))


def _mixer_prompt_kernel(x_ref, gmix_ref, win_ref, wpool_ref, pscale_ref, gv_ref, ws_ref, bsp_ref,
                         wupp_ref, wupg_ref, wout_ref, gffn_ref, wrt_ref, brt_ref,
                         h_ref, rec_out_ref, pay_out_ref, pool_ref, cnt_ref,
                         aext_ref, s_ref, ltri_ref, run_ref):
    b = pl.program_id(0)
    j = pl.program_id(1)

    @pl.when((b == 0) & (j == 0))
    def _():
        _init_ltri(ltri_ref)
        run_ref[...] = jnp.zeros_like(run_ref)

    x = x_ref[...]
    xn = _rms(x, gmix_ref[...]).astype(BF16)
    a = _dot(xn, win_ref[:, 0:512])
    u = _dot(xn, win_ref[:, 512:1024])
    v = _dot(xn, win_ref[:, 1024:1536])
    ga = _dot(xn, win_ref[:, 1536:2560])
    gb = _dot(xn, win_ref[:, 2560:3584])

    @pl.when(j == 0)
    def _():
        aext_ref[0:HIST_ROWS, :] = jnp.zeros((HIST_ROWS, POOL_WIDTH), F32)

    @pl.when(j > 0)
    def _():
        aext_ref[0:HIST_ROWS, :] = aext_ref[TL:TL + HIST_ROWS, :]

    aext_ref[HIST_ROWS:HIST_ROWS + TL, :] = a
    pool_ref[...] = a[TL - HIST_ROWS:TL, :]
    pos = j * TL + lax.broadcasted_iota(I32, (TL, 1), 0)

    def load_shifted(k, g):
        return aext_ref[HIST_ROWS - k:HIST_ROWS - k + TL, g * POOL_GROUP_DIM:(g + 1) * POOL_GROUP_DIM]

    pa = _pool_windows(load_shifted, a, pos, wpool_ref, pscale_ref)

    vn = _rms(v, gv_ref[...]).astype(BF16)
    rr = lax.broadcasted_iota(I32, (GMLP_CHUNK, GMLP_CHUNK), 0)
    cc = lax.broadcasted_iota(I32, (GMLP_CHUNK, GMLP_CHUNK), 1)
    for hd in range(GMLP_HEADS):
        cols = slice(hd * GMLP_HEAD_DIM, (hd + 1) * GMLP_HEAD_DIM)
        wsm = jnp.where(cc <= rr, ws_ref[hd], 0.0).astype(BF16)
        for c in range(TL // GMLP_CHUNK):
            rows = slice(c * GMLP_CHUNK, (c + 1) * GMLP_CHUNK)
            s_ref[rows, cols] = _dot(wsm, vn[rows, cols]) + bsp_ref[rows, hd:hd + 1]
    pb = u * s_ref[...]

    h, hn = _mixer_tail(x, pa, pb, ga, gb, wupp_ref, wupg_ref, wout_ref, gffn_ref)
    h_ref[...] = h
    _route(hn, wrt_ref, brt_ref, ltri_ref, run_ref, rec_out_ref, pay_out_ref)
    cnt_ref[...] = run_ref[...]


def _mixer_sample_kernel(x_ref, hist_ref, run0_ref, gmix_ref, win_ref, wpool_ref, pscale_ref, gv_ref,
                         ws_ref, bsp_ref, wupp_ref, wupg_ref, wout_ref, gffn_ref, wrt_ref, brt_ref,
                         h_ref, rec_out_ref, pay_out_ref, pool_ref, vn_ref, cnt_ref,
                         aext_ref, ltri_ref, run_ref):
    _init_ltri(ltri_ref)
    run_ref[...] = run0_ref[...]

    x = x_ref[...]
    xn = _rms(x, gmix_ref[...]).astype(BF16)
    a = _dot(xn, win_ref[:, 0:512])
    u = _dot(xn, win_ref[:, 512:1024])
    v = _dot(xn, win_ref[:, 1024:1536])
    ga = _dot(xn, win_ref[:, 1536:2560])
    gb = _dot(xn, win_ref[:, 2560:3584])

    a3 = a.reshape(DEC_BATCH, DEC_SEQ, POOL_WIDTH)
    aext_ref[:, 0:HIST_ROWS, :] = hist_ref[...]
    aext_ref[:, HIST_ROWS:HIST_ROWS + DEC_SEQ, :] = a3
    pool_ref[...] = a3[:, DEC_SEQ - HIST_ROWS:DEC_SEQ, :]
    row = lax.broadcasted_iota(I32, (TL, 1), 0)
    pos = PAST_LEN + row % DEC_SEQ

    def load_shifted(k, g):
        sl = aext_ref[:, HIST_ROWS - k:HIST_ROWS - k + DEC_SEQ, g * POOL_GROUP_DIM:(g + 1) * POOL_GROUP_DIM]
        return sl.reshape(TL, POOL_GROUP_DIM)

    pa = _pool_windows(load_shifted, a, pos, wpool_ref, pscale_ref)

    vnf = _rms(v, gv_ref[...])
    vn_ref[...] = vnf
    vn = vnf.astype(BF16)
    rsel = (lax.broadcasted_iota(I32, (TL, GMLP_CHUNK), 1)
            == lax.broadcasted_iota(I32, (TL, GMLP_CHUNK), 0) % DEC_SEQ)
    rsel_b = jnp.where(rsel, 1.0, 0.0).astype(BF16)
    csel = (lax.broadcasted_iota(I32, (GMLP_CHUNK, TL), 0)
            == lax.broadcasted_iota(I32, (GMLP_CHUNK, TL), 1) % DEC_SEQ)
    csel_b = jnp.where(csel, 1.0, 0.0).astype(BF16)
    rr = lax.broadcasted_iota(I32, (TL, TL), 0)
    cc = lax.broadcasted_iota(I32, (TL, TL), 1)
    keep = (rr // DEC_SEQ == cc // DEC_SEQ) & (cc <= rr)
    s_parts = []
    for hd in range(GMLP_HEADS):
        cols = slice(hd * GMLP_HEAD_DIM, (hd + 1) * GMLP_HEAD_DIM)
        wrow = _dot(rsel_b, ws_ref[hd].astype(BF16)).astype(BF16)
        wfull = _dot(wrow, csel_b)
        wblk = jnp.where(keep, wfull, 0.0).astype(BF16)
        s_parts.append(_dot(wblk, vn[:, cols]) + bsp_ref[:, hd:hd + 1])
    pb = u * jnp.concatenate(s_parts, axis=-1)

    h, hn = _mixer_tail(x, pa, pb, ga, gb, wupp_ref, wupg_ref, wout_ref, gffn_ref)
    h_ref[...] = h
    _route(hn, wrt_ref, brt_ref, ltri_ref, run_ref, rec_out_ref, pay_out_ref)
    cnt_ref[...] = run_ref[...]


def _dispatch_kernel(pad_start_ref, pad_len_ref, nvalid_ref,
                     d0_ref, d1_ref, hp_ref, hs_ref, gffn_ref, xb_ref, row_ref, zero_ref, sems):
    i = pl.program_id(0)
    slot = i % 2
    rows = row_ref.at[slot]

    def fill(h):
        rows[...] = _rms(h, gffn_ref[...]).reshape(TL, SUBLANES, LANES)

    @pl.when(i < NT_PROMPT)
    def _():
        fill(hp_ref[...])

    @pl.when(i == NT_PROMPT)
    def _():
        fill(hs_ref[...])

    def issue(t, c):
        _row_copy(rows, t, xb_ref, d0_ref[0, t], sems.at[slot]).start()
        _row_copy(rows, t, xb_ref, d1_ref[0, t], sems.at[slot]).start()
        return c

    lax.fori_loop(0, TL, issue, 0, unroll=DMA_UNROLL)

    def drain(s):
        def body(t, c):
            _row_copy(row_ref.at[s], t, xb_ref, 0, sems.at[s]).wait()
            _row_copy(row_ref.at[s], t, xb_ref, 0, sems.at[s]).wait()
            return c
        lax.fori_loop(0, TL, body, 0, unroll=DMA_UNROLL)

    @pl.when(i > 0)
    def _():
        drain(1 - slot)

    @pl.when(i == NT_TOTAL - 1)
    def _():
        drain(slot)
        zero_ref[...] = jnp.zeros_like(zero_ref)
        bits = [1 << k for k in reversed(range(BM.bit_length() - 1))]

        def pad_copies(fn):
            for e in range(N_EXPERTS):
                off = pad_start_ref[e]
                n = pad_len_ref[e]
                for bit in bits:
                    take = jnp.bitwise_and(n, bit)

                    @pl.when(take != 0)
                    def _(off=off, bit=bit):
                        fn(pltpu.make_async_copy(zero_ref.at[pl.ds(0, bit)], xb_ref.at[pl.ds(off, bit)],
                                                 sems.at[slot]))
                    off = off + take
            for blk in range(NB_MIN, NB_MAX):
                @pl.when(blk >= nvalid_ref[0])
                def _(blk=blk):
                    for half in range(BM // ZERO_ROWS):
                        fn(pltpu.make_async_copy(
                            zero_ref, xb_ref.at[pl.ds(blk * BM + half * ZERO_ROWS, ZERO_ROWS)], sems.at[slot]))

        pad_copies(lambda cp: cp.start())
        pad_copies(lambda cp: cp.wait())


def _experts_kernel(blk_e_ref, nvalid_ref, xb_ref, wg_ref, wu_ref, wd_ref, yb_ref):
    i = pl.program_id(0)

    @pl.when(i < nvalid_ref[0])
    def _():
        x = xb_ref[...].reshape(BM, D_MODEL).astype(BF16)
        hg = _dot(x, wg_ref[...])
        hu = _dot(x, wu_ref[...])
        act = (hg * jax.nn.sigmoid(hg)) * hu
        yb_ref[...] = _dot(act.astype(BF16), wd_ref[...]).reshape(BM, SUBLANES, LANES)

    @pl.when(i >= nvalid_ref[0])
    def _():
        yb_ref[...] = jnp.zeros_like(yb_ref)


def _combine_kernel(d0_ref, d1_ref, d0n_ref, d1n_ref, payp_ref, pays_ref, hp_ref, hs_ref, gfin_ref, yb_ref,
                    yp_ref, ys_ref, buf_ref, sems):
    i = pl.program_id(0)
    slot = i % 2

    def gather(da_ref, db_ref, s):
        def body(t, c):
            _row_copy(yb_ref, da_ref[0, t], buf_ref.at[s, 0], t, sems.at[s]).start()
            _row_copy(yb_ref, db_ref[0, t], buf_ref.at[s, 1], t, sems.at[s]).start()
            return c
        lax.fori_loop(0, TL, body, 0, unroll=DMA_UNROLL)

    @pl.when(i == 0)
    def _():
        gather(d0_ref, d1_ref, slot)

    @pl.when(i + 1 < NT_TOTAL)
    def _():
        gather(d0n_ref, d1n_ref, 1 - slot)

    def drain(t, c):
        _row_copy(yb_ref, 0, buf_ref.at[slot, 0], t, sems.at[slot]).wait()
        _row_copy(yb_ref, 0, buf_ref.at[slot, 1], t, sems.at[slot]).wait()
        return c

    lax.fori_loop(0, TL, drain, 0, unroll=DMA_UNROLL)

    def finish(h, pay, out_ref):
        y0 = buf_ref[slot, 0].reshape(TL, D_MODEL)
        y1 = buf_ref[slot, 1].reshape(TL, D_MODEL)
        moe = y0 * pay[:, PAY_W0:PAY_W0 + 1] + y1 * pay[:, PAY_W1:PAY_W1 + 1]
        out_ref[...] = _rms(h + moe, gfin_ref[...])

    @pl.when(i < NT_PROMPT)
    def _():
        finish(hp_ref[...], payp_ref[...], yp_ref)

    @pl.when(i == NT_PROMPT)
    def _():
        finish(hs_ref[...], pays_ref[...], ys_ref)


def _const_spec(shape):
    zeros = (0,) * len(shape)
    return pl.BlockSpec(shape, lambda *_: zeros, pipeline_mode=pl.Buffered(1))


def kernel(x_prompt, x_sample, state_pool, g_mix, w_in, w_pool, pool_scale, g_v, w_spatial, b_spatial,
           w_up_pool, w_up_gmlp, w_out, g_ffn, w_group, b_group, w_router, b_router, w_gate, w_up,
           w_down, g_final):
    gmix = g_mix[0][None, :]
    win = w_in[0].astype(BF16)
    wpool = w_pool[0].astype(BF16)
    pscale = pool_scale[0][None, :]
    gv = g_v[0][None, :]
    ws = w_spatial[0]
    bsp_t = b_spatial[0].T
    wupp = w_up_pool[0].astype(BF16)
    wupg = w_up_gmlp[0].astype(BF16)
    wout = w_out[0].astype(BF16)
    gffn = g_ffn[0][None, :]
    n_route = N_GROUPS + N_EXPERTS
    wrt = jnp.pad(jnp.concatenate([w_group[0], w_router[0]], axis=1), ((0, 0), (0, LANES - n_route)))
    brt = jnp.pad(jnp.concatenate([b_group[0], b_router[0]]), (0, LANES - n_route))[None, :]
    wg = w_gate[0].astype(BF16)
    wu = w_up[0].astype(BF16)
    wd = w_down[0].astype(BF16)
    gfin = g_final[None, :]
    bsp_prompt = jnp.tile(bsp_t, (TL // GMLP_CHUNK, 1))
    bsp_sample = jnp.tile(bsp_t[:DEC_SEQ], (DEC_BATCH, 1))
    hist = jnp.pad(state_pool[0], ((0, 0), (1, 0), (0, 0)))

    weight_specs = [
        _const_spec((1, D_MODEL)),
        _const_spec((D_MODEL, IN_COLS)),
        _const_spec((4, POOL_GROUP_DIM, POOL_GROUP_DIM)),
        _const_spec((1, POOL_WIDTH)),
        _const_spec((1, GMLP_WIDTH)),
        _const_spec((GMLP_HEADS, GMLP_CHUNK, GMLP_CHUNK)),
        _const_spec((TL, GMLP_HEADS)),
        _const_spec((POOL_WIDTH, D_MODEL)),
        _const_spec((GMLP_WIDTH, D_MODEL)),
        _const_spec((D_MODEL, D_MODEL)),
        _const_spec((1, D_MODEL)),
        _const_spec((D_MODEL, LANES)),
        _const_spec((1, LANES)),
    ]
    any_spec = pl.BlockSpec(memory_space=pl.ANY)
    route_scratch = [
        pltpu.VMEM((TL, TL), BF16),
        pltpu.VMEM((1, LANES), F32),
    ]

    h_p, rec_p, pay_p, pool_p, cnt_p = pl.pallas_call(
        _mixer_prompt_kernel,
        grid=(BATCH, TILES_PER_SEQ),
        in_specs=[pl.BlockSpec((None, TL, D_MODEL), lambda b, j: (b, j, 0))] + weight_specs,
        out_specs=[
            pl.BlockSpec((None, TL, D_MODEL), lambda b, j: (b, j, 0)),
            pl.BlockSpec((None, REC_ROWS, TL), lambda b, j: (b * TILES_PER_SEQ + j, 0, 0)),
            pl.BlockSpec((TL, LANES), lambda b, j: (b * TILES_PER_SEQ + j, 0)),
            pl.BlockSpec((None, HIST_ROWS, POOL_WIDTH), lambda b, j: (b, 0, 0)),
            pl.BlockSpec((1, LANES), lambda b, j: (0, 0)),
        ],
        out_shape=[
            jax.ShapeDtypeStruct((BATCH, SEQ, D_MODEL), F32),
            jax.ShapeDtypeStruct((NT_PROMPT, REC_ROWS, TL), I32),
            jax.ShapeDtypeStruct((T_PROMPT, LANES), F32),
            jax.ShapeDtypeStruct((BATCH, HIST_ROWS, POOL_WIDTH), F32),
            jax.ShapeDtypeStruct((1, LANES), F32),
        ],
        scratch_shapes=[
            pltpu.VMEM((HIST_ROWS + TL, POOL_WIDTH), F32),
            pltpu.VMEM((TL, GMLP_WIDTH), F32),
        ] + route_scratch,
        compiler_params=pltpu.CompilerParams(
            dimension_semantics=("arbitrary", "arbitrary"), vmem_limit_bytes=VMEM_LIMIT),
        name="mixer_prompt",
    )(x_prompt, gmix, win, wpool, pscale, gv, ws, bsp_prompt, wupp, wupg, wout, gffn, wrt, brt)

    h_s, rec_s, pay_s, pool_s, vn_s, cnt = pl.pallas_call(
        _mixer_sample_kernel,
        grid=(1,),
        in_specs=[
            _const_spec((TL, D_MODEL)),
            _const_spec((DEC_BATCH, HIST_ROWS, POOL_WIDTH)),
            _const_spec((1, LANES)),
        ] + weight_specs,
        out_specs=[
            _const_spec((TL, D_MODEL)),
            pl.BlockSpec((None, REC_ROWS, TL), lambda i: (0, 0, 0)),
            _const_spec((TL, LANES)),
            _const_spec((DEC_BATCH, HIST_ROWS, POOL_WIDTH)),
            _const_spec((TL, GMLP_WIDTH)),
            _const_spec((1, LANES)),
        ],
        out_shape=[
            jax.ShapeDtypeStruct((T_SAMPLE, D_MODEL), F32),
            jax.ShapeDtypeStruct((1, REC_ROWS, TL), I32),
            jax.ShapeDtypeStruct((T_SAMPLE, LANES), F32),
            jax.ShapeDtypeStruct((DEC_BATCH, HIST_ROWS, POOL_WIDTH), F32),
            jax.ShapeDtypeStruct((T_SAMPLE, GMLP_WIDTH), F32),
            jax.ShapeDtypeStruct((1, LANES), F32),
        ],
        scratch_shapes=[
            pltpu.VMEM((DEC_BATCH, HIST_ROWS + DEC_SEQ, POOL_WIDTH), F32),
        ] + route_scratch,
        compiler_params=pltpu.CompilerParams(
            dimension_semantics=("arbitrary",), vmem_limit_bytes=VMEM_LIMIT),
        name="mixer_sample",
    )(x_sample.reshape(T_SAMPLE, D_MODEL), hist, cnt_p, gmix, win, wpool, pscale, gv, ws, bsp_sample,
      wupp, wupg, wout, gffn, wrt, brt)

    counts = cnt[0, :N_EXPERTS].astype(I32)
    nblk = (counts + BM - 1) // BM
    bend = jnp.cumsum(nblk)
    bstart = bend - nblk
    nvalid = bend[-1:]
    step = jnp.arange(NB_MAX, dtype=I32)
    blk_e = jnp.minimum(jnp.sum(step[:, None] >= bend[None, :], axis=1), N_EXPERTS - 1).astype(I32)
    pad_start = bstart * BM + counts
    pad_len = nblk * BM - counts
    rec = jnp.concatenate([rec_p, rec_s], axis=0)
    expert_ids = jnp.arange(N_EXPERTS, dtype=I32)[None, None, :]

    def slots(e, r):
        first = jnp.sum(jnp.where(e[:, :, None] == expert_ids, (bstart * BM)[None, None, :], 0), axis=-1)
        return (first + r)[:, None, :]

    d0 = slots(rec[:, REC_E0], rec[:, REC_R0])
    d1 = slots(rec[:, REC_E1], rec[:, REC_R1])

    last_p = NT_PROMPT - 1

    def hp_map(i, *_):
        ip = jnp.minimum(i, last_p)
        return (ip // TILES_PER_SEQ, ip % TILES_PER_SEQ, 0)

    def smem_tile(index_map):
        return pl.BlockSpec((None, 1, TL), index_map, memory_space=pltpu.SMEM)

    xb = pl.pallas_call(
        _dispatch_kernel,
        grid_spec=pltpu.PrefetchScalarGridSpec(
            num_scalar_prefetch=3,
            grid=(NT_TOTAL,),
            in_specs=[
                smem_tile(lambda i, *_: (i, 0, 0)),
                smem_tile(lambda i, *_: (i, 0, 0)),
                pl.BlockSpec((None, TL, D_MODEL), hp_map),
                pl.BlockSpec((TL, D_MODEL), lambda i, *_: (0, 0)),
                pl.BlockSpec((1, D_MODEL), lambda i, *_: (0, 0)),
            ],
            out_specs=any_spec,
            scratch_shapes=[
                pltpu.VMEM((2, TL, SUBLANES, LANES), F32),
                pltpu.VMEM((ZERO_ROWS, SUBLANES, LANES), F32),
                pltpu.SemaphoreType.DMA((2,)),
            ],
        ),
        out_shape=jax.ShapeDtypeStruct((N_SLOTS, SUBLANES, LANES), F32),
        compiler_params=pltpu.CompilerParams(
            dimension_semantics=("arbitrary",), vmem_limit_bytes=VMEM_LIMIT),
        name="dispatch",
    )(pad_start, pad_len, nvalid, d0, d1, h_p, h_s, gffn)

    def xb_map(i, be, nv):
        return (jnp.minimum(i, nv[0] - 1), 0, 0)

    yb = pl.pallas_call(
        _experts_kernel,
        grid_spec=pltpu.PrefetchScalarGridSpec(
            num_scalar_prefetch=2,
            grid=(NB_MAX,),
            in_specs=[
                pl.BlockSpec((BM, SUBLANES, LANES), xb_map),
                pl.BlockSpec((None, D_MODEL, D_EXPERT), lambda i, be, nv: (be[i], 0, 0)),
                pl.BlockSpec((None, D_MODEL, D_EXPERT), lambda i, be, nv: (be[i], 0, 0)),
                pl.BlockSpec((None, D_EXPERT, D_MODEL), lambda i, be, nv: (be[i], 0, 0)),
            ],
            out_specs=pl.BlockSpec((BM, SUBLANES, LANES), lambda i, be, nv: (i, 0, 0)),
        ),
        out_shape=jax.ShapeDtypeStruct((N_SLOTS, SUBLANES, LANES), F32),
        compiler_params=pltpu.CompilerParams(
            dimension_semantics=("arbitrary",), vmem_limit_bytes=VMEM_LIMIT),
        name="experts",
    )(blk_e, nvalid, xb, wg, wu, wd)

    def next_tile(i):
        return (jnp.minimum(i + 1, NT_TOTAL - 1), 0, 0)

    y_p, y_s = pl.pallas_call(
        _combine_kernel,
        grid=(NT_TOTAL,),
        in_specs=[
            smem_tile(lambda i: (i, 0, 0)),
            smem_tile(lambda i: (i, 0, 0)),
            smem_tile(next_tile),
            smem_tile(next_tile),
            pl.BlockSpec((TL, LANES), lambda i: (jnp.minimum(i, last_p), 0)),
            pl.BlockSpec((TL, LANES), lambda i: (0, 0)),
            pl.BlockSpec((None, TL, D_MODEL), hp_map),
            pl.BlockSpec((TL, D_MODEL), lambda i: (0, 0)),
            pl.BlockSpec((1, D_MODEL), lambda i: (0, 0)),
            any_spec,
        ],
        out_specs=[
            pl.BlockSpec((None, TL, D_MODEL), hp_map),
            pl.BlockSpec((TL, D_MODEL), lambda i: (0, 0)),
        ],
        out_shape=[
            jax.ShapeDtypeStruct((BATCH, SEQ, D_MODEL), F32),
            jax.ShapeDtypeStruct((T_SAMPLE, D_MODEL), F32),
        ],
        scratch_shapes=[
            pltpu.VMEM((2, 2, TL, SUBLANES, LANES), F32),
            pltpu.SemaphoreType.DMA((2,)),
        ],
        compiler_params=pltpu.CompilerParams(
            dimension_semantics=("arbitrary",), vmem_limit_bytes=VMEM_LIMIT),
        name="combine",
    )(d0, d1, d0, d1, pay_p, pay_s, h_p, h_s, gfin, yb)

    y_sample = y_s.reshape(DEC_BATCH, DEC_SEQ, D_MODEL)
    new_pool_prompt = pool_p[None, :, 1:, :]
    new_pool_sample = pool_s[None, :, 1:, :]
    new_gmlp_v_sample = vn_s.reshape(1, DEC_BATCH, DEC_SEQ, GMLP_WIDTH)
    return (y_p, y_sample, new_pool_prompt, new_pool_sample, new_gmlp_v_sample)
```

```python
import jax
import jax.numpy as jnp
from jax import lax
from jax.experimental import pallas as pl
from jax.experimental.pallas import tpu as pltpu

F32 = jnp.float32
BF16 = jnp.bfloat16
I32 = jnp.int32

D_MODEL = 1024
SEQ = 16384
BATCH = 2
DEC_BATCH = 16
DEC_SEQ = 32
PAST_LEN = 1024
POOL_WIDTH = 512
POOL_GROUP_DIM = 128
POOL_WINDOWS = (2, 4, 8, 16)
HIST_ROWS = 16
GMLP_WIDTH = 512
GMLP_HEADS = 4
GMLP_HEAD_DIM = 128
GMLP_CHUNK = 128
IN_COLS = POOL_WIDTH + 2 * GMLP_WIDTH + 2 * D_MODEL
N_GROUPS = 4
EXPERTS_PER_GROUP = 4
N_EXPERTS = 16
D_EXPERT = 512
EPS = 1e-6

LANES = 128
SUBLANES = 8
assert D_MODEL == SUBLANES * LANES
TL = 512
BM = 512
T_PROMPT = BATCH * SEQ
T_SAMPLE = DEC_BATCH * DEC_SEQ
T_TOTAL = T_PROMPT + T_SAMPLE
NT_PROMPT = T_PROMPT // TL
TILES_PER_SEQ = SEQ // TL
NT_TOTAL = T_TOTAL // TL
assert T_SAMPLE == TL
N_ASSIGN = 2 * T_TOTAL
NB_MIN = N_ASSIGN // BM
NB_MAX = (N_ASSIGN + N_EXPERTS * (BM - 1)) // BM
N_SLOTS = NB_MAX * BM
ZERO_ROWS = BM // 2
DMA_UNROLL = 8
VMEM_LIMIT = 56 * 1024 * 1024

REC_E0, REC_E1, REC_R0, REC_R1 = 0, 1, 2, 3
REC_ROWS = 8
PAY_W0, PAY_W1 = 0, 1


def _dot(a, b):
    return jnp.dot(a, b, preferred_element_type=F32)


def _rms(x, g):
    return x * lax.rsqrt(jnp.mean(x * x, axis=-1, keepdims=True) + EPS) * g


def _split_bf16(x):
    hi = x.astype(BF16)
    lo = (x - hi.astype(F32)).astype(BF16)
    return hi, lo


def _row_copy(src_ref, s, dst_ref, d, sem):
    return pltpu.make_async_copy(src_ref.at[pl.ds(s, 1)], dst_ref.at[pl.ds(d, 1)], sem)


def _pool_windows(load_shifted, a, pos, wpool_ref, pscale_ref):
    outs = []
    for g, w in enumerate(POOL_WINDOWS):
        cols = slice(g * POOL_GROUP_DIM, (g + 1) * POOL_GROUP_DIM)
        acc = a[:, cols]
        for k in range(1, w):
            acc = acc + load_shifted(k, g)
        cnt = jnp.minimum(pos + 1, w).astype(F32)
        d = (acc / cnt - a[:, cols]).astype(BF16)
        outs.append(_dot(d, wpool_ref[g]))
    return jnp.concatenate(outs, axis=-1) * pscale_ref[...]


def _merge(pa, pb, ga, gb, wupp_ref, wupg_ref):
    return (jax.nn.sigmoid(ga) * _dot(pa.astype(BF16), wupp_ref[...])
            + jax.nn.sigmoid(gb) * _dot(pb.astype(BF16), wupg_ref[...]))


def _init_ltri(ltri_ref):
    r = lax.broadcasted_iota(I32, (TL, TL), 0)
    c = lax.broadcasted_iota(I32, (TL, TL), 1)
    ltri_ref[...] = jnp.where(c < r, 1.0, 0.0).astype(BF16)


def _router_logits(hn, wrt_ref, brt_ref):
    hi, lo = _split_bf16(hn)
    whi, wlo = _split_bf16(wrt_ref[...])
    return _dot(hi, whi) + _dot(lo, whi) + _dot(hi, wlo) + brt_ref[...]


def _assign(lg, ltri_ref, run_ref, rec_out_ref, pay_out_ref, counted=True):
    lane = lax.broadcasted_iota(I32, lg.shape, 1)
    neg = jnp.float32(-jnp.inf)
    big = jnp.int32(1 << 20)

    def first_argmax(v, vmax):
        return jnp.min(jnp.where(v == vmax, lane, big), axis=-1, keepdims=True)

    gmask = lane < N_GROUPS
    glm = jnp.where(gmask, lg, neg)
    gmax = jnp.max(glm, axis=-1, keepdims=True)
    grp = first_argmax(glm, gmax)
    p_g = 1.0 / jnp.sum(jnp.where(gmask, jnp.exp(lg - gmax), 0.0), axis=-1, keepdims=True)

    lo_lane = N_GROUPS + EXPERTS_PER_GROUP * grp
    elm = jnp.where((lane >= lo_lane) & (lane < lo_lane + EXPERTS_PER_GROUP), lg, neg)
    v1 = jnp.max(elm, axis=-1, keepdims=True)
    i1 = first_argmax(elm, v1)
    elm2 = jnp.where(lane == i1, neg, elm)
    v2 = jnp.max(elm2, axis=-1, keepdims=True)
    i2 = first_argmax(elm2, v2)
    ex2 = jnp.exp(v2 - v1)
    inv = 1.0 / (1.0 + ex2)
    w0 = inv * p_g
    w1 = ex2 * inv * p_g
    e0 = i1 - N_GROUPS
    e1 = i2 - N_GROUPS

    sel0 = lane == e0
    sel1 = lane == e1
    onehot = jnp.where(sel0 | sel1, 1.0, 0.0)
    before = _dot(ltri_ref[...], onehot.astype(BF16)) + run_ref[...]
    r0 = jnp.sum(jnp.where(sel0, before, 0.0), axis=-1, keepdims=True)
    r1 = jnp.sum(jnp.where(sel1, before, 0.0), axis=-1, keepdims=True)
    run_ref[...] = run_ref[...] + jnp.where(counted, jnp.sum(onehot, axis=0, keepdims=True), 0.0)

    rec = jnp.where(lane == REC_E0, e0.astype(F32),
                    jnp.where(lane == REC_E1, e1.astype(F32),
                              jnp.where(lane == REC_R0, r0,
                                        jnp.where(lane == REC_R1, r1, 0.0))))
    rec_out_ref[...] = rec.T[0:REC_ROWS, :].astype(I32)
    pay_out_ref[...] = jnp.where(lane == PAY_W0, w0, jnp.where(lane == PAY_W1, w1, 0.0))


def _mixer_prompt_kernel(x_ref, xprev_ref, gmix_ref, win_ref, wpool_ref, pscale_ref, gv_ref, ws_ref,
                         bsp_ref, wupp_ref, wupg_ref, wout_ref, gffn_ref, wrt_ref, brt_ref,
                         h_ref, rec_out_ref, pay_out_ref, pool_ref, cnt_ref,
                         aext_ref, s_ref, ltri_ref, run_ref, merged_ref):
    s = pl.program_id(0)

    @pl.when(s == 0)
    def _():
        _init_ltri(ltri_ref)
        run_ref[...] = jnp.zeros_like(run_ref)
        aext_ref[TL:TL + HIST_ROWS, :] = jnp.zeros((HIST_ROWS, POOL_WIDTH), F32)
        merged_ref[...] = jnp.zeros_like(merged_ref)

    j = jnp.minimum(s, NT_PROMPT - 1) % TILES_PER_SEQ

    h = xprev_ref[...] + _dot(merged_ref[...], wout_ref[...])
    xn = _rms(x_ref[...], gmix_ref[...]).astype(BF16)
    a = _dot(xn, win_ref[:, 0:512])
    h_ref[...] = h
    hn = _rms(h, gffn_ref[...])
    lg = _router_logits(hn, wrt_ref, brt_ref)

    aext_ref[0:HIST_ROWS, :] = jnp.where(j > 0, aext_ref[TL:TL + HIST_ROWS, :], 0.0)
    aext_ref[HIST_ROWS:HIST_ROWS + TL, :] = a
    pool_ref[...] = a[TL - HIST_ROWS:TL, :]
    pos = j * TL + lax.broadcasted_iota(I32, (TL, 1), 0)

    def load_shifted(k, g):
        return aext_ref[HIST_ROWS - k:HIST_ROWS - k + TL, g * POOL_GROUP_DIM:(g + 1) * POOL_GROUP_DIM]

    v = _dot(xn, win_ref[:, 1024:1536])
    pa = _pool_windows(load_shifted, a, pos, wpool_ref, pscale_ref)
    u = _dot(xn, win_ref[:, 512:1024])
    _assign(lg, ltri_ref, run_ref, rec_out_ref, pay_out_ref, counted=s > 0)
    cnt_ref[...] = run_ref[...]
    ga = _dot(xn, win_ref[:, 1536:2560])

    vn = _rms(v, gv_ref[...]).astype(BF16)
    rr = lax.broadcasted_iota(I32, (GMLP_CHUNK, GMLP_CHUNK), 0)
    cc = lax.broadcasted_iota(I32, (GMLP_CHUNK, GMLP_CHUNK), 1)
    for hd in range(GMLP_HEADS):
        cols = slice(hd * GMLP_HEAD_DIM, (hd + 1) * GMLP_HEAD_DIM)
        wsm = jnp.where(cc <= rr, ws_ref[hd], 0.0).astype(BF16)
        for c in range(TL // GMLP_CHUNK):
            rows = slice(c * GMLP_CHUNK, (c + 1) * GMLP_CHUNK)
            s_ref[rows, cols] = _dot(wsm, vn[rows, cols]) + bsp_ref[rows, hd:hd + 1]
    pb = u * s_ref[...]
    gb = _dot(xn, win_ref[:, 2560:3584])

    merged_ref[...] = _merge(pa, pb, ga, gb, wupp_ref, wupg_ref).astype(BF16)


def _mixer_sample_kernel(x_ref, hist_ref, run0_ref, gmix_ref, win_ref, wpool_ref, pscale_ref, gv_ref,
                         ws_ref, bsp_ref, wupp_ref, wupg_ref, wout_ref, gffn_ref, wrt_ref, brt_ref,
                         h_ref, rec_out_ref, pay_out_ref, pool_ref, vn_ref, cnt_ref,
                         aext_ref, ltri_ref, run_ref):
    _init_ltri(ltri_ref)
    run_ref[...] = run0_ref[...]

    x = x_ref[...]
    xn = _rms(x, gmix_ref[...]).astype(BF16)
    a = _dot(xn, win_ref[:, 0:512])
    u = _dot(xn, win_ref[:, 512:1024])
    v = _dot(xn, win_ref[:, 1024:1536])
    ga = _dot(xn, win_ref[:, 1536:2560])
    gb = _dot(xn, win_ref[:, 2560:3584])

    a3 = a.reshape(DEC_BATCH, DEC_SEQ, POOL_WIDTH)
    aext_ref[:, 0:HIST_ROWS, :] = hist_ref[...]
    aext_ref[:, HIST_ROWS:HIST_ROWS + DEC_SEQ, :] = a3
    pool_ref[...] = a3[:, DEC_SEQ - HIST_ROWS:DEC_SEQ, :]
    row = lax.broadcasted_iota(I32, (TL, 1), 0)
    pos = PAST_LEN + row % DEC_SEQ

    def load_shifted(k, g):
        sl = aext_ref[:, HIST_ROWS - k:HIST_ROWS - k + DEC_SEQ, g * POOL_GROUP_DIM:(g + 1) * POOL_GROUP_DIM]
        return sl.reshape(TL, POOL_GROUP_DIM)

    pa = _pool_windows(load_shifted, a, pos, wpool_ref, pscale_ref)

    vnf = _rms(v, gv_ref[...])
    vn_ref[...] = vnf
    vn = vnf.astype(BF16)
    rsel = (lax.broadcasted_iota(I32, (TL, GMLP_CHUNK), 1)
            == lax.broadcasted_iota(I32, (TL, GMLP_CHUNK), 0) % DEC_SEQ)
    rsel_b = jnp.where(rsel, 1.0, 0.0).astype(BF16)
    csel = (lax.broadcasted_iota(I32, (GMLP_CHUNK, TL), 0)
            == lax.broadcasted_iota(I32, (GMLP_CHUNK, TL), 1) % DEC_SEQ)
    csel_b = jnp.where(csel, 1.0, 0.0).astype(BF16)
    rr = lax.broadcasted_iota(I32, (TL, TL), 0)
    cc = lax.broadcasted_iota(I32, (TL, TL), 1)
    keep = (rr // DEC_SEQ == cc // DEC_SEQ) & (cc <= rr)
    s_parts = []
    for hd in range(GMLP_HEADS):
        cols = slice(hd * GMLP_HEAD_DIM, (hd + 1) * GMLP_HEAD_DIM)
        wrow = _dot(rsel_b, ws_ref[hd].astype(BF16)).astype(BF16)
        wfull = _dot(wrow, csel_b)
        wblk = jnp.where(keep, wfull, 0.0).astype(BF16)
        s_parts.append(_dot(wblk, vn[:, cols]) + bsp_ref[:, hd:hd + 1])
    pb = u * jnp.concatenate(s_parts, axis=-1)

    h = x + _dot(_merge(pa, pb, ga, gb, wupp_ref, wupg_ref).astype(BF16), wout_ref[...])
    h_ref[...] = h
    lg = _router_logits(_rms(h, gffn_ref[...]), wrt_ref, brt_ref)
    _assign(lg, ltri_ref, run_ref, rec_out_ref, pay_out_ref)
    cnt_ref[...] = run_ref[...]


def _dispatch_kernel(pad_start_ref, pad_len_ref, nvalid_ref,
                     d0_ref, d1_ref, hp_ref, hs_ref, gffn_ref, xb_ref, row_ref, zero_ref, sems):
    i = pl.program_id(0)
    slot = i % 2
    rows = row_ref.at[slot]

    def fill(h):
        rows[...] = _rms(h, gffn_ref[...]).reshape(TL, SUBLANES, LANES)

    @pl.when(i < NT_PROMPT)
    def _():
        fill(hp_ref[...])

    @pl.when(i == NT_PROMPT)
    def _():
        fill(hs_ref[...])

    def issue(t, c):
        _row_copy(rows, t, xb_ref, d0_ref[0, t], sems.at[slot]).start(priority=0)
        _row_copy(rows, t, xb_ref, d1_ref[0, t], sems.at[slot]).start(priority=1)
        return c

    lax.fori_loop(0, TL, issue, 0, unroll=DMA_UNROLL)

    def drain(s):
        def body(t, c):
            _row_copy(row_ref.at[s], t, xb_ref, 0, sems.at[s]).wait()
            _row_copy(row_ref.at[s], t, xb_ref, 0, sems.at[s]).wait()
            return c
        lax.fori_loop(0, TL, body, 0, unroll=DMA_UNROLL)

    @pl.when(i > 0)
    def _():
        drain(1 - slot)

    @pl.when(i == NT_TOTAL - 1)
    def _():
        drain(slot)
        zero_ref[...] = jnp.zeros_like(zero_ref)
        bits = [1 << k for k in reversed(range(BM.bit_length() - 1))]

        def pad_copies(fn):
            for e in range(N_EXPERTS):
                off = pad_start_ref[e]
                n = pad_len_ref[e]
                for bit in bits:
                    take = jnp.bitwise_and(n, bit)

                    @pl.when(take != 0)
                    def _(off=off, bit=bit):
                        fn(pltpu.make_async_copy(zero_ref.at[pl.ds(0, bit)], xb_ref.at[pl.ds(off, bit)],
                                                 sems.at[slot]))
                    off = off + take
            for blk in range(NB_MIN, NB_MAX):
                @pl.when(blk >= nvalid_ref[0])
                def _(blk=blk):
                    for half in range(BM // ZERO_ROWS):
                        fn(pltpu.make_async_copy(
                            zero_ref, xb_ref.at[pl.ds(blk * BM + half * ZERO_ROWS, ZERO_ROWS)], sems.at[slot]))

        pad_copies(lambda cp: cp.start())
        pad_copies(lambda cp: cp.wait())


def _experts_kernel(blk_e_ref, nvalid_ref, xb_ref, wg_ref, wu_ref, wd_ref, yb_ref, wgb_ref, wub_ref, wdb_ref):
    i = pl.program_id(0)

    @pl.when((i == 0) | (blk_e_ref[i] != blk_e_ref[jnp.maximum(i - 1, 0)]))
    def _():
        wgb_ref[...] = wg_ref[...].astype(BF16)
        wub_ref[...] = wu_ref[...].astype(BF16)
        wdb_ref[...] = wd_ref[...].astype(BF16)

    @pl.when(i < nvalid_ref[0])
    def _():
        x = xb_ref[...].reshape(BM, D_MODEL).astype(BF16)
        hg = _dot(x, wgb_ref[...])
        hu = _dot(x, wub_ref[...])
        act = (hg * jax.nn.sigmoid(hg)) * hu
        yb_ref[...] = _dot(act.astype(BF16), wdb_ref[...]).reshape(BM, SUBLANES, LANES)

    @pl.when(i >= nvalid_ref[0])
    def _():
        yb_ref[...] = jnp.zeros_like(yb_ref)


def _combine_kernel(d0_ref, d1_ref, d0n_ref, d1n_ref, payp_ref, pays_ref, hp_ref, hs_ref, gfin_ref, yb_ref,
                    yp_ref, ys_ref, buf_ref, sems):
    i = pl.program_id(0)
    slot = i % 2

    def gather(da_ref, db_ref, s):
        def body(t, c):
            _row_copy(yb_ref, da_ref[0, t], buf_ref.at[s, 0], t, sems.at[s]).start(priority=0)
            _row_copy(yb_ref, db_ref[0, t], buf_ref.at[s, 1], t, sems.at[s]).start(priority=1)
            return c
        lax.fori_loop(0, TL, body, 0, unroll=DMA_UNROLL)

    @pl.when(i == 0)
    def _():
        gather(d0_ref, d1_ref, slot)

    @pl.when(i + 1 < NT_TOTAL)
    def _():
        gather(d0n_ref, d1n_ref, 1 - slot)

    def drain(t, c):
        _row_copy(yb_ref, 0, buf_ref.at[slot, 0], t, sems.at[slot]).wait()
        _row_copy(yb_ref, 0, buf_ref.at[slot, 1], t, sems.at[slot]).wait()
        return c

    lax.fori_loop(0, TL, drain, 0, unroll=DMA_UNROLL)

    def finish(h, pay, out_ref):
        y0 = buf_ref[slot, 0].reshape(TL, D_MODEL)
        y1 = buf_ref[slot, 1].reshape(TL, D_MODEL)
        moe = y0 * pay[:, PAY_W0:PAY_W0 + 1] + y1 * pay[:, PAY_W1:PAY_W1 + 1]
        out_ref[...] = _rms(h + moe, gfin_ref[...])

    @pl.when(i < NT_PROMPT)
    def _():
        finish(hp_ref[...], payp_ref[...], yp_ref)

    @pl.when(i == NT_PROMPT)
    def _():
        finish(hs_ref[...], pays_ref[...], ys_ref)


def _const_spec(shape):
    zeros = (0,) * len(shape)
    return pl.BlockSpec(shape, lambda *_: zeros, pipeline_mode=pl.Buffered(1))


def kernel(x_prompt, x_sample, state_pool, g_mix, w_in, w_pool, pool_scale, g_v, w_spatial, b_spatial,
           w_up_pool, w_up_gmlp, w_out, g_ffn, w_group, b_group, w_router, b_router, w_gate, w_up,
           w_down, g_final):
    gmix = g_mix[0][None, :]
    win = w_in[0].astype(BF16)
    wpool = w_pool[0].astype(BF16)
    pscale = pool_scale[0][None, :]
    gv = g_v[0][None, :]
    ws = w_spatial[0]
    bsp_t = b_spatial[0].T
    wupp = w_up_pool[0].astype(BF16)
    wupg = w_up_gmlp[0].astype(BF16)
    wout = w_out[0].astype(BF16)
    gffn = g_ffn[0][None, :]
    n_route = N_GROUPS + N_EXPERTS
    wrt = jnp.pad(jnp.concatenate([w_group[0], w_router[0]], axis=1), ((0, 0), (0, LANES - n_route)))
    brt = jnp.pad(jnp.concatenate([b_group[0], b_router[0]]), (0, LANES - n_route))[None, :]
    wg = w_gate[0]
    wu = w_up[0]
    wd = w_down[0]
    gfin = g_final[None, :]
    bsp_prompt = jnp.tile(bsp_t, (TL // GMLP_CHUNK, 1))
    bsp_sample = jnp.tile(bsp_t[:DEC_SEQ], (DEC_BATCH, 1))
    hist = jnp.pad(state_pool[0], ((0, 0), (1, 0), (0, 0)))

    weight_specs = [
        _const_spec((1, D_MODEL)),
        _const_spec((D_MODEL, IN_COLS)),
        _const_spec((4, POOL_GROUP_DIM, POOL_GROUP_DIM)),
        _const_spec((1, POOL_WIDTH)),
        _const_spec((1, GMLP_WIDTH)),
        _const_spec((GMLP_HEADS, GMLP_CHUNK, GMLP_CHUNK)),
        _const_spec((TL, GMLP_HEADS)),
        _const_spec((POOL_WIDTH, D_MODEL)),
        _const_spec((GMLP_WIDTH, D_MODEL)),
        _const_spec((D_MODEL, D_MODEL)),
        _const_spec((1, D_MODEL)),
        _const_spec((D_MODEL, LANES)),
        _const_spec((1, LANES)),
    ]
    any_spec = pl.BlockSpec(memory_space=pl.ANY)
    route_scratch = [
        pltpu.VMEM((TL, TL), BF16),
        pltpu.VMEM((1, LANES), F32),
    ]

    def started(s):
        return jnp.minimum(s, NT_PROMPT - 1)

    def finished(s):
        return jnp.maximum(s - 1, 0)

    x_tiles = x_prompt.reshape(NT_PROMPT, TL, D_MODEL)
    h_p, rec_p, pay_p, pool_p, cnt_p = pl.pallas_call(
        _mixer_prompt_kernel,
        grid=(NT_PROMPT + 1,),
        in_specs=[
            pl.BlockSpec((None, TL, D_MODEL), lambda s: (started(s), 0, 0)),
            pl.BlockSpec((None, TL, D_MODEL), lambda s: (finished(s), 0, 0)),
        ] + weight_specs,
        out_specs=[
            pl.BlockSpec((None, TL, D_MODEL), lambda s: (finished(s), 0, 0)),
            pl.BlockSpec((None, REC_ROWS, TL), lambda s: (finished(s), 0, 0)),
            pl.BlockSpec((TL, LANES), lambda s: (finished(s), 0)),
            pl.BlockSpec((None, HIST_ROWS, POOL_WIDTH), lambda s: (started(s) // TILES_PER_SEQ, 0, 0)),
            pl.BlockSpec((1, LANES), lambda s: (0, 0)),
        ],
        out_shape=[
            jax.ShapeDtypeStruct((NT_PROMPT, TL, D_MODEL), F32),
            jax.ShapeDtypeStruct((NT_PROMPT, REC_ROWS, TL), I32),
            jax.ShapeDtypeStruct((T_PROMPT, LANES), F32),
            jax.ShapeDtypeStruct((BATCH, HIST_ROWS, POOL_WIDTH), F32),
            jax.ShapeDtypeStruct((1, LANES), F32),
        ],
        scratch_shapes=[
            pltpu.VMEM((HIST_ROWS + TL, POOL_WIDTH), F32),
            pltpu.VMEM((TL, GMLP_WIDTH), F32),
        ] + route_scratch + [pltpu.VMEM((TL, D_MODEL), BF16)],
        compiler_params=pltpu.CompilerParams(
            dimension_semantics=("arbitrary",), vmem_limit_bytes=VMEM_LIMIT),
        name="mixer_prompt",
    )(x_tiles, x_tiles, gmix, win, wpool, pscale, gv, ws, bsp_prompt, wupp, wupg, wout, gffn, wrt, brt)

    h_s, rec_s, pay_s, pool_s, vn_s, cnt = pl.pallas_call(
        _mixer_sample_kernel,
        grid=(1,),
        in_specs=[
            _const_spec((TL, D_MODEL)),
            _const_spec((DEC_BATCH, HIST_ROWS, POOL_WIDTH)),
            _const_spec((1, LANES)),
        ] + weight_specs,
        out_specs=[
            _const_spec((TL, D_MODEL)),
            pl.BlockSpec((None, REC_ROWS, TL), lambda i: (0, 0, 0)),
            _const_spec((TL, LANES)),
            _const_spec((DEC_BATCH, HIST_ROWS, POOL_WIDTH)),
            _const_spec((TL, GMLP_WIDTH)),
            _const_spec((1, LANES)),
        ],
        out_shape=[
            jax.ShapeDtypeStruct((T_SAMPLE, D_MODEL), F32),
            jax.ShapeDtypeStruct((1, REC_ROWS, TL), I32),
            jax.ShapeDtypeStruct((T_SAMPLE, LANES), F32),
            jax.ShapeDtypeStruct((DEC_BATCH, HIST_ROWS, POOL_WIDTH), F32),
            jax.ShapeDtypeStruct((T_SAMPLE, GMLP_WIDTH), F32),
            jax.ShapeDtypeStruct((1, LANES), F32),
        ],
        scratch_shapes=[
            pltpu.VMEM((DEC_BATCH, HIST_ROWS + DEC_SEQ, POOL_WIDTH), F32),
        ] + route_scratch,
        compiler_params=pltpu.CompilerParams(
            dimension_semantics=("arbitrary",), vmem_limit_bytes=VMEM_LIMIT),
        name="mixer_sample",
    )(x_sample.reshape(T_SAMPLE, D_MODEL), hist, cnt_p, gmix, win, wpool, pscale, gv, ws, bsp_sample,
      wupp, wupg, wout, gffn, wrt, brt)

    counts = cnt[0, :N_EXPERTS].astype(I32)
    nblk = (counts + BM - 1) // BM
    bend = jnp.cumsum(nblk)
    bstart = bend - nblk
    nvalid = bend[-1:]
    step = jnp.minimum(jnp.arange(NB_MAX, dtype=I32), nvalid - 1)
    blk_e = jnp.minimum(jnp.sum(step[:, None] >= bend[None, :], axis=1), N_EXPERTS - 1).astype(I32)
    pad_start = bstart * BM + counts
    pad_len = nblk * BM - counts
    rec = jnp.concatenate([rec_p, rec_s], axis=0)
    expert_ids = jnp.arange(N_EXPERTS, dtype=I32)[None, None, :]

    def slots(e, r):
        first = jnp.sum(jnp.where(e[:, :, None] == expert_ids, (bstart * BM)[None, None, :], 0), axis=-1)
        return (first + r)[:, None, :]

    d0 = slots(rec[:, REC_E0], rec[:, REC_R0])
    d1 = slots(rec[:, REC_E1], rec[:, REC_R1])

    last_p = NT_PROMPT - 1

    def hp_map(i, *_):
        return (jnp.minimum(i, last_p), 0, 0)

    def smem_tile(index_map):
        return pl.BlockSpec((None, 1, TL), index_map, memory_space=pltpu.SMEM)

    xb = pl.pallas_call(
        _dispatch_kernel,
        grid_spec=pltpu.PrefetchScalarGridSpec(
            num_scalar_prefetch=3,
            grid=(NT_TOTAL,),
            in_specs=[
                smem_tile(lambda i, *_: (i, 0, 0)),
                smem_tile(lambda i, *_: (i, 0, 0)),
                pl.BlockSpec((None, TL, D_MODEL), hp_map),
                pl.BlockSpec((TL, D_MODEL), lambda i, *_: (0, 0)),
                pl.BlockSpec((1, D_MODEL), lambda i, *_: (0, 0)),
            ],
            out_specs=any_spec,
            scratch_shapes=[
                pltpu.VMEM((2, TL, SUBLANES, LANES), F32),
                pltpu.VMEM((ZERO_ROWS, SUBLANES, LANES), F32),
                pltpu.SemaphoreType.DMA((2,)),
            ],
        ),
        out_shape=jax.ShapeDtypeStruct((N_SLOTS, SUBLANES, LANES), F32),
        compiler_params=pltpu.CompilerParams(
            dimension_semantics=("arbitrary",), vmem_limit_bytes=VMEM_LIMIT),
        name="dispatch",
    )(pad_start, pad_len, nvalid, d0, d1, h_p, h_s, gffn)

    def xb_map(i, be, nv):
        return (jnp.minimum(i, nv[0] - 1), 0, 0)

    yb = pl.pallas_call(
        _experts_kernel,
        grid_spec=pltpu.PrefetchScalarGridSpec(
            num_scalar_prefetch=2,
            grid=(NB_MAX,),
            in_specs=[
                pl.BlockSpec((BM, SUBLANES, LANES), xb_map),
                pl.BlockSpec((None, D_MODEL, D_EXPERT), lambda i, be, nv: (be[i], 0, 0)),
                pl.BlockSpec((None, D_MODEL, D_EXPERT), lambda i, be, nv: (be[i], 0, 0)),
                pl.BlockSpec((None, D_EXPERT, D_MODEL), lambda i, be, nv: (be[i], 0, 0)),
            ],
            out_specs=pl.BlockSpec((BM, SUBLANES, LANES), lambda i, be, nv: (i, 0, 0)),
            scratch_shapes=[
                pltpu.VMEM((D_MODEL, D_EXPERT), BF16),
                pltpu.VMEM((D_MODEL, D_EXPERT), BF16),
                pltpu.VMEM((D_EXPERT, D_MODEL), BF16),
            ],
        ),
        out_shape=jax.ShapeDtypeStruct((N_SLOTS, SUBLANES, LANES), F32),
        compiler_params=pltpu.CompilerParams(
            dimension_semantics=("arbitrary",), vmem_limit_bytes=VMEM_LIMIT),
        name="experts",
    )(blk_e, nvalid, xb, wg, wu, wd)

    def next_tile(i):
        return (jnp.minimum(i + 1, NT_TOTAL - 1), 0, 0)

    y_p, y_s = pl.pallas_call(
        _combine_kernel,
        grid=(NT_TOTAL,),
        in_specs=[
            smem_tile(lambda i: (i, 0, 0)),
            smem_tile(lambda i: (i, 0, 0)),
            smem_tile(next_tile),
            smem_tile(next_tile),
            pl.BlockSpec((TL, LANES), lambda i: (jnp.minimum(i, last_p), 0)),
            pl.BlockSpec((TL, LANES), lambda i: (0, 0)),
            pl.BlockSpec((None, TL, D_MODEL), hp_map),
            pl.BlockSpec((TL, D_MODEL), lambda i: (0, 0)),
            pl.BlockSpec((1, D_MODEL), lambda i: (0, 0)),
            any_spec,
        ],
        out_specs=[
            pl.BlockSpec((None, TL, D_MODEL), hp_map),
            pl.BlockSpec((TL, D_MODEL), lambda i: (0, 0)),
        ],
        out_shape=[
            jax.ShapeDtypeStruct((NT_PROMPT, TL, D_MODEL), F32),
            jax.ShapeDtypeStruct((T_SAMPLE, D_MODEL), F32),
        ],
        scratch_shapes=[
            pltpu.VMEM((2, 2, TL, SUBLANES, LANES), F32),
            pltpu.SemaphoreType.DMA((2,)),
        ],
        compiler_params=pltpu.CompilerParams(
            dimension_semantics=("arbitrary",), vmem_limit_bytes=VMEM_LIMIT),
        name="combine",
    )(d0, d1, d0, d1, pay_p, pay_s, h_p, h_s, gfin, yb)

    y_sample = y_s.reshape(DEC_BATCH, DEC_SEQ, D_MODEL)
    new_pool_prompt = pool_p[None, :, 1:, :]
    new_pool_sample = pool_s[None, :, 1:, :]
    new_gmlp_v_sample = vn_s.reshape(1, DEC_BATCH, DEC_SEQ, GMLP_WIDTH)
    y_prompt = y_p.reshape(BATCH, SEQ, D_MODEL)
    return (y_prompt, y_sample, new_pool_prompt, new_pool_sample, new_gmlp_v_sample)
```

```python
import jax
import jax.numpy as jnp
from jax import lax
from jax.experimental import pallas as pl
from jax.experimental.pallas import tpu as pltpu

F32 = jnp.float32
BF16 = jnp.bfloat16
I32 = jnp.int32

D_MODEL = 1024
SEQ = 16384
BATCH = 2
DEC_BATCH = 16
DEC_SEQ = 32
PAST_LEN = 1024
POOL_WIDTH = 512
POOL_GROUP_DIM = 128
POOL_WINDOWS = (2, 4, 8, 16)
HIST_ROWS = 16
GMLP_WIDTH = 512
GMLP_HEADS = 4
GMLP_HEAD_DIM = 128
GMLP_CHUNK = 128
IN_COLS = POOL_WIDTH + 2 * GMLP_WIDTH + 2 * D_MODEL
N_GROUPS = 4
EXPERTS_PER_GROUP = 4
N_EXPERTS = 16
D_EXPERT = 512
EPS = 1e-6

LANES = 128
SUBLANES = 8
assert D_MODEL == SUBLANES * LANES
TL = 512
BM = 512
T_PROMPT = BATCH * SEQ
T_SAMPLE = DEC_BATCH * DEC_SEQ
T_TOTAL = T_PROMPT + T_SAMPLE
NT_PROMPT = T_PROMPT // TL
TILES_PER_SEQ = SEQ // TL
NT_TOTAL = T_TOTAL // TL
assert T_SAMPLE == TL
N_ASSIGN = 2 * T_TOTAL
NB_MIN = N_ASSIGN // BM
NB_MAX = (N_ASSIGN + N_EXPERTS * (BM - 1)) // BM
N_SLOTS = NB_MAX * BM
ZERO_ROWS = BM // 2
DMA_UNROLL = 8
DMA_CHUNK = 64
VMEM_LIMIT = 56 * 1024 * 1024

REC_E0, REC_E1, REC_R0, REC_R1 = 0, 1, 2, 3
REC_ROWS = 8
PAY_W0, PAY_W1 = 0, 1


def _dot(a, b):
    return jnp.dot(a, b, preferred_element_type=F32)


def _rms(x, g):
    return x * lax.rsqrt(jnp.mean(x * x, axis=-1, keepdims=True) + EPS) * g


def _split_bf16(x):
    hi = x.astype(BF16)
    lo = (x - hi.astype(F32)).astype(BF16)
    return hi, lo


def _row_copy(src_ref, s, dst_ref, d, sem):
    return pltpu.make_async_copy(src_ref.at[pl.ds(s, 1)], dst_ref.at[pl.ds(d, 1)], sem)


def _pool_windows(load_shifted, a, pos, wpool_ref, pscale_ref):
    outs = []
    for g, w in enumerate(POOL_WINDOWS):
        cols = slice(g * POOL_GROUP_DIM, (g + 1) * POOL_GROUP_DIM)
        acc = a[:, cols]
        for k in range(1, w):
            acc = acc + load_shifted(k, g)
        cnt = jnp.minimum(pos + 1, w).astype(F32)
        d = (acc / cnt - a[:, cols]).astype(BF16)
        outs.append(_dot(d, wpool_ref[g]))
    return jnp.concatenate(outs, axis=-1) * pscale_ref[...]


def _merge(pa, pb, ga, gb, wupp_ref, wupg_ref):
    return (jax.nn.sigmoid(ga) * _dot(pa.astype(BF16), wupp_ref[...])
            + jax.nn.sigmoid(gb) * _dot(pb.astype(BF16), wupg_ref[...]))


def _init_ltri(ltri_ref):
    r = lax.broadcasted_iota(I32, (TL, TL), 0)
    c = lax.broadcasted_iota(I32, (TL, TL), 1)
    ltri_ref[...] = jnp.where(c < r, 1.0, 0.0).astype(BF16)


def _router_logits(hn, wrt_ref, brt_ref):
    hi, lo = _split_bf16(hn)
    whi, wlo = _split_bf16(wrt_ref[...])
    hi_w = _dot(hi, jnp.concatenate([whi, wlo], axis=1))
    return hi_w[:, 0:LANES] + hi_w[:, LANES:2 * LANES] + _dot(lo, whi) + brt_ref[...]


def _assign(lg, ltri_ref, run_ref, rec_out_ref, pay_out_ref, counted=True):
    lane = lax.broadcasted_iota(I32, lg.shape, 1)
    neg = jnp.float32(-jnp.inf)
    big = jnp.int32(1 << 20)

    def first_argmax(v, vmax):
        return jnp.min(jnp.where(v == vmax, lane, big), axis=-1, keepdims=True)

    gmask = lane < N_GROUPS
    glm = jnp.where(gmask, lg, neg)
    gmax = jnp.max(glm, axis=-1, keepdims=True)
    grp = first_argmax(glm, gmax)
    p_g = 1.0 / jnp.sum(jnp.where(gmask, jnp.exp(lg - gmax), 0.0), axis=-1, keepdims=True)

    lo_lane = N_GROUPS + EXPERTS_PER_GROUP * grp
    elm = jnp.where((lane >= lo_lane) & (lane < lo_lane + EXPERTS_PER_GROUP), lg, neg)
    v1 = jnp.max(elm, axis=-1, keepdims=True)
    i1 = first_argmax(elm, v1)
    elm2 = jnp.where(lane == i1, neg, elm)
    v2 = jnp.max(elm2, axis=-1, keepdims=True)
    i2 = first_argmax(elm2, v2)
    ex2 = jnp.exp(v2 - v1)
    inv = 1.0 / (1.0 + ex2)
    w0 = inv * p_g
    w1 = ex2 * inv * p_g
    e0 = i1 - N_GROUPS
    e1 = i2 - N_GROUPS

    sel0 = lane == e0
    sel1 = lane == e1
    onehot = jnp.where(sel0 | sel1, 1.0, 0.0)
    before = _dot(ltri_ref[...], onehot.astype(BF16)) + run_ref[...]
    r0 = jnp.sum(jnp.where(sel0, before, 0.0), axis=-1, keepdims=True)
    r1 = jnp.sum(jnp.where(sel1, before, 0.0), axis=-1, keepdims=True)
    run_ref[...] = run_ref[...] + jnp.where(counted, jnp.sum(onehot, axis=0, keepdims=True), 0.0)

    rec = jnp.where(lane == REC_E0, e0.astype(F32),
                    jnp.where(lane == REC_E1, e1.astype(F32),
                              jnp.where(lane == REC_R0, r0,
                                        jnp.where(lane == REC_R1, r1, 0.0))))
    rec_out_ref[...] = rec.T[0:REC_ROWS, :].astype(I32)
    pay_out_ref[...] = jnp.where(lane == PAY_W0, w0, jnp.where(lane == PAY_W1, w1, 0.0))


def _mixer_prompt_kernel(x_ref, xprev_ref, gmix_ref, win_ref, wpool_ref, pscale_ref, gv_ref, ws_ref,
                         bsp_ref, wupp_ref, wupg_ref, wout_ref, gffn_ref, wrt_ref, brt_ref,
                         h_ref, rec_out_ref, pay_out_ref, pool_ref, cnt_ref,
                         aext_ref, s_ref, ltri_ref, run_ref, merged_ref):
    s = pl.program_id(0)

    @pl.when(s == 0)
    def _():
        _init_ltri(ltri_ref)
        run_ref[...] = jnp.zeros_like(run_ref)
        aext_ref[TL:TL + HIST_ROWS, :] = jnp.zeros((HIST_ROWS, POOL_WIDTH), F32)
        merged_ref[...] = jnp.zeros_like(merged_ref)

    j = jnp.minimum(s, NT_PROMPT - 1) % TILES_PER_SEQ

    h = xprev_ref[...] + _dot(merged_ref[...], wout_ref[...])
    xn = _rms(x_ref[...], gmix_ref[...]).astype(BF16)
    a = _dot(xn, win_ref[:, 0:512])
    h_ref[...] = h
    hn = _rms(h, gffn_ref[...])
    lg = _router_logits(hn, wrt_ref, brt_ref)

    aext_ref[0:HIST_ROWS, :] = jnp.where(j > 0, aext_ref[TL:TL + HIST_ROWS, :], 0.0)
    aext_ref[HIST_ROWS:HIST_ROWS + TL, :] = a
    pool_ref[...] = a[TL - HIST_ROWS:TL, :]
    pos = j * TL + lax.broadcasted_iota(I32, (TL, 1), 0)

    def load_shifted(k, g):
        return aext_ref[HIST_ROWS - k:HIST_ROWS - k + TL, g * POOL_GROUP_DIM:(g + 1) * POOL_GROUP_DIM]

    v = _dot(xn, win_ref[:, 1024:1536])
    pa = _pool_windows(load_shifted, a, pos, wpool_ref, pscale_ref)
    u = _dot(xn, win_ref[:, 512:1024])
    _assign(lg, ltri_ref, run_ref, rec_out_ref, pay_out_ref, counted=s > 0)
    cnt_ref[...] = run_ref[...]
    ga = _dot(xn, win_ref[:, 1536:2560])

    vn = _rms(v, gv_ref[...]).astype(BF16)
    rr = lax.broadcasted_iota(I32, (GMLP_CHUNK, GMLP_CHUNK), 0)
    cc = lax.broadcasted_iota(I32, (GMLP_CHUNK, GMLP_CHUNK), 1)
    for hd in range(GMLP_HEADS):
        cols = slice(hd * GMLP_HEAD_DIM, (hd + 1) * GMLP_HEAD_DIM)
        wsm = jnp.where(cc <= rr, ws_ref[hd], 0.0).astype(BF16)
        for c in range(TL // GMLP_CHUNK):
            rows = slice(c * GMLP_CHUNK, (c + 1) * GMLP_CHUNK)
            s_ref[rows, cols] = _dot(wsm, vn[rows, cols]) + bsp_ref[rows, hd:hd + 1]
    pb = u * s_ref[...]
    gb = _dot(xn, win_ref[:, 2560:3584])

    merged_ref[...] = _merge(pa, pb, ga, gb, wupp_ref, wupg_ref).astype(BF16)


def _mixer_sample_kernel(x_ref, hist_ref, run0_ref, gmix_ref, win_ref, wpool_ref, pscale_ref, gv_ref,
                         ws_ref, bsp_ref, wupp_ref, wupg_ref, wout_ref, gffn_ref, wrt_ref, brt_ref,
                         h_ref, rec_out_ref, pay_out_ref, pool_ref, vn_ref, cnt_ref,
                         aext_ref, ltri_ref, run_ref):
    _init_ltri(ltri_ref)
    run_ref[...] = run0_ref[...]

    x = x_ref[...]
    xn = _rms(x, gmix_ref[...]).astype(BF16)
    a = _dot(xn, win_ref[:, 0:512])
    u = _dot(xn, win_ref[:, 512:1024])
    v = _dot(xn, win_ref[:, 1024:1536])
    ga = _dot(xn, win_ref[:, 1536:2560])
    gb = _dot(xn, win_ref[:, 2560:3584])

    a3 = a.reshape(DEC_BATCH, DEC_SEQ, POOL_WIDTH)
    aext_ref[:, 0:HIST_ROWS, :] = hist_ref[...]
    aext_ref[:, HIST_ROWS:HIST_ROWS + DEC_SEQ, :] = a3
    pool_ref[...] = a3[:, DEC_SEQ - HIST_ROWS:DEC_SEQ, :]
    row = lax.broadcasted_iota(I32, (TL, 1), 0)
    pos = PAST_LEN + row % DEC_SEQ

    def load_shifted(k, g):
        sl = aext_ref[:, HIST_ROWS - k:HIST_ROWS - k + DEC_SEQ, g * POOL_GROUP_DIM:(g + 1) * POOL_GROUP_DIM]
        return sl.reshape(TL, POOL_GROUP_DIM)

    pa = _pool_windows(load_shifted, a, pos, wpool_ref, pscale_ref)

    vnf = _rms(v, gv_ref[...])
    vn_ref[...] = vnf
    vn = vnf.astype(BF16)
    rsel = (lax.broadcasted_iota(I32, (TL, GMLP_CHUNK), 1)
            == lax.broadcasted_iota(I32, (TL, GMLP_CHUNK), 0) % DEC_SEQ)
    rsel_b = jnp.where(rsel, 1.0, 0.0).astype(BF16)
    csel = (lax.broadcasted_iota(I32, (GMLP_CHUNK, TL), 0)
            == lax.broadcasted_iota(I32, (GMLP_CHUNK, TL), 1) % DEC_SEQ)
    csel_b = jnp.where(csel, 1.0, 0.0).astype(BF16)
    rr = lax.broadcasted_iota(I32, (TL, TL), 0)
    cc = lax.broadcasted_iota(I32, (TL, TL), 1)
    keep = (rr // DEC_SEQ == cc // DEC_SEQ) & (cc <= rr)
    s_parts = []
    for hd in range(GMLP_HEADS):
        cols = slice(hd * GMLP_HEAD_DIM, (hd + 1) * GMLP_HEAD_DIM)
        wrow = _dot(rsel_b, ws_ref[hd].astype(BF16)).astype(BF16)
        wfull = _dot(wrow, csel_b)
        wblk = jnp.where(keep, wfull, 0.0).astype(BF16)
        s_parts.append(_dot(wblk, vn[:, cols]) + bsp_ref[:, hd:hd + 1])
    pb = u * jnp.concatenate(s_parts, axis=-1)

    h = x + _dot(_merge(pa, pb, ga, gb, wupp_ref, wupg_ref).astype(BF16), wout_ref[...])
    h_ref[...] = h
    lg = _router_logits(_rms(h, gffn_ref[...]), wrt_ref, brt_ref)
    _assign(lg, ltri_ref, run_ref, rec_out_ref, pay_out_ref)
    cnt_ref[...] = run_ref[...]


def _dispatch_kernel(pad_start_ref, pad_len_ref, nvalid_ref,
                     d0_ref, d1_ref, hp_ref, hs_ref, gffn_ref, xb_ref, row_ref, zero_ref, sems):
    i = pl.program_id(0)
    slot = i % 2
    rows = row_ref.at[slot]

    def fill_and_scatter(h_ref):
        for c in range(TL // DMA_CHUNK):
            r0 = c * DMA_CHUNK
            hn = _rms(h_ref[r0:r0 + DMA_CHUNK, :], gffn_ref[...])
            rows[r0:r0 + DMA_CHUNK] = hn.reshape(DMA_CHUNK, SUBLANES, LANES)
            for t in range(r0, r0 + DMA_CHUNK):
                _row_copy(rows, t, xb_ref, d0_ref[0, t], sems.at[slot]).start(priority=0)
                _row_copy(rows, t, xb_ref, d1_ref[0, t], sems.at[slot]).start(priority=1)

    @pl.when(i < NT_PROMPT)
    def _():
        fill_and_scatter(hp_ref)

    @pl.when(i == NT_PROMPT)
    def _():
        fill_and_scatter(hs_ref)

    def drain(s):
        def body(t, c):
            _row_copy(row_ref.at[s], t, xb_ref, 0, sems.at[s]).wait()
            _row_copy(row_ref.at[s], t, xb_ref, 0, sems.at[s]).wait()
            return c
        lax.fori_loop(0, TL, body, 0, unroll=DMA_UNROLL)

    @pl.when(i > 0)
    def _():
        drain(1 - slot)

    @pl.when(i == NT_TOTAL - 1)
    def _():
        drain(slot)
        zero_ref[...] = jnp.zeros_like(zero_ref)
        bits = [1 << k for k in reversed(range(BM.bit_length() - 1))]

        def pad_copies(fn):
            for e in range(N_EXPERTS):
                off = pad_start_ref[e]
                n = pad_len_ref[e]
                for bit in bits:
                    take = jnp.bitwise_and(n, bit)

                    @pl.when(take != 0)
                    def _(off=off, bit=bit):
                        fn(pltpu.make_async_copy(zero_ref.at[pl.ds(0, bit)], xb_ref.at[pl.ds(off, bit)],
                                                 sems.at[slot]))
                    off = off + take
            for blk in range(NB_MIN, NB_MAX):
                @pl.when(blk >= nvalid_ref[0])
                def _(blk=blk):
                    for half in range(BM // ZERO_ROWS):
                        fn(pltpu.make_async_copy(
                            zero_ref, xb_ref.at[pl.ds(blk * BM + half * ZERO_ROWS, ZERO_ROWS)], sems.at[slot]))

        pad_copies(lambda cp: cp.start())
        pad_copies(lambda cp: cp.wait())


def _experts_kernel(blk_e_ref, nvalid_ref, xb_ref, wg_ref, wu_ref, wd_ref, yb_ref, wgb_ref, wub_ref, wdb_ref):
    i = pl.program_id(0)

    @pl.when((i == 0) | (blk_e_ref[i] != blk_e_ref[jnp.maximum(i - 1, 0)]))
    def _():
        wgb_ref[...] = wg_ref[...].astype(BF16)
        wub_ref[...] = wu_ref[...].astype(BF16)
        wdb_ref[...] = wd_ref[...].astype(BF16)

    @pl.when(i < nvalid_ref[0])
    def _():
        x = xb_ref[...].reshape(BM, D_MODEL).astype(BF16)
        hg = _dot(x, wgb_ref[...])
        hu = _dot(x, wub_ref[...])
        act = (hg * jax.nn.sigmoid(hg)) * hu
        yb_ref[...] = _dot(act.astype(BF16), wdb_ref[...]).reshape(BM, SUBLANES, LANES)

    @pl.when(i >= nvalid_ref[0])
    def _():
        yb_ref[...] = jnp.zeros_like(yb_ref)


def _combine_kernel(d0_ref, d1_ref, d0n_ref, d1n_ref, payp_ref, pays_ref, hp_ref, hs_ref, gfin_ref, yb_ref,
                    yp_ref, ys_ref, buf_ref, sems):
    i = pl.program_id(0)
    slot = i % 2

    def gather(da_ref, db_ref, s):
        def body(t, c):
            _row_copy(yb_ref, da_ref[0, t], buf_ref.at[s, 0], t, sems.at[s]).start(priority=0)
            _row_copy(yb_ref, db_ref[0, t], buf_ref.at[s, 1], t, sems.at[s]).start(priority=1)
            return c
        lax.fori_loop(0, TL, body, 0, unroll=DMA_UNROLL)

    @pl.when(i == 0)
    def _():
        gather(d0_ref, d1_ref, slot)

    def drain(t, c):
        _row_copy(yb_ref, 0, buf_ref.at[slot, 0], t, sems.at[slot]).wait()
        _row_copy(yb_ref, 0, buf_ref.at[slot, 1], t, sems.at[slot]).wait()
        return c

    lax.fori_loop(0, TL, drain, 0, unroll=DMA_UNROLL)

    def finish_chunk(h_ref, pay_ref, out_ref, r0):
        rows = slice(r0, r0 + DMA_CHUNK)
        y0 = buf_ref.at[slot, 0][rows].reshape(DMA_CHUNK, D_MODEL)
        y1 = buf_ref.at[slot, 1][rows].reshape(DMA_CHUNK, D_MODEL)
        pay = pay_ref[rows, :]
        moe = y0 * pay[:, PAY_W0:PAY_W0 + 1] + y1 * pay[:, PAY_W1:PAY_W1 + 1]
        out_ref[rows, :] = _rms(h_ref[rows, :] + moe, gfin_ref[...])

    @pl.when(i < NT_PROMPT)
    def _():
        for c in range(TL // DMA_CHUNK):
            r0 = c * DMA_CHUNK
            finish_chunk(hp_ref, payp_ref, yp_ref, r0)
            for t in range(r0, r0 + DMA_CHUNK):
                _row_copy(yb_ref, d0n_ref[0, t], buf_ref.at[1 - slot, 0], t, sems.at[1 - slot]).start(priority=0)
                _row_copy(yb_ref, d1n_ref[0, t], buf_ref.at[1 - slot, 1], t, sems.at[1 - slot]).start(priority=1)

    @pl.when(i == NT_PROMPT)
    def _():
        for c in range(TL // DMA_CHUNK):
            finish_chunk(hs_ref, pays_ref, ys_ref, c * DMA_CHUNK)


def _const_spec(shape):
    zeros = (0,) * len(shape)
    return pl.BlockSpec(shape, lambda *_: zeros, pipeline_mode=pl.Buffered(1))


def kernel(x_prompt, x_sample, state_pool, g_mix, w_in, w_pool, pool_scale, g_v, w_spatial, b_spatial,
           w_up_pool, w_up_gmlp, w_out, g_ffn, w_group, b_group, w_router, b_router, w_gate, w_up,
           w_down, g_final):
    gmix = g_mix[0][None, :]
    win = w_in[0].astype(BF16)
    wpool = w_pool[0].astype(BF16)
    pscale = pool_scale[0][None, :]
    gv = g_v[0][None, :]
    ws = w_spatial[0]
    bsp_t = b_spatial[0].T
    wupp = w_up_pool[0].astype(BF16)
    wupg = w_up_gmlp[0].astype(BF16)
    wout = w_out[0].astype(BF16)
    gffn = g_ffn[0][None, :]
    n_route = N_GROUPS + N_EXPERTS
    wrt = jnp.pad(jnp.concatenate([w_group[0], w_router[0]], axis=1), ((0, 0), (0, LANES - n_route)))
    brt = jnp.pad(jnp.concatenate([b_group[0], b_router[0]]), (0, LANES - n_route))[None, :]
    wg = w_gate[0]
    wu = w_up[0]
    wd = w_down[0]
    gfin = g_final[None, :]
    bsp_prompt = jnp.tile(bsp_t, (TL // GMLP_CHUNK, 1))
    bsp_sample = jnp.tile(bsp_t[:DEC_SEQ], (DEC_BATCH, 1))
    hist = jnp.pad(state_pool[0], ((0, 0), (1, 0), (0, 0)))

    weight_specs = [
        _const_spec((1, D_MODEL)),
        _const_spec((D_MODEL, IN_COLS)),
        _const_spec((4, POOL_GROUP_DIM, POOL_GROUP_DIM)),
        _const_spec((1, POOL_WIDTH)),
        _const_spec((1, GMLP_WIDTH)),
        _const_spec((GMLP_HEADS, GMLP_CHUNK, GMLP_CHUNK)),
        _const_spec((TL, GMLP_HEADS)),
        _const_spec((POOL_WIDTH, D_MODEL)),
        _const_spec((GMLP_WIDTH, D_MODEL)),
        _const_spec((D_MODEL, D_MODEL)),
        _const_spec((1, D_MODEL)),
        _const_spec((D_MODEL, LANES)),
        _const_spec((1, LANES)),
    ]
    any_spec = pl.BlockSpec(memory_space=pl.ANY)
    route_scratch = [
        pltpu.VMEM((TL, TL), BF16),
        pltpu.VMEM((1, LANES), F32),
    ]

    def started(s):
        return jnp.minimum(s, NT_PROMPT - 1)

    def finished(s):
        return jnp.maximum(s - 1, 0)

    x_tiles = x_prompt.reshape(NT_PROMPT, TL, D_MODEL)
    h_p, rec_p, pay_p, pool_p, cnt_p = pl.pallas_call(
        _mixer_prompt_kernel,
        grid=(NT_PROMPT + 1,),
        in_specs=[
            pl.BlockSpec((None, TL, D_MODEL), lambda s: (started(s), 0, 0)),
            pl.BlockSpec((None, TL, D_MODEL), lambda s: (finished(s), 0, 0)),
        ] + weight_specs,
        out_specs=[
            pl.BlockSpec((None, TL, D_MODEL), lambda s: (finished(s), 0, 0)),
            pl.BlockSpec((None, REC_ROWS, TL), lambda s: (finished(s), 0, 0)),
            pl.BlockSpec((TL, LANES), lambda s: (finished(s), 0)),
            pl.BlockSpec((None, HIST_ROWS, POOL_WIDTH), lambda s: (started(s) // TILES_PER_SEQ, 0, 0)),
            pl.BlockSpec((1, LANES), lambda s: (0, 0)),
        ],
        out_shape=[
            jax.ShapeDtypeStruct((NT_PROMPT, TL, D_MODEL), F32),
            jax.ShapeDtypeStruct((NT_PROMPT, REC_ROWS, TL), I32),
            jax.ShapeDtypeStruct((T_PROMPT, LANES), F32),
            jax.ShapeDtypeStruct((BATCH, HIST_ROWS, POOL_WIDTH), F32),
            jax.ShapeDtypeStruct((1, LANES), F32),
        ],
        scratch_shapes=[
            pltpu.VMEM((HIST_ROWS + TL, POOL_WIDTH), F32),
            pltpu.VMEM((TL, GMLP_WIDTH), F32),
        ] + route_scratch + [pltpu.VMEM((TL, D_MODEL), BF16)],
        compiler_params=pltpu.CompilerParams(
            dimension_semantics=("arbitrary",), vmem_limit_bytes=VMEM_LIMIT),
        name="mixer_prompt",
    )(x_tiles, x_tiles, gmix, win, wpool, pscale, gv, ws, bsp_prompt, wupp, wupg, wout, gffn, wrt, brt)

    h_s, rec_s, pay_s, pool_s, vn_s, cnt = pl.pallas_call(
        _mixer_sample_kernel,
        grid=(1,),
        in_specs=[
            _const_spec((TL, D_MODEL)),
            _const_spec((DEC_BATCH, HIST_ROWS, POOL_WIDTH)),
            _const_spec((1, LANES)),
        ] + weight_specs,
        out_specs=[
            _const_spec((TL, D_MODEL)),
            pl.BlockSpec((None, REC_ROWS, TL), lambda i: (0, 0, 0)),
            _const_spec((TL, LANES)),
            _const_spec((DEC_BATCH, HIST_ROWS, POOL_WIDTH)),
            _const_spec((TL, GMLP_WIDTH)),
            _const_spec((1, LANES)),
        ],
        out_shape=[
            jax.ShapeDtypeStruct((T_SAMPLE, D_MODEL), F32),
            jax.ShapeDtypeStruct((1, REC_ROWS, TL), I32),
            jax.ShapeDtypeStruct((T_SAMPLE, LANES), F32),
            jax.ShapeDtypeStruct((DEC_BATCH, HIST_ROWS, POOL_WIDTH), F32),
            jax.ShapeDtypeStruct((T_SAMPLE, GMLP_WIDTH), F32),
            jax.ShapeDtypeStruct((1, LANES), F32),
        ],
        scratch_shapes=[
            pltpu.VMEM((DEC_BATCH, HIST_ROWS + DEC_SEQ, POOL_WIDTH), F32),
        ] + route_scratch,
        compiler_params=pltpu.CompilerParams(
            dimension_semantics=("arbitrary",), vmem_limit_bytes=VMEM_LIMIT),
        name="mixer_sample",
    )(x_sample.reshape(T_SAMPLE, D_MODEL), hist, cnt_p, gmix, win, wpool, pscale, gv, ws, bsp_sample,
      wupp, wupg, wout, gffn, wrt, brt)

    counts = cnt[0, :N_EXPERTS].astype(I32)
    nblk = (counts + BM - 1) // BM
    bend = jnp.cumsum(nblk)
    bstart = bend - nblk
    nvalid = bend[-1:]
    step = jnp.minimum(jnp.arange(NB_MAX, dtype=I32), nvalid - 1)
    blk_e = jnp.minimum(jnp.sum(step[:, None] >= bend[None, :], axis=1), N_EXPERTS - 1).astype(I32)
    pad_start = bstart * BM + counts
    pad_len = nblk * BM - counts
    rec = jnp.concatenate([rec_p, rec_s], axis=0)
    expert_ids = jnp.arange(N_EXPERTS, dtype=I32)[None, None, :]

    def slots(e, r):
        first = jnp.sum(jnp.where(e[:, :, None] == expert_ids, (bstart * BM)[None, None, :], 0), axis=-1)
        return (first + r)[:, None, :]

    d0 = slots(rec[:, REC_E0], rec[:, REC_R0])
    d1 = slots(rec[:, REC_E1], rec[:, REC_R1])

    last_p = NT_PROMPT - 1

    def hp_map(i, *_):
        return (jnp.minimum(i, last_p), 0, 0)

    def smem_tile(index_map):
        return pl.BlockSpec((None, 1, TL), index_map, memory_space=pltpu.SMEM)

    xb = pl.pallas_call(
        _dispatch_kernel,
        grid_spec=pltpu.PrefetchScalarGridSpec(
            num_scalar_prefetch=3,
            grid=(NT_TOTAL,),
            in_specs=[
                smem_tile(lambda i, *_: (i, 0, 0)),
                smem_tile(lambda i, *_: (i, 0, 0)),
                pl.BlockSpec((None, TL, D_MODEL), hp_map),
                pl.BlockSpec((TL, D_MODEL), lambda i, *_: (0, 0)),
                pl.BlockSpec((1, D_MODEL), lambda i, *_: (0, 0)),
            ],
            out_specs=any_spec,
            scratch_shapes=[
                pltpu.VMEM((2, TL, SUBLANES, LANES), F32),
                pltpu.VMEM((ZERO_ROWS, SUBLANES, LANES), F32),
                pltpu.SemaphoreType.DMA((2,)),
            ],
        ),
        out_shape=jax.ShapeDtypeStruct((N_SLOTS, SUBLANES, LANES), F32),
        compiler_params=pltpu.CompilerParams(
            dimension_semantics=("arbitrary",), vmem_limit_bytes=VMEM_LIMIT),
        name="dispatch",
    )(pad_start, pad_len, nvalid, d0, d1, h_p, h_s, gffn)

    def xb_map(i, be, nv):
        return (jnp.minimum(i, nv[0] - 1), 0, 0)

    yb = pl.pallas_call(
        _experts_kernel,
        grid_spec=pltpu.PrefetchScalarGridSpec(
            num_scalar_prefetch=2,
            grid=(NB_MAX,),
            in_specs=[
                pl.BlockSpec((BM, SUBLANES, LANES), xb_map),
                pl.BlockSpec((None, D_MODEL, D_EXPERT), lambda i, be, nv: (be[i], 0, 0)),
                pl.BlockSpec((None, D_MODEL, D_EXPERT), lambda i, be, nv: (be[i], 0, 0)),
                pl.BlockSpec((None, D_EXPERT, D_MODEL), lambda i, be, nv: (be[i], 0, 0)),
            ],
            out_specs=pl.BlockSpec((BM, SUBLANES, LANES), lambda i, be, nv: (i, 0, 0)),
            scratch_shapes=[
                pltpu.VMEM((D_MODEL, D_EXPERT), BF16),
                pltpu.VMEM((D_MODEL, D_EXPERT), BF16),
                pltpu.VMEM((D_EXPERT, D_MODEL), BF16),
            ],
        ),
        out_shape=jax.ShapeDtypeStruct((N_SLOTS, SUBLANES, LANES), F32),
        compiler_params=pltpu.CompilerParams(
            dimension_semantics=("arbitrary",), vmem_limit_bytes=VMEM_LIMIT),
        name="experts",
    )(blk_e, nvalid, xb, wg, wu, wd)

    def next_tile(i):
        return (jnp.minimum(i + 1, NT_TOTAL - 1), 0, 0)

    y_p, y_s = pl.pallas_call(
        _combine_kernel,
        grid=(NT_TOTAL,),
        in_specs=[
            smem_tile(lambda i: (i, 0, 0)),
            smem_tile(lambda i: (i, 0, 0)),
            smem_tile(next_tile),
            smem_tile(next_tile),
            pl.BlockSpec((TL, LANES), lambda i: (jnp.minimum(i, last_p), 0)),
            pl.BlockSpec((TL, LANES), lambda i: (0, 0)),
            pl.BlockSpec((None, TL, D_MODEL), hp_map),
            pl.BlockSpec((TL, D_MODEL), lambda i: (0, 0)),
            pl.BlockSpec((1, D_MODEL), lambda i: (0, 0)),
            any_spec,
        ],
        out_specs=[
            pl.BlockSpec((None, TL, D_MODEL), hp_map),
            pl.BlockSpec((TL, D_MODEL), lambda i: (0, 0)),
        ],
        out_shape=[
            jax.ShapeDtypeStruct((NT_PROMPT, TL, D_MODEL), F32),
            jax.ShapeDtypeStruct((T_SAMPLE, D_MODEL), F32),
        ],
        scratch_shapes=[
            pltpu.VMEM((2, 2, TL, SUBLANES, LANES), F32),
            pltpu.SemaphoreType.DMA((2,)),
        ],
        compiler_params=pltpu.CompilerParams(
            dimension_semantics=("arbitrary",), vmem_limit_bytes=VMEM_LIMIT),
        name="combine",
    )(d0, d1, d0, d1, pay_p, pay_s, h_p, h_s, gfin, yb)

    y_sample = y_s.reshape(DEC_BATCH, DEC_SEQ, D_MODEL)
    new_pool_prompt = pool_p[None, :, 1:, :]
    new_pool_sample = pool_s[None, :, 1:, :]
    new_gmlp_v_sample = vn_s.reshape(1, DEC_BATCH, DEC_SEQ, GMLP_WIDTH)
    y_prompt = y_p.reshape(BATCH, SEQ, D_MODEL)
    return (y_prompt, y_sample, new_pool_prompt, new_pool_sample, new_gmlp_v_sample)
```

```python
import jax
import jax.numpy as jnp
from jax import lax
from jax.experimental import pallas as pl
from jax.experimental.pallas import tpu as pltpu

F32 = jnp.float32
BF16 = jnp.bfloat16
I32 = jnp.int32

D_MODEL = 1024
SEQ = 16384
BATCH = 2
DEC_BATCH = 16
DEC_SEQ = 32
PAST_LEN = 1024
POOL_WIDTH = 512
POOL_GROUP_DIM = 128
POOL_WINDOWS = (2, 4, 8, 16)
HIST_ROWS = 16
GMLP_WIDTH = 512
GMLP_HEADS = 4
GMLP_HEAD_DIM = 128
GMLP_CHUNK = 128
IN_COLS = POOL_WIDTH + 2 * GMLP_WIDTH + 2 * D_MODEL
N_GROUPS = 4
EXPERTS_PER_GROUP = 4
N_EXPERTS = 16
D_EXPERT = 512
EPS = 1e-6

LANES = 128
SUBLANES = 8
assert D_MODEL == SUBLANES * LANES
TL = 512
BM = 256
T_PROMPT = BATCH * SEQ
T_SAMPLE = DEC_BATCH * DEC_SEQ
T_TOTAL = T_PROMPT + T_SAMPLE
NT_PROMPT = T_PROMPT // TL
TILES_PER_SEQ = SEQ // TL
NT_TOTAL = T_TOTAL // TL
assert T_SAMPLE == TL
PAIR_LO = (0, 0, 0, 1, 1, 2)
PAIR_HI = (1, 2, 3, 2, 3, 3)
N_PAIRS = len(PAIR_LO)
N_BUCKETS = N_GROUPS * N_PAIRS
NB_MIN = T_TOTAL // BM
NB_MAX = (T_TOTAL + N_BUCKETS * (BM - 1)) // BM
N_SLOTS = NB_MAX * BM
ZERO_ROWS = BM // 2
DMA_UNROLL = 8
DMA_CHUNK = 64
VMEM_LIMIT = 56 * 1024 * 1024

REC_BUCKET, REC_RANK = 0, 1
REC_ROWS = 8
PAY_LO, PAY_HI = 0, 1


def _dot(a, b):
    return jnp.dot(a, b, preferred_element_type=F32)


def _rms(x, g):
    return x * lax.rsqrt(jnp.mean(x * x, axis=-1, keepdims=True) + EPS) * g


def _split_bf16(x):
    hi = x.astype(BF16)
    lo = (x - hi.astype(F32)).astype(BF16)
    return hi, lo


def _row_copy(src_ref, s, dst_ref, d, sem):
    return pltpu.make_async_copy(src_ref.at[pl.ds(s, 1)], dst_ref.at[pl.ds(d, 1)], sem)


def _pool_windows(load_shifted, a, pos, wpool_ref, pscale_ref):
    outs = []
    for g, w in enumerate(POOL_WINDOWS):
        cols = slice(g * POOL_GROUP_DIM, (g + 1) * POOL_GROUP_DIM)
        acc = a[:, cols]
        for k in range(1, w):
            acc = acc + load_shifted(k, g)
        cnt = jnp.minimum(pos + 1, w).astype(F32)
        d = (acc / cnt - a[:, cols]).astype(BF16)
        outs.append(_dot(d, wpool_ref[g]))
    return jnp.concatenate(outs, axis=-1) * pscale_ref[...]


def _merge(pa, pb, ga, gb, wupp_ref, wupg_ref):
    return (jax.nn.sigmoid(ga) * _dot(pa.astype(BF16), wupp_ref[...])
            + jax.nn.sigmoid(gb) * _dot(pb.astype(BF16), wupg_ref[...]))


def _init_ltri(ltri_ref):
    r = lax.broadcasted_iota(I32, (TL, TL), 0)
    c = lax.broadcasted_iota(I32, (TL, TL), 1)
    ltri_ref[...] = jnp.where(c < r, 1.0, 0.0).astype(BF16)


def _router_logits(hn, wrt_ref, brt_ref):
    hi, lo = _split_bf16(hn)
    whi, wlo = _split_bf16(wrt_ref[...])
    hi_w = _dot(hi, jnp.concatenate([whi, wlo], axis=1))
    return hi_w[:, 0:LANES] + hi_w[:, LANES:2 * LANES] + _dot(lo, whi) + brt_ref[...]


def _assign(lg, ltri_ref, run_ref, rec_out_ref, pay_out_ref, counted=True):
    lane = lax.broadcasted_iota(I32, lg.shape, 1)
    neg = jnp.float32(-jnp.inf)
    big = jnp.int32(1 << 20)

    def first_argmax(v, vmax):
        return jnp.min(jnp.where(v == vmax, lane, big), axis=-1, keepdims=True)

    gmask = lane < N_GROUPS
    glm = jnp.where(gmask, lg, neg)
    gmax = jnp.max(glm, axis=-1, keepdims=True)
    grp = first_argmax(glm, gmax)
    p_g = 1.0 / jnp.sum(jnp.where(gmask, jnp.exp(lg - gmax), 0.0), axis=-1, keepdims=True)

    lo_lane = N_GROUPS + EXPERTS_PER_GROUP * grp
    elm = jnp.where((lane >= lo_lane) & (lane < lo_lane + EXPERTS_PER_GROUP), lg, neg)
    v1 = jnp.max(elm, axis=-1, keepdims=True)
    i1 = first_argmax(elm, v1)
    elm2 = jnp.where(lane == i1, neg, elm)
    v2 = jnp.max(elm2, axis=-1, keepdims=True)
    i2 = first_argmax(elm2, v2)
    ex2 = jnp.exp(v2 - v1)
    inv = 1.0 / (1.0 + ex2)
    w_top = inv * p_g
    w_second = ex2 * inv * p_g

    loc1 = i1 - lo_lane
    loc2 = i2 - lo_lane
    top_is_lo = loc1 < loc2
    e_lo = jnp.minimum(loc1, loc2)
    e_hi = jnp.maximum(loc1, loc2)
    pair = jnp.where(e_lo == 0, 0, jnp.where(e_lo == 1, 3, 5)) + e_hi - e_lo - 1
    bucket = grp * N_PAIRS + pair
    w_lo = jnp.where(top_is_lo, w_top, w_second)
    w_hi = jnp.where(top_is_lo, w_second, w_top)

    sel = lane == bucket
    onehot = jnp.where(sel, 1.0, 0.0)
    before = _dot(ltri_ref[...], onehot.astype(BF16)) + run_ref[...]
    rank = jnp.sum(jnp.where(sel, before, 0.0), axis=-1, keepdims=True)
    run_ref[...] = run_ref[...] + jnp.where(counted, jnp.sum(onehot, axis=0, keepdims=True), 0.0)

    rec = jnp.where(lane == REC_BUCKET, bucket.astype(F32), jnp.where(lane == REC_RANK, rank, 0.0))
    rec_out_ref[...] = rec.T[0:REC_ROWS, :].astype(I32)
    pay_out_ref[...] = jnp.where(lane == PAY_LO, w_lo, jnp.where(lane == PAY_HI, w_hi, 0.0))


def _mixer_prompt_kernel(x_ref, xprev_ref, gmix_ref, win_ref, wpool_ref, pscale_ref, gv_ref, ws_ref,
                         bsp_ref, wupp_ref, wupg_ref, wout_ref, gffn_ref, wrt_ref, brt_ref,
                         h_ref, rec_out_ref, pay_out_ref, pool_ref, cnt_ref,
                         aext_ref, s_ref, ltri_ref, run_ref, merged_ref):
    s = pl.program_id(0)

    @pl.when(s == 0)
    def _():
        _init_ltri(ltri_ref)
        run_ref[...] = jnp.zeros_like(run_ref)
        aext_ref[TL:TL + HIST_ROWS, :] = jnp.zeros((HIST_ROWS, POOL_WIDTH), F32)
        merged_ref[...] = jnp.zeros_like(merged_ref)

    j = jnp.minimum(s, NT_PROMPT - 1) % TILES_PER_SEQ

    h = xprev_ref[...] + _dot(merged_ref[...], wout_ref[...])
    xn = _rms(x_ref[...], gmix_ref[...]).astype(BF16)
    a = _dot(xn, win_ref[:, 0:512])
    h_ref[...] = h
    hn = _rms(h, gffn_ref[...])
    lg = _router_logits(hn, wrt_ref, brt_ref)

    aext_ref[0:HIST_ROWS, :] = jnp.where(j > 0, aext_ref[TL:TL + HIST_ROWS, :], 0.0)
    aext_ref[HIST_ROWS:HIST_ROWS + TL, :] = a
    pool_ref[...] = a[TL - HIST_ROWS:TL, :]
    pos = j * TL + lax.broadcasted_iota(I32, (TL, 1), 0)

    def load_shifted(k, g):
        return aext_ref[HIST_ROWS - k:HIST_ROWS - k + TL, g * POOL_GROUP_DIM:(g + 1) * POOL_GROUP_DIM]

    v = _dot(xn, win_ref[:, 1024:1536])
    pa = _pool_windows(load_shifted, a, pos, wpool_ref, pscale_ref)
    u = _dot(xn, win_ref[:, 512:1024])
    _assign(lg, ltri_ref, run_ref, rec_out_ref, pay_out_ref, counted=s > 0)
    cnt_ref[...] = run_ref[...]
    ga = _dot(xn, win_ref[:, 1536:2560])

    vn = _rms(v, gv_ref[...]).astype(BF16)
    rr = lax.broadcasted_iota(I32, (GMLP_CHUNK, GMLP_CHUNK), 0)
    cc = lax.broadcasted_iota(I32, (GMLP_CHUNK, GMLP_CHUNK), 1)
    for hd in range(GMLP_HEADS):
        cols = slice(hd * GMLP_HEAD_DIM, (hd + 1) * GMLP_HEAD_DIM)
        wsm = jnp.where(cc <= rr, ws_ref[hd], 0.0).astype(BF16)
        for c in range(TL // GMLP_CHUNK):
            rows = slice(c * GMLP_CHUNK, (c + 1) * GMLP_CHUNK)
            s_ref[rows, cols] = _dot(wsm, vn[rows, cols]) + bsp_ref[rows, hd:hd + 1]
    pb = u * s_ref[...]
    gb = _dot(xn, win_ref[:, 2560:3584])

    merged_ref[...] = _merge(pa, pb, ga, gb, wupp_ref, wupg_ref).astype(BF16)


def _mixer_sample_kernel(x_ref, hist_ref, run0_ref, gmix_ref, win_ref, wpool_ref, pscale_ref, gv_ref,
                         ws_ref, bsp_ref, wupp_ref, wupg_ref, wout_ref, gffn_ref, wrt_ref, brt_ref,
                         h_ref, rec_out_ref, pay_out_ref, pool_ref, vn_ref, cnt_ref,
                         aext_ref, ltri_ref, run_ref):
    _init_ltri(ltri_ref)
    run_ref[...] = run0_ref[...]

    x = x_ref[...]
    xn = _rms(x, gmix_ref[...]).astype(BF16)
    a = _dot(xn, win_ref[:, 0:512])
    u = _dot(xn, win_ref[:, 512:1024])
    v = _dot(xn, win_ref[:, 1024:1536])
    ga = _dot(xn, win_ref[:, 1536:2560])
    gb = _dot(xn, win_ref[:, 2560:3584])

    a3 = a.reshape(DEC_BATCH, DEC_SEQ, POOL_WIDTH)
    aext_ref[:, 0:HIST_ROWS, :] = hist_ref[...]
    aext_ref[:, HIST_ROWS:HIST_ROWS + DEC_SEQ, :] = a3
    pool_ref[...] = a3[:, DEC_SEQ - HIST_ROWS:DEC_SEQ, :]
    row = lax.broadcasted_iota(I32, (TL, 1), 0)
    pos = PAST_LEN + row % DEC_SEQ

    def load_shifted(k, g):
        sl = aext_ref[:, HIST_ROWS - k:HIST_ROWS - k + DEC_SEQ, g * POOL_GROUP_DIM:(g + 1) * POOL_GROUP_DIM]
        return sl.reshape(TL, POOL_GROUP_DIM)

    pa = _pool_windows(load_shifted, a, pos, wpool_ref, pscale_ref)

    vnf = _rms(v, gv_ref[...])
    vn_ref[...] = vnf
    vn = vnf.astype(BF16)
    rsel = (lax.broadcasted_iota(I32, (TL, GMLP_CHUNK), 1)
            == lax.broadcasted_iota(I32, (TL, GMLP_CHUNK), 0) % DEC_SEQ)
    rsel_b = jnp.where(rsel, 1.0, 0.0).astype(BF16)
    csel = (lax.broadcasted_iota(I32, (GMLP_CHUNK, TL), 0)
            == lax.broadcasted_iota(I32, (GMLP_CHUNK, TL), 1) % DEC_SEQ)
    csel_b = jnp.where(csel, 1.0, 0.0).astype(BF16)
    rr = lax.broadcasted_iota(I32, (TL, TL), 0)
    cc = lax.broadcasted_iota(I32, (TL, TL), 1)
    keep = (rr // DEC_SEQ == cc // DEC_SEQ) & (cc <= rr)
    s_parts = []
    for hd in range(GMLP_HEADS):
        cols = slice(hd * GMLP_HEAD_DIM, (hd + 1) * GMLP_HEAD_DIM)
        wrow = _dot(rsel_b, ws_ref[hd].astype(BF16)).astype(BF16)
        wfull = _dot(wrow, csel_b)
        wblk = jnp.where(keep, wfull, 0.0).astype(BF16)
        s_parts.append(_dot(wblk, vn[:, cols]) + bsp_ref[:, hd:hd + 1])
    pb = u * jnp.concatenate(s_parts, axis=-1)

    h = x + _dot(_merge(pa, pb, ga, gb, wupp_ref, wupg_ref).astype(BF16), wout_ref[...])
    h_ref[...] = h
    lg = _router_logits(_rms(h, gffn_ref[...]), wrt_ref, brt_ref)
    _assign(lg, ltri_ref, run_ref, rec_out_ref, pay_out_ref)
    cnt_ref[...] = run_ref[...]


def _dispatch_kernel(pad_start_ref, pad_len_ref, nvalid_ref,
                     d_ref, hp_ref, hs_ref, gffn_ref, xb_ref, row_ref, zero_ref, sems):
    i = pl.program_id(0)
    slot = i % 2
    rows = row_ref.at[slot]

    def fill_and_scatter(h_ref):
        for c in range(TL // DMA_CHUNK):
            r0 = c * DMA_CHUNK
            hn = _rms(h_ref[r0:r0 + DMA_CHUNK, :], gffn_ref[...])
            rows[r0:r0 + DMA_CHUNK] = hn.reshape(DMA_CHUNK, SUBLANES, LANES)
            for t in range(r0, r0 + DMA_CHUNK):
                _row_copy(rows, t, xb_ref, d_ref[0, t], sems.at[slot]).start(priority=t % 2)

    @pl.when(i < NT_PROMPT)
    def _():
        fill_and_scatter(hp_ref)

    @pl.when(i == NT_PROMPT)
    def _():
        fill_and_scatter(hs_ref)

    def drain(s):
        def body(t, c):
            _row_copy(row_ref.at[s], t, xb_ref, 0, sems.at[s]).wait()
            return c
        lax.fori_loop(0, TL, body, 0, unroll=DMA_UNROLL)

    @pl.when(i > 0)
    def _():
        drain(1 - slot)

    @pl.when(i == NT_TOTAL - 1)
    def _():
        drain(slot)
        zero_ref[...] = jnp.zeros_like(zero_ref)
        bits = [1 << k for k in reversed(range(BM.bit_length() - 1))]

        def pad_copies(fn):
            for e in range(N_BUCKETS):
                off = pad_start_ref[e]
                n = pad_len_ref[e]
                for bit in bits:
                    take = jnp.bitwise_and(n, bit)

                    @pl.when(take != 0)
                    def _(off=off, bit=bit):
                        fn(pltpu.make_async_copy(zero_ref.at[pl.ds(0, bit)], xb_ref.at[pl.ds(off, bit)],
                                                 sems.at[slot]))
                    off = off + take
            for blk in range(NB_MIN, NB_MAX):
                @pl.when(blk >= nvalid_ref[0])
                def _(blk=blk):
                    for half in range(BM // ZERO_ROWS):
                        fn(pltpu.make_async_copy(
                            zero_ref, xb_ref.at[pl.ds(blk * BM + half * ZERO_ROWS, ZERO_ROWS)], sems.at[slot]))

        pad_copies(lambda cp: cp.start())
        pad_copies(lambda cp: cp.wait())


def _experts_kernel(blk_lo_ref, blk_hi_ref, nvalid_ref, xb_ref,
                    wg_lo_ref, wu_lo_ref, wd_lo_ref, wg_hi_ref, wu_hi_ref, wd_hi_ref,
                    yb_ref, wgb_lo_ref, wub_lo_ref, wdb_lo_ref, wgb_hi_ref, wub_hi_ref, wdb_hi_ref):
    i = pl.program_id(0)
    prev = jnp.maximum(i - 1, 0)

    @pl.when((i == 0) | (blk_lo_ref[i] != blk_lo_ref[prev]))
    def _():
        wgb_lo_ref[...] = wg_lo_ref[...].astype(BF16)
        wub_lo_ref[...] = wu_lo_ref[...].astype(BF16)
        wdb_lo_ref[...] = wd_lo_ref[...].astype(BF16)

    @pl.when((i == 0) | (blk_hi_ref[i] != blk_hi_ref[prev]))
    def _():
        wgb_hi_ref[...] = wg_hi_ref[...].astype(BF16)
        wub_hi_ref[...] = wu_hi_ref[...].astype(BF16)
        wdb_hi_ref[...] = wd_hi_ref[...].astype(BF16)

    @pl.when(i < nvalid_ref[0])
    def _():
        x = xb_ref[...].reshape(BM, D_MODEL).astype(BF16)

        def hidden(wgb, wub):
            hg = _dot(x, wgb[...])
            return ((hg * jax.nn.sigmoid(hg)) * _dot(x, wub[...])).astype(BF16)

        act_lo = hidden(wgb_lo_ref, wub_lo_ref)
        act_hi = hidden(wgb_hi_ref, wub_hi_ref)
        y_lo = _dot(act_lo, wdb_lo_ref[...])
        y_hi = _dot(act_hi, wdb_hi_ref[...])
        yb_ref[:, 0] = y_lo.reshape(BM, SUBLANES, LANES)
        yb_ref[:, 1] = y_hi.reshape(BM, SUBLANES, LANES)

    @pl.when(i >= nvalid_ref[0])
    def _():
        yb_ref[...] = jnp.zeros_like(yb_ref)


def _combine_kernel(d_ref, dn_ref, payp_ref, pays_ref, hp_ref, hs_ref, gfin_ref, yb_ref,
                    yp_ref, ys_ref, buf_ref, sems):
    i = pl.program_id(0)
    slot = i % 2

    @pl.when(i == 0)
    def _():
        def body(t, c):
            _row_copy(yb_ref, d_ref[0, t], buf_ref.at[slot], t, sems.at[slot]).start()
            return c
        lax.fori_loop(0, TL, body, 0, unroll=DMA_UNROLL)

    def drain(t, c):
        _row_copy(yb_ref, 0, buf_ref.at[slot], t, sems.at[slot]).wait()
        return c

    lax.fori_loop(0, TL, drain, 0, unroll=DMA_UNROLL)

    def finish_chunk(h_ref, pay_ref, out_ref, r0):
        rows = slice(r0, r0 + DMA_CHUNK)
        both = buf_ref.at[slot][rows]
        y_lo = both[:, 0].reshape(DMA_CHUNK, D_MODEL)
        y_hi = both[:, 1].reshape(DMA_CHUNK, D_MODEL)
        pay = pay_ref[rows, :]
        moe = y_lo * pay[:, PAY_LO:PAY_LO + 1] + y_hi * pay[:, PAY_HI:PAY_HI + 1]
        out_ref[rows, :] = _rms(h_ref[rows, :] + moe, gfin_ref[...])

    @pl.when(i < NT_PROMPT)
    def _():
        for c in range(TL // DMA_CHUNK):
            r0 = c * DMA_CHUNK
            finish_chunk(hp_ref, payp_ref, yp_ref, r0)
            for t in range(r0, r0 + DMA_CHUNK):
                _row_copy(yb_ref, dn_ref[0, t], buf_ref.at[1 - slot], t, sems.at[1 - slot]).start(priority=t % 2)

    @pl.when(i == NT_PROMPT)
    def _():
        for c in range(TL // DMA_CHUNK):
            finish_chunk(hs_ref, pays_ref, ys_ref, c * DMA_CHUNK)


def _const_spec(shape):
    zeros = (0,) * len(shape)
    return pl.BlockSpec(shape, lambda *_: zeros, pipeline_mode=pl.Buffered(1))


def kernel(x_prompt, x_sample, state_pool, g_mix, w_in, w_pool, pool_scale, g_v, w_spatial, b_spatial,
           w_up_pool, w_up_gmlp, w_out, g_ffn, w_group, b_group, w_router, b_router, w_gate, w_up,
           w_down, g_final):
    gmix = g_mix[0][None, :]
    win = w_in[0].astype(BF16)
    wpool = w_pool[0].astype(BF16)
    pscale = pool_scale[0][None, :]
    gv = g_v[0][None, :]
    ws = w_spatial[0]
    bsp_t = b_spatial[0].T
    wupp = w_up_pool[0].astype(BF16)
    wupg = w_up_gmlp[0].astype(BF16)
    wout = w_out[0].astype(BF16)
    gffn = g_ffn[0][None, :]
    n_route = N_GROUPS + N_EXPERTS
    wrt = jnp.pad(jnp.concatenate([w_group[0], w_router[0]], axis=1), ((0, 0), (0, LANES - n_route)))
    brt = jnp.pad(jnp.concatenate([b_group[0], b_router[0]]), (0, LANES - n_route))[None, :]
    wg = w_gate[0]
    wu = w_up[0]
    wd = w_down[0]
    gfin = g_final[None, :]
    bsp_prompt = jnp.tile(bsp_t, (TL // GMLP_CHUNK, 1))
    bsp_sample = jnp.tile(bsp_t[:DEC_SEQ], (DEC_BATCH, 1))
    hist = jnp.pad(state_pool[0], ((0, 0), (1, 0), (0, 0)))

    weight_specs = [
        _const_spec((1, D_MODEL)),
        _const_spec((D_MODEL, IN_COLS)),
        _const_spec((4, POOL_GROUP_DIM, POOL_GROUP_DIM)),
        _const_spec((1, POOL_WIDTH)),
        _const_spec((1, GMLP_WIDTH)),
        _const_spec((GMLP_HEADS, GMLP_CHUNK, GMLP_CHUNK)),
        _const_spec((TL, GMLP_HEADS)),
        _const_spec((POOL_WIDTH, D_MODEL)),
        _const_spec((GMLP_WIDTH, D_MODEL)),
        _const_spec((D_MODEL, D_MODEL)),
        _const_spec((1, D_MODEL)),
        _const_spec((D_MODEL, LANES)),
        _const_spec((1, LANES)),
    ]
    any_spec = pl.BlockSpec(memory_space=pl.ANY)
    route_scratch = [
        pltpu.VMEM((TL, TL), BF16),
        pltpu.VMEM((1, LANES), F32),
    ]

    def started(s):
        return jnp.minimum(s, NT_PROMPT - 1)

    def finished(s):
        return jnp.maximum(s - 1, 0)

    x_tiles = x_prompt.reshape(NT_PROMPT, TL, D_MODEL)
    h_p, rec_p, pay_p, pool_p, cnt_p = pl.pallas_call(
        _mixer_prompt_kernel,
        grid=(NT_PROMPT + 1,),
        in_specs=[
            pl.BlockSpec((None, TL, D_MODEL), lambda s: (started(s), 0, 0)),
            pl.BlockSpec((None, TL, D_MODEL), lambda s: (finished(s), 0, 0)),
        ] + weight_specs,
        out_specs=[
            pl.BlockSpec((None, TL, D_MODEL), lambda s: (finished(s), 0, 0)),
            pl.BlockSpec((None, REC_ROWS, TL), lambda s: (finished(s), 0, 0)),
            pl.BlockSpec((TL, LANES), lambda s: (finished(s), 0)),
            pl.BlockSpec((None, HIST_ROWS, POOL_WIDTH), lambda s: (started(s) // TILES_PER_SEQ, 0, 0)),
            pl.BlockSpec((1, LANES), lambda s: (0, 0)),
        ],
        out_shape=[
            jax.ShapeDtypeStruct((NT_PROMPT, TL, D_MODEL), F32),
            jax.ShapeDtypeStruct((NT_PROMPT, REC_ROWS, TL), I32),
            jax.ShapeDtypeStruct((T_PROMPT, LANES), F32),
            jax.ShapeDtypeStruct((BATCH, HIST_ROWS, POOL_WIDTH), F32),
            jax.ShapeDtypeStruct((1, LANES), F32),
        ],
        scratch_shapes=[
            pltpu.VMEM((HIST_ROWS + TL, POOL_WIDTH), F32),
            pltpu.VMEM((TL, GMLP_WIDTH), F32),
        ] + route_scratch + [pltpu.VMEM((TL, D_MODEL), BF16)],
        compiler_params=pltpu.CompilerParams(
            dimension_semantics=("arbitrary",), vmem_limit_bytes=VMEM_LIMIT),
        name="mixer_prompt",
    )(x_tiles, x_tiles, gmix, win, wpool, pscale, gv, ws, bsp_prompt, wupp, wupg, wout, gffn, wrt, brt)

    h_s, rec_s, pay_s, pool_s, vn_s, cnt = pl.pallas_call(
        _mixer_sample_kernel,
        grid=(1,),
        in_specs=[
            _const_spec((TL, D_MODEL)),
            _const_spec((DEC_BATCH, HIST_ROWS, POOL_WIDTH)),
            _const_spec((1, LANES)),
        ] + weight_specs,
        out_specs=[
            _const_spec((TL, D_MODEL)),
            pl.BlockSpec((None, REC_ROWS, TL), lambda i: (0, 0, 0)),
            _const_spec((TL, LANES)),
            _const_spec((DEC_BATCH, HIST_ROWS, POOL_WIDTH)),
            _const_spec((TL, GMLP_WIDTH)),
            _const_spec((1, LANES)),
        ],
        out_shape=[
            jax.ShapeDtypeStruct((T_SAMPLE, D_MODEL), F32),
            jax.ShapeDtypeStruct((1, REC_ROWS, TL), I32),
            jax.ShapeDtypeStruct((T_SAMPLE, LANES), F32),
            jax.ShapeDtypeStruct((DEC_BATCH, HIST_ROWS, POOL_WIDTH), F32),
            jax.ShapeDtypeStruct((T_SAMPLE, GMLP_WIDTH), F32),
            jax.ShapeDtypeStruct((1, LANES), F32),
        ],
        scratch_shapes=[
            pltpu.VMEM((DEC_BATCH, HIST_ROWS + DEC_SEQ, POOL_WIDTH), F32),
        ] + route_scratch,
        compiler_params=pltpu.CompilerParams(
            dimension_semantics=("arbitrary",), vmem_limit_bytes=VMEM_LIMIT),
        name="mixer_sample",
    )(x_sample.reshape(T_SAMPLE, D_MODEL), hist, cnt_p, gmix, win, wpool, pscale, gv, ws, bsp_sample,
      wupp, wupg, wout, gffn, wrt, brt)

    counts = cnt[0, :N_BUCKETS].astype(I32)
    nblk = (counts + BM - 1) // BM
    bend = jnp.cumsum(nblk)
    bstart = bend - nblk
    nvalid = bend[-1:]
    step = jnp.minimum(jnp.arange(NB_MAX, dtype=I32), nvalid - 1)
    bucket_ids = jnp.arange(N_BUCKETS, dtype=I32)
    blk_bucket = jnp.minimum(jnp.sum(step[:, None] >= bend[None, :], axis=1), N_BUCKETS - 1)
    expert_lo = jnp.array([g * EXPERTS_PER_GROUP + p for g in range(N_GROUPS) for p in PAIR_LO], I32)
    expert_hi = jnp.array([g * EXPERTS_PER_GROUP + p for g in range(N_GROUPS) for p in PAIR_HI], I32)
    in_bucket = blk_bucket[:, None] == bucket_ids[None, :]
    blk_lo = jnp.sum(jnp.where(in_bucket, expert_lo[None, :], 0), axis=1).astype(I32)
    blk_hi = jnp.sum(jnp.where(in_bucket, expert_hi[None, :], 0), axis=1).astype(I32)
    pad_start = bstart * BM + counts
    pad_len = nblk * BM - counts
    rec = jnp.concatenate([rec_p, rec_s], axis=0)
    first_slot = jnp.sum(jnp.where(rec[:, REC_BUCKET, :, None] == bucket_ids[None, None, :],
                                   (bstart * BM)[None, None, :], 0), axis=-1)
    d = (first_slot + rec[:, REC_RANK])[:, None, :]

    last_p = NT_PROMPT - 1

    def hp_map(i, *_):
        return (jnp.minimum(i, last_p), 0, 0)

    def smem_tile(index_map):
        return pl.BlockSpec((None, 1, TL), index_map, memory_space=pltpu.SMEM)

    xb = pl.pallas_call(
        _dispatch_kernel,
        grid_spec=pltpu.PrefetchScalarGridSpec(
            num_scalar_prefetch=3,
            grid=(NT_TOTAL,),
            in_specs=[
                smem_tile(lambda i, *_: (i, 0, 0)),
                pl.BlockSpec((None, TL, D_MODEL), hp_map),
                pl.BlockSpec((TL, D_MODEL), lambda i, *_: (0, 0)),
                pl.BlockSpec((1, D_MODEL), lambda i, *_: (0, 0)),
            ],
            out_specs=any_spec,
            scratch_shapes=[
                pltpu.VMEM((2, TL, SUBLANES, LANES), F32),
                pltpu.VMEM((ZERO_ROWS, SUBLANES, LANES), F32),
                pltpu.SemaphoreType.DMA((2,)),
            ],
        ),
        out_shape=jax.ShapeDtypeStruct((N_SLOTS, SUBLANES, LANES), F32),
        compiler_params=pltpu.CompilerParams(
            dimension_semantics=("arbitrary",), vmem_limit_bytes=VMEM_LIMIT),
        name="dispatch",
    )(pad_start, pad_len, nvalid, d, h_p, h_s, gffn)

    def xb_map(i, lo, hi, nv):
        return (jnp.minimum(i, nv[0] - 1), 0, 0)

    def lo_map(i, lo, hi, nv):
        return (lo[i], 0, 0)

    def hi_map(i, lo, hi, nv):
        return (hi[i], 0, 0)

    in_proj = (None, D_MODEL, D_EXPERT)
    out_proj = (None, D_EXPERT, D_MODEL)
    yb = pl.pallas_call(
        _experts_kernel,
        grid_spec=pltpu.PrefetchScalarGridSpec(
            num_scalar_prefetch=3,
            grid=(NB_MAX,),
            in_specs=[
                pl.BlockSpec((BM, SUBLANES, LANES), xb_map),
                pl.BlockSpec(in_proj, lo_map), pl.BlockSpec(in_proj, lo_map), pl.BlockSpec(out_proj, lo_map),
                pl.BlockSpec(in_proj, hi_map), pl.BlockSpec(in_proj, hi_map), pl.BlockSpec(out_proj, hi_map),
            ],
            out_specs=pl.BlockSpec((BM, 2, SUBLANES, LANES), lambda i, lo, hi, nv: (i, 0, 0, 0)),
            scratch_shapes=2 * [
                pltpu.VMEM((D_MODEL, D_EXPERT), BF16),
                pltpu.VMEM((D_MODEL, D_EXPERT), BF16),
                pltpu.VMEM((D_EXPERT, D_MODEL), BF16),
            ],
        ),
        out_shape=jax.ShapeDtypeStruct((N_SLOTS, 2, SUBLANES, LANES), F32),
        compiler_params=pltpu.CompilerParams(
            dimension_semantics=("arbitrary",), vmem_limit_bytes=VMEM_LIMIT),
        name="experts",
    )(blk_lo, blk_hi, nvalid, xb, wg, wu, wd, wg, wu, wd)

    def next_tile(i):
        return (jnp.minimum(i + 1, NT_TOTAL - 1), 0, 0)

    y_p, y_s = pl.pallas_call(
        _combine_kernel,
        grid=(NT_TOTAL,),
        in_specs=[
            smem_tile(lambda i: (i, 0, 0)),
            smem_tile(next_tile),
            pl.BlockSpec((TL, LANES), lambda i: (jnp.minimum(i, last_p), 0)),
            pl.BlockSpec((TL, LANES), lambda i: (0, 0)),
            pl.BlockSpec((None, TL, D_MODEL), hp_map),
            pl.BlockSpec((TL, D_MODEL), lambda i: (0, 0)),
            pl.BlockSpec((1, D_MODEL), lambda i: (0, 0)),
            any_spec,
        ],
        out_specs=[
            pl.BlockSpec((None, TL, D_MODEL), hp_map),
            pl.BlockSpec((TL, D_MODEL), lambda i: (0, 0)),
        ],
        out_shape=[
            jax.ShapeDtypeStruct((NT_PROMPT, TL, D_MODEL), F32),
            jax.ShapeDtypeStruct((T_SAMPLE, D_MODEL), F32),
        ],
        scratch_shapes=[
            pltpu.VMEM((2, TL, 2, SUBLANES, LANES), F32),
            pltpu.SemaphoreType.DMA((2,)),
        ],
        compiler_params=pltpu.CompilerParams(
            dimension_semantics=("arbitrary",), vmem_limit_bytes=VMEM_LIMIT),
        name="combine",
    )(d, d, pay_p, pay_s, h_p, h_s, gfin, yb)

    y_sample = y_s.reshape(DEC_BATCH, DEC_SEQ, D_MODEL)
    new_pool_prompt = pool_p[None, :, 1:, :]
    new_pool_sample = pool_s[None, :, 1:, :]
    new_gmlp_v_sample = vn_s.reshape(1, DEC_BATCH, DEC_SEQ, GMLP_WIDTH)
    y_prompt = y_p.reshape(BATCH, SEQ, D_MODEL)
    return (y_prompt, y_sample, new_pool_prompt, new_pool_sample, new_gmlp_v_sample)
```

```python
import jax
import jax.numpy as jnp
from jax import lax
from jax.experimental import pallas as pl
from jax.experimental.pallas import tpu as pltpu

F32 = jnp.float32
BF16 = jnp.bfloat16
I32 = jnp.int32

D_MODEL = 1024
SEQ = 16384
BATCH = 2
DEC_BATCH = 16
DEC_SEQ = 32
PAST_LEN = 1024
POOL_WIDTH = 512
POOL_GROUP_DIM = 128
POOL_WINDOWS = (2, 4, 8, 16)
HIST_ROWS = 16
GMLP_WIDTH = 512
GMLP_HEADS = 4
GMLP_HEAD_DIM = 128
GMLP_CHUNK = 128
IN_COLS = POOL_WIDTH + 2 * GMLP_WIDTH + 2 * D_MODEL
N_GROUPS = 4
EXPERTS_PER_GROUP = 4
N_EXPERTS = 16
D_EXPERT = 512
EPS = 1e-6

LANES = 128
SUBLANES = 8
assert D_MODEL == SUBLANES * LANES
TL = 512
BM = 512
T_PROMPT = BATCH * SEQ
T_SAMPLE = DEC_BATCH * DEC_SEQ
T_TOTAL = T_PROMPT + T_SAMPLE
NT_PROMPT = T_PROMPT // TL
TILES_PER_SEQ = SEQ // TL
NT_TOTAL = T_TOTAL // TL
assert T_SAMPLE == TL
PAIR_LO = (0, 0, 0, 1, 1, 2)
PAIR_HI = (1, 2, 3, 2, 3, 3)
N_PAIRS = len(PAIR_LO)
N_BUCKETS = N_GROUPS * N_PAIRS
NB_MIN = T_TOTAL // BM
NB_MAX = (T_TOTAL + N_BUCKETS * (BM - 1)) // BM
N_SLOTS = NB_MAX * BM
ZERO_ROWS = BM // 2
DMA_UNROLL = 8
DMA_CHUNK = 64
VMEM_LIMIT = 56 * 1024 * 1024

REC_BUCKET, REC_RANK = 0, 1
REC_ROWS = 8
PAY_LO, PAY_HI = 0, 1


def _dot(a, b):
    return jnp.dot(a, b, preferred_element_type=F32)


def _rms(x, g):
    return x * lax.rsqrt(jnp.mean(x * x, axis=-1, keepdims=True) + EPS) * g


def _split_bf16(x):
    hi = x.astype(BF16)
    lo = (x - hi.astype(F32)).astype(BF16)
    return hi, lo


def _pack_bf16_pair(lo, hi):
    lo_bits = lax.bitcast_convert_type(lo.astype(BF16).astype(F32), I32)
    hi_bits = lax.bitcast_convert_type(hi.astype(BF16).astype(F32), I32)
    return lax.shift_right_logical(lo_bits, jnp.int32(16)) | hi_bits


def _unpack_bf16_pair(packed):
    lo = lax.bitcast_convert_type(lax.shift_left(packed, jnp.int32(16)), F32)
    hi = lax.bitcast_convert_type(packed & jnp.int32(-65536), F32)
    return lo, hi


def _row_copy(src_ref, s, dst_ref, d, sem):
    return pltpu.make_async_copy(src_ref.at[pl.ds(s, 1)], dst_ref.at[pl.ds(d, 1)], sem)


def _pool_windows(load_shifted, a, pos, wpool_ref, pscale_ref):
    outs = []
    for g, w in enumerate(POOL_WINDOWS):
        cols = slice(g * POOL_GROUP_DIM, (g + 1) * POOL_GROUP_DIM)
        acc = a[:, cols]
        for k in range(1, w):
            acc = acc + load_shifted(k, g)
        cnt = jnp.minimum(pos + 1, w).astype(F32)
        d = (acc / cnt - a[:, cols]).astype(BF16)
        outs.append(_dot(d, wpool_ref[g]))
    return jnp.concatenate(outs, axis=-1) * pscale_ref[...]


def _merge(pa, pb, ga, gb, wupp_ref, wupg_ref):
    return (jax.nn.sigmoid(ga) * _dot(pa.astype(BF16), wupp_ref[...])
            + jax.nn.sigmoid(gb) * _dot(pb.astype(BF16), wupg_ref[...]))


def _init_ltri(ltri_ref):
    r = lax.broadcasted_iota(I32, (TL, TL), 0)
    c = lax.broadcasted_iota(I32, (TL, TL), 1)
    ltri_ref[...] = jnp.where(c < r, 1.0, 0.0).astype(BF16)


def _router_logits(hn, wrt_ref, brt_ref):
    hi, lo = _split_bf16(hn)
    whi, wlo = _split_bf16(wrt_ref[...])
    hi_w = _dot(hi, jnp.concatenate([whi, wlo], axis=1))
    return hi_w[:, 0:LANES] + hi_w[:, LANES:2 * LANES] + _dot(lo, whi) + brt_ref[...]


def _assign(lg, ltri_ref, run_ref, rec_out_ref, pay_out_ref, counted=True):
    lane = lax.broadcasted_iota(I32, lg.shape, 1)
    neg = jnp.float32(-jnp.inf)
    big = jnp.int32(1 << 20)

    def first_argmax(v, vmax):
        return jnp.min(jnp.where(v == vmax, lane, big), axis=-1, keepdims=True)

    gmask = lane < N_GROUPS
    glm = jnp.where(gmask, lg, neg)
    gmax = jnp.max(glm, axis=-1, keepdims=True)
    grp = first_argmax(glm, gmax)
    p_g = 1.0 / jnp.sum(jnp.where(gmask, jnp.exp(lg - gmax), 0.0), axis=-1, keepdims=True)

    lo_lane = N_GROUPS + EXPERTS_PER_GROUP * grp
    elm = jnp.where((lane >= lo_lane) & (lane < lo_lane + EXPERTS_PER_GROUP), lg, neg)
    v1 = jnp.max(elm, axis=-1, keepdims=True)
    i1 = first_argmax(elm, v1)
    elm2 = jnp.where(lane == i1, neg, elm)
    v2 = jnp.max(elm2, axis=-1, keepdims=True)
    i2 = first_argmax(elm2, v2)
    ex2 = jnp.exp(v2 - v1)
    inv = 1.0 / (1.0 + ex2)
    w_top = inv * p_g
    w_second = ex2 * inv * p_g

    loc1 = i1 - lo_lane
    loc2 = i2 - lo_lane
    top_is_lo = loc1 < loc2
    e_lo = jnp.minimum(loc1, loc2)
    e_hi = jnp.maximum(loc1, loc2)
    pair = jnp.where(e_lo == 0, 0, jnp.where(e_lo == 1, 3, 5)) + e_hi - e_lo - 1
    bucket = grp * N_PAIRS + pair
    w_lo = jnp.where(top_is_lo, w_top, w_second)
    w_hi = jnp.where(top_is_lo, w_second, w_top)

    sel = lane == bucket
    onehot = jnp.where(sel, 1.0, 0.0)
    before = _dot(ltri_ref[...], onehot.astype(BF16)) + run_ref[...]
    rank = jnp.sum(jnp.where(sel, before, 0.0), axis=-1, keepdims=True)
    run_ref[...] = run_ref[...] + jnp.where(counted, jnp.sum(onehot, axis=0, keepdims=True), 0.0)

    rec = jnp.where(lane == REC_BUCKET, bucket.astype(F32), jnp.where(lane == REC_RANK, rank, 0.0))
    rec_out_ref[...] = rec.T[0:REC_ROWS, :].astype(I32)
    pay_out_ref[...] = jnp.where(lane == PAY_LO, w_lo, jnp.where(lane == PAY_HI, w_hi, 0.0))


def _mixer_prompt_kernel(x_ref, xprev_ref, gmix_ref, win_ref, wpool_ref, pscale_ref, gv_ref, ws_ref,
                         bsp_ref, wupp_ref, wupg_ref, wout_ref, gffn_ref, wrt_ref, brt_ref,
                         h_ref, rec_out_ref, pay_out_ref, pool_ref, cnt_ref,
                         aext_ref, s_ref, ltri_ref, run_ref, merged_ref):
    s = pl.program_id(0)

    @pl.when(s == 0)
    def _():
        _init_ltri(ltri_ref)
        run_ref[...] = jnp.zeros_like(run_ref)
        aext_ref[TL:TL + HIST_ROWS, :] = jnp.zeros((HIST_ROWS, POOL_WIDTH), F32)
        merged_ref[...] = jnp.zeros_like(merged_ref)

    j = jnp.minimum(s, NT_PROMPT - 1) % TILES_PER_SEQ

    h = xprev_ref[...] + _dot(merged_ref[...], wout_ref[...])
    xn = _rms(x_ref[...], gmix_ref[...]).astype(BF16)
    a = _dot(xn, win_ref[:, 0:512])
    h_ref[...] = h
    hn = _rms(h, gffn_ref[...])
    lg = _router_logits(hn, wrt_ref, brt_ref)

    aext_ref[0:HIST_ROWS, :] = jnp.where(j > 0, aext_ref[TL:TL + HIST_ROWS, :], 0.0)
    aext_ref[HIST_ROWS:HIST_ROWS + TL, :] = a
    pool_ref[...] = a[TL - HIST_ROWS:TL, :]
    pos = j * TL + lax.broadcasted_iota(I32, (TL, 1), 0)

    def load_shifted(k, g):
        return aext_ref[HIST_ROWS - k:HIST_ROWS - k + TL, g * POOL_GROUP_DIM:(g + 1) * POOL_GROUP_DIM]

    v = _dot(xn, win_ref[:, 1024:1536])
    pa = _pool_windows(load_shifted, a, pos, wpool_ref, pscale_ref)
    u = _dot(xn, win_ref[:, 512:1024])
    _assign(lg, ltri_ref, run_ref, rec_out_ref, pay_out_ref, counted=s > 0)
    cnt_ref[...] = run_ref[...]
    ga = _dot(xn, win_ref[:, 1536:2560])

    vn = _rms(v, gv_ref[...]).astype(BF16)
    rr = lax.broadcasted_iota(I32, (GMLP_CHUNK, GMLP_CHUNK), 0)
    cc = lax.broadcasted_iota(I32, (GMLP_CHUNK, GMLP_CHUNK), 1)
    for hd in range(GMLP_HEADS):
        cols = slice(hd * GMLP_HEAD_DIM, (hd + 1) * GMLP_HEAD_DIM)
        wsm = jnp.where(cc <= rr, ws_ref[hd], 0.0).astype(BF16)
        for c in range(TL // GMLP_CHUNK):
            rows = slice(c * GMLP_CHUNK, (c + 1) * GMLP_CHUNK)
            s_ref[rows, cols] = _dot(wsm, vn[rows, cols]) + bsp_ref[rows, hd:hd + 1]
    pb = u * s_ref[...]
    gb = _dot(xn, win_ref[:, 2560:3584])

    merged_ref[...] = _merge(pa, pb, ga, gb, wupp_ref, wupg_ref).astype(BF16)


def _mixer_sample_kernel(x_ref, hist_ref, run0_ref, gmix_ref, win_ref, wpool_ref, pscale_ref, gv_ref,
                         ws_ref, bsp_ref, wupp_ref, wupg_ref, wout_ref, gffn_ref, wrt_ref, brt_ref,
                         h_ref, rec_out_ref, pay_out_ref, pool_ref, vn_ref, cnt_ref,
                         aext_ref, ltri_ref, run_ref):
    _init_ltri(ltri_ref)
    run_ref[...] = run0_ref[...]

    x = x_ref[...]
    xn = _rms(x, gmix_ref[...]).astype(BF16)
    a = _dot(xn, win_ref[:, 0:512])
    u = _dot(xn, win_ref[:, 512:1024])
    v = _dot(xn, win_ref[:, 1024:1536])
    ga = _dot(xn, win_ref[:, 1536:2560])
    gb = _dot(xn, win_ref[:, 2560:3584])

    a3 = a.reshape(DEC_BATCH, DEC_SEQ, POOL_WIDTH)
    aext_ref[:, 0:HIST_ROWS, :] = hist_ref[...]
    aext_ref[:, HIST_ROWS:HIST_ROWS + DEC_SEQ, :] = a3
    pool_ref[...] = a3[:, DEC_SEQ - HIST_ROWS:DEC_SEQ, :]
    row = lax.broadcasted_iota(I32, (TL, 1), 0)
    pos = PAST_LEN + row % DEC_SEQ

    def load_shifted(k, g):
        sl = aext_ref[:, HIST_ROWS - k:HIST_ROWS - k + DEC_SEQ, g * POOL_GROUP_DIM:(g + 1) * POOL_GROUP_DIM]
        return sl.reshape(TL, POOL_GROUP_DIM)

    pa = _pool_windows(load_shifted, a, pos, wpool_ref, pscale_ref)

    vnf = _rms(v, gv_ref[...])
    vn_ref[...] = vnf
    vn = vnf.astype(BF16)
    rsel = (lax.broadcasted_iota(I32, (TL, GMLP_CHUNK), 1)
            == lax.broadcasted_iota(I32, (TL, GMLP_CHUNK), 0) % DEC_SEQ)
    rsel_b = jnp.where(rsel, 1.0, 0.0).astype(BF16)
    csel = (lax.broadcasted_iota(I32, (GMLP_CHUNK, TL), 0)
            == lax.broadcasted_iota(I32, (GMLP_CHUNK, TL), 1) % DEC_SEQ)
    csel_b = jnp.where(csel, 1.0, 0.0).astype(BF16)
    rr = lax.broadcasted_iota(I32, (TL, TL), 0)
    cc = lax.broadcasted_iota(I32, (TL, TL), 1)
    keep = (rr // DEC_SEQ == cc // DEC_SEQ) & (cc <= rr)
    s_parts = []
    for hd in range(GMLP_HEADS):
        cols = slice(hd * GMLP_HEAD_DIM, (hd + 1) * GMLP_HEAD_DIM)
        wrow = _dot(rsel_b, ws_ref[hd].astype(BF16)).astype(BF16)
        wfull = _dot(wrow, csel_b)
        wblk = jnp.where(keep, wfull, 0.0).astype(BF16)
        s_parts.append(_dot(wblk, vn[:, cols]) + bsp_ref[:, hd:hd + 1])
    pb = u * jnp.concatenate(s_parts, axis=-1)

    h = x + _dot(_merge(pa, pb, ga, gb, wupp_ref, wupg_ref).astype(BF16), wout_ref[...])
    h_ref[...] = h
    lg = _router_logits(_rms(h, gffn_ref[...]), wrt_ref, brt_ref)
    _assign(lg, ltri_ref, run_ref, rec_out_ref, pay_out_ref)
    cnt_ref[...] = run_ref[...]


def _dispatch_kernel(pad_start_ref, pad_len_ref, nvalid_ref,
                     d_ref, hp_ref, hs_ref, gffn_ref, xb_ref, row_ref, zero_ref, sems):
    i = pl.program_id(0)
    slot = i % 2
    rows = row_ref.at[slot]

    def fill_and_scatter(h_ref):
        for c in range(TL // DMA_CHUNK):
            r0 = c * DMA_CHUNK
            hn = _rms(h_ref[r0:r0 + DMA_CHUNK, :], gffn_ref[...])
            rows[r0:r0 + DMA_CHUNK] = hn.reshape(DMA_CHUNK, SUBLANES, LANES)
            for t in range(r0, r0 + DMA_CHUNK):
                _row_copy(rows, t, xb_ref, d_ref[0, t], sems.at[slot]).start(priority=t % 2)

    @pl.when(i < NT_PROMPT)
    def _():
        fill_and_scatter(hp_ref)

    @pl.when(i == NT_PROMPT)
    def _():
        fill_and_scatter(hs_ref)

    def drain(s):
        def body(t, c):
            _row_copy(row_ref.at[s], t, xb_ref, 0, sems.at[s]).wait()
            return c
        lax.fori_loop(0, TL, body, 0, unroll=DMA_UNROLL)

    @pl.when(i > 0)
    def _():
        drain(1 - slot)

    @pl.when(i == NT_TOTAL - 1)
    def _():
        drain(slot)
        zero_ref[...] = jnp.zeros_like(zero_ref)
        bits = [1 << k for k in reversed(range(BM.bit_length() - 1))]

        def pad_copies(fn):
            for e in range(N_BUCKETS):
                off = pad_start_ref[e]
                n = pad_len_ref[e]
                for bit in bits:
                    take = jnp.bitwise_and(n, bit)

                    @pl.when(take != 0)
                    def _(off=off, bit=bit):
                        fn(pltpu.make_async_copy(zero_ref.at[pl.ds(0, bit)], xb_ref.at[pl.ds(off, bit)],
                                                 sems.at[slot]))
                    off = off + take
            for blk in range(NB_MIN, NB_MAX):
                @pl.when(blk >= nvalid_ref[0])
                def _(blk=blk):
                    for half in range(BM // ZERO_ROWS):
                        fn(pltpu.make_async_copy(
                            zero_ref, xb_ref.at[pl.ds(blk * BM + half * ZERO_ROWS, ZERO_ROWS)], sems.at[slot]))

        pad_copies(lambda cp: cp.start())
        pad_copies(lambda cp: cp.wait())


def _experts_kernel(blk_lo_ref, blk_hi_ref, nvalid_ref, xb_ref,
                    wg_lo_ref, wu_lo_ref, wd_lo_ref, wg_hi_ref, wu_hi_ref, wd_hi_ref,
                    yb_ref, wgb_lo_ref, wub_lo_ref, wdb_lo_ref, wgb_hi_ref, wub_hi_ref, wdb_hi_ref):
    i = pl.program_id(0)
    prev = jnp.maximum(i - 1, 0)

    @pl.when((i == 0) | (blk_lo_ref[i] != blk_lo_ref[prev]))
    def _():
        wgb_lo_ref[...] = wg_lo_ref[...].astype(BF16)
        wub_lo_ref[...] = wu_lo_ref[...].astype(BF16)
        wdb_lo_ref[...] = wd_lo_ref[...].astype(BF16)

    @pl.when((i == 0) | (blk_hi_ref[i] != blk_hi_ref[prev]))
    def _():
        wgb_hi_ref[...] = wg_hi_ref[...].astype(BF16)
        wub_hi_ref[...] = wu_hi_ref[...].astype(BF16)
        wdb_hi_ref[...] = wd_hi_ref[...].astype(BF16)

    @pl.when(i < nvalid_ref[0])
    def _():
        x = xb_ref[...].reshape(BM, D_MODEL).astype(BF16)

        def hidden(wgb, wub):
            hg = _dot(x, wgb[...])
            return ((hg * jax.nn.sigmoid(hg)) * _dot(x, wub[...])).astype(BF16)

        act_lo = hidden(wgb_lo_ref, wub_lo_ref)
        act_hi = hidden(wgb_hi_ref, wub_hi_ref)
        y_lo = _dot(act_lo, wdb_lo_ref[...])
        y_hi = _dot(act_hi, wdb_hi_ref[...])
        yb_ref[...] = _pack_bf16_pair(y_lo, y_hi).reshape(BM, SUBLANES, LANES)

    @pl.when(i >= nvalid_ref[0])
    def _():
        yb_ref[...] = jnp.zeros_like(yb_ref)


def _combine_kernel(d_ref, dn_ref, payp_ref, pays_ref, hp_ref, hs_ref, gfin_ref, yb_ref,
                    yp_ref, ys_ref, buf_ref, sems):
    i = pl.program_id(0)
    slot = i % 2

    @pl.when(i == 0)
    def _():
        def body(t, c):
            _row_copy(yb_ref, d_ref[0, t], buf_ref.at[slot], t, sems.at[slot]).start()
            return c
        lax.fori_loop(0, TL, body, 0, unroll=DMA_UNROLL)

    def drain(t, c):
        _row_copy(yb_ref, 0, buf_ref.at[slot], t, sems.at[slot]).wait()
        return c

    lax.fori_loop(0, TL, drain, 0, unroll=DMA_UNROLL)

    def finish_chunk(h_ref, pay_ref, out_ref, r0):
        rows = slice(r0, r0 + DMA_CHUNK)
        y_lo, y_hi = _unpack_bf16_pair(buf_ref.at[slot][rows].reshape(DMA_CHUNK, D_MODEL))
        pay = pay_ref[rows, :]
        moe = y_lo * pay[:, PAY_LO:PAY_LO + 1] + y_hi * pay[:, PAY_HI:PAY_HI + 1]
        out_ref[rows, :] = _rms(h_ref[rows, :] + moe, gfin_ref[...])

    @pl.when(i < NT_PROMPT)
    def _():
        for c in range(TL // DMA_CHUNK):
            r0 = c * DMA_CHUNK
            finish_chunk(hp_ref, payp_ref, yp_ref, r0)
            for t in range(r0, r0 + DMA_CHUNK):
                _row_copy(yb_ref, dn_ref[0, t], buf_ref.at[1 - slot], t, sems.at[1 - slot]).start(priority=t % 2)

    @pl.when(i == NT_PROMPT)
    def _():
        for c in range(TL // DMA_CHUNK):
            finish_chunk(hs_ref, pays_ref, ys_ref, c * DMA_CHUNK)


def _const_spec(shape):
    zeros = (0,) * len(shape)
    return pl.BlockSpec(shape, lambda *_: zeros, pipeline_mode=pl.Buffered(1))


def kernel(x_prompt, x_sample, state_pool, g_mix, w_in, w_pool, pool_scale, g_v, w_spatial, b_spatial,
           w_up_pool, w_up_gmlp, w_out, g_ffn, w_group, b_group, w_router, b_router, w_gate, w_up,
           w_down, g_final):
    gmix = g_mix[0][None, :]
    win = w_in[0].astype(BF16)
    wpool = w_pool[0].astype(BF16)
    pscale = pool_scale[0][None, :]
    gv = g_v[0][None, :]
    ws = w_spatial[0]
    bsp_t = b_spatial[0].T
    wupp = w_up_pool[0].astype(BF16)
    wupg = w_up_gmlp[0].astype(BF16)
    wout = w_out[0].astype(BF16)
    gffn = g_ffn[0][None, :]
    n_route = N_GROUPS + N_EXPERTS
    wrt = jnp.pad(jnp.concatenate([w_group[0], w_router[0]], axis=1), ((0, 0), (0, LANES - n_route)))
    brt = jnp.pad(jnp.concatenate([b_group[0], b_router[0]]), (0, LANES - n_route))[None, :]
    wg = w_gate[0]
    wu = w_up[0]
    wd = w_down[0]
    gfin = g_final[None, :]
    bsp_prompt = jnp.tile(bsp_t, (TL // GMLP_CHUNK, 1))
    bsp_sample = jnp.tile(bsp_t[:DEC_SEQ], (DEC_BATCH, 1))
    hist = jnp.pad(state_pool[0], ((0, 0), (1, 0), (0, 0)))

    weight_specs = [
        _const_spec((1, D_MODEL)),
        _const_spec((D_MODEL, IN_COLS)),
        _const_spec((4, POOL_GROUP_DIM, POOL_GROUP_DIM)),
        _const_spec((1, POOL_WIDTH)),
        _const_spec((1, GMLP_WIDTH)),
        _const_spec((GMLP_HEADS, GMLP_CHUNK, GMLP_CHUNK)),
        _const_spec((TL, GMLP_HEADS)),
        _const_spec((POOL_WIDTH, D_MODEL)),
        _const_spec((GMLP_WIDTH, D_MODEL)),
        _const_spec((D_MODEL, D_MODEL)),
        _const_spec((1, D_MODEL)),
        _const_spec((D_MODEL, LANES)),
        _const_spec((1, LANES)),
    ]
    any_spec = pl.BlockSpec(memory_space=pl.ANY)
    route_scratch = [
        pltpu.VMEM((TL, TL), BF16),
        pltpu.VMEM((1, LANES), F32),
    ]

    def started(s):
        return jnp.minimum(s, NT_PROMPT - 1)

    def finished(s):
        return jnp.maximum(s - 1, 0)

    x_tiles = x_prompt.reshape(NT_PROMPT, TL, D_MODEL)
    h_p, rec_p, pay_p, pool_p, cnt_p = pl.pallas_call(
        _mixer_prompt_kernel,
        grid=(NT_PROMPT + 1,),
        in_specs=[
            pl.BlockSpec((None, TL, D_MODEL), lambda s: (started(s), 0, 0)),
            pl.BlockSpec((None, TL, D_MODEL), lambda s: (finished(s), 0, 0)),
        ] + weight_specs,
        out_specs=[
            pl.BlockSpec((None, TL, D_MODEL), lambda s: (finished(s), 0, 0)),
            pl.BlockSpec((None, REC_ROWS, TL), lambda s: (finished(s), 0, 0)),
            pl.BlockSpec((TL, LANES), lambda s: (finished(s), 0)),
            pl.BlockSpec((None, HIST_ROWS, POOL_WIDTH), lambda s: (started(s) // TILES_PER_SEQ, 0, 0)),
            pl.BlockSpec((1, LANES), lambda s: (0, 0)),
        ],
        out_shape=[
            jax.ShapeDtypeStruct((NT_PROMPT, TL, D_MODEL), F32),
            jax.ShapeDtypeStruct((NT_PROMPT, REC_ROWS, TL), I32),
            jax.ShapeDtypeStruct((T_PROMPT, LANES), F32),
            jax.ShapeDtypeStruct((BATCH, HIST_ROWS, POOL_WIDTH), F32),
            jax.ShapeDtypeStruct((1, LANES), F32),
        ],
        scratch_shapes=[
            pltpu.VMEM((HIST_ROWS + TL, POOL_WIDTH), F32),
            pltpu.VMEM((TL, GMLP_WIDTH), F32),
        ] + route_scratch + [pltpu.VMEM((TL, D_MODEL), BF16)],
        compiler_params=pltpu.CompilerParams(
            dimension_semantics=("arbitrary",), vmem_limit_bytes=VMEM_LIMIT),
        name="mixer_prompt",
    )(x_tiles, x_tiles, gmix, win, wpool, pscale, gv, ws, bsp_prompt, wupp, wupg, wout, gffn, wrt, brt)

    h_s, rec_s, pay_s, pool_s, vn_s, cnt = pl.pallas_call(
        _mixer_sample_kernel,
        grid=(1,),
        in_specs=[
            _const_spec((TL, D_MODEL)),
            _const_spec((DEC_BATCH, HIST_ROWS, POOL_WIDTH)),
            _const_spec((1, LANES)),
        ] + weight_specs,
        out_specs=[
            _const_spec((TL, D_MODEL)),
            pl.BlockSpec((None, REC_ROWS, TL), lambda i: (0, 0, 0)),
            _const_spec((TL, LANES)),
            _const_spec((DEC_BATCH, HIST_ROWS, POOL_WIDTH)),
            _const_spec((TL, GMLP_WIDTH)),
            _const_spec((1, LANES)),
        ],
        out_shape=[
            jax.ShapeDtypeStruct((T_SAMPLE, D_MODEL), F32),
            jax.ShapeDtypeStruct((1, REC_ROWS, TL), I32),
            jax.ShapeDtypeStruct((T_SAMPLE, LANES), F32),
            jax.ShapeDtypeStruct((DEC_BATCH, HIST_ROWS, POOL_WIDTH), F32),
            jax.ShapeDtypeStruct((T_SAMPLE, GMLP_WIDTH), F32),
            jax.ShapeDtypeStruct((1, LANES), F32),
        ],
        scratch_shapes=[
            pltpu.VMEM((DEC_BATCH, HIST_ROWS + DEC_SEQ, POOL_WIDTH), F32),
        ] + route_scratch,
        compiler_params=pltpu.CompilerParams(
            dimension_semantics=("arbitrary",), vmem_limit_bytes=VMEM_LIMIT),
        name="mixer_sample",
    )(x_sample.reshape(T_SAMPLE, D_MODEL), hist, cnt_p, gmix, win, wpool, pscale, gv, ws, bsp_sample,
      wupp, wupg, wout, gffn, wrt, brt)

    counts = cnt[0, :N_BUCKETS].astype(I32)
    nblk = (counts + BM - 1) // BM
    bend = jnp.cumsum(nblk)
    bstart = bend - nblk
    nvalid = bend[-1:]
    step = jnp.minimum(jnp.arange(NB_MAX, dtype=I32), nvalid - 1)
    bucket_ids = jnp.arange(N_BUCKETS, dtype=I32)
    blk_bucket = jnp.minimum(jnp.sum(step[:, None] >= bend[None, :], axis=1), N_BUCKETS - 1)
    expert_lo = jnp.array([g * EXPERTS_PER_GROUP + p for g in range(N_GROUPS) for p in PAIR_LO], I32)
    expert_hi = jnp.array([g * EXPERTS_PER_GROUP + p for g in range(N_GROUPS) for p in PAIR_HI], I32)
    in_bucket = blk_bucket[:, None] == bucket_ids[None, :]
    blk_lo = jnp.sum(jnp.where(in_bucket, expert_lo[None, :], 0), axis=1).astype(I32)
    blk_hi = jnp.sum(jnp.where(in_bucket, expert_hi[None, :], 0), axis=1).astype(I32)
    pad_start = bstart * BM + counts
    pad_len = nblk * BM - counts
    rec = jnp.concatenate([rec_p, rec_s], axis=0)
    first_slot = jnp.sum(jnp.where(rec[:, REC_BUCKET, :, None] == bucket_ids[None, None, :],
                                   (bstart * BM)[None, None, :], 0), axis=-1)
    d = (first_slot + rec[:, REC_RANK])[:, None, :]

    last_p = NT_PROMPT - 1

    def hp_map(i, *_):
        return (jnp.minimum(i, last_p), 0, 0)

    def smem_tile(index_map):
        return pl.BlockSpec((None, 1, TL), index_map, memory_space=pltpu.SMEM)

    xb = pl.pallas_call(
        _dispatch_kernel,
        grid_spec=pltpu.PrefetchScalarGridSpec(
            num_scalar_prefetch=3,
            grid=(NT_TOTAL,),
            in_specs=[
                smem_tile(lambda i, *_: (i, 0, 0)),
                pl.BlockSpec((None, TL, D_MODEL), hp_map),
                pl.BlockSpec((TL, D_MODEL), lambda i, *_: (0, 0)),
                pl.BlockSpec((1, D_MODEL), lambda i, *_: (0, 0)),
            ],
            out_specs=any_spec,
            scratch_shapes=[
                pltpu.VMEM((2, TL, SUBLANES, LANES), F32),
                pltpu.VMEM((ZERO_ROWS, SUBLANES, LANES), F32),
                pltpu.SemaphoreType.DMA((2,)),
            ],
        ),
        out_shape=jax.ShapeDtypeStruct((N_SLOTS, SUBLANES, LANES), F32),
        compiler_params=pltpu.CompilerParams(
            dimension_semantics=("arbitrary",), vmem_limit_bytes=VMEM_LIMIT),
        name="dispatch",
    )(pad_start, pad_len, nvalid, d, h_p, h_s, gffn)

    def xb_map(i, lo, hi, nv):
        return (jnp.minimum(i, nv[0] - 1), 0, 0)

    def lo_map(i, lo, hi, nv):
        return (lo[i], 0, 0)

    def hi_map(i, lo, hi, nv):
        return (hi[i], 0, 0)

    in_proj = (None, D_MODEL, D_EXPERT)
    out_proj = (None, D_EXPERT, D_MODEL)
    yb = pl.pallas_call(
        _experts_kernel,
        grid_spec=pltpu.PrefetchScalarGridSpec(
            num_scalar_prefetch=3,
            grid=(NB_MAX,),
            in_specs=[
                pl.BlockSpec((BM, SUBLANES, LANES), xb_map),
                pl.BlockSpec(in_proj, lo_map), pl.BlockSpec(in_proj, lo_map), pl.BlockSpec(out_proj, lo_map),
                pl.BlockSpec(in_proj, hi_map), pl.BlockSpec(in_proj, hi_map), pl.BlockSpec(out_proj, hi_map),
            ],
            out_specs=pl.BlockSpec((BM, SUBLANES, LANES), lambda i, lo, hi, nv: (i, 0, 0)),
            scratch_shapes=2 * [
                pltpu.VMEM((D_MODEL, D_EXPERT), BF16),
                pltpu.VMEM((D_MODEL, D_EXPERT), BF16),
                pltpu.VMEM((D_EXPERT, D_MODEL), BF16),
            ],
        ),
        out_shape=jax.ShapeDtypeStruct((N_SLOTS, SUBLANES, LANES), I32),
        compiler_params=pltpu.CompilerParams(
            dimension_semantics=("arbitrary",), vmem_limit_bytes=VMEM_LIMIT),
        name="experts",
    )(blk_lo, blk_hi, nvalid, xb, wg, wu, wd, wg, wu, wd)

    def next_tile(i):
        return (jnp.minimum(i + 1, NT_TOTAL - 1), 0, 0)

    y_p, y_s = pl.pallas_call(
        _combine_kernel,
        grid=(NT_TOTAL,),
        in_specs=[
            smem_tile(lambda i: (i, 0, 0)),
            smem_tile(next_tile),
            pl.BlockSpec((TL, LANES), lambda i: (jnp.minimum(i, last_p), 0)),
            pl.BlockSpec((TL, LANES), lambda i: (0, 0)),
            pl.BlockSpec((None, TL, D_MODEL), hp_map),
            pl.BlockSpec((TL, D_MODEL), lambda i: (0, 0)),
            pl.BlockSpec((1, D_MODEL), lambda i: (0, 0)),
            any_spec,
        ],
        out_specs=[
            pl.BlockSpec((None, TL, D_MODEL), hp_map),
            pl.BlockSpec((TL, D_MODEL), lambda i: (0, 0)),
        ],
        out_shape=[
            jax.ShapeDtypeStruct((NT_PROMPT, TL, D_MODEL), F32),
            jax.ShapeDtypeStruct((T_SAMPLE, D_MODEL), F32),
        ],
        scratch_shapes=[
            pltpu.VMEM((2, TL, SUBLANES, LANES), I32),
            pltpu.SemaphoreType.DMA((2,)),
        ],
        compiler_params=pltpu.CompilerParams(
            dimension_semantics=("arbitrary",), vmem_limit_bytes=VMEM_LIMIT),
        name="combine",
    )(d, d, pay_p, pay_s, h_p, h_s, gfin, yb)

    y_sample = y_s.reshape(DEC_BATCH, DEC_SEQ, D_MODEL)
    new_pool_prompt = pool_p[None, :, 1:, :]
    new_pool_sample = pool_s[None, :, 1:, :]
    new_gmlp_v_sample = vn_s.reshape(1, DEC_BATCH, DEC_SEQ, GMLP_WIDTH)
    y_prompt = y_p.reshape(BATCH, SEQ, D_MODEL)
    return (y_prompt, y_sample, new_pool_prompt, new_pool_sample, new_gmlp_v_sample)
```

```python
import jax
import jax.numpy as jnp
from jax import lax
from jax.experimental import pallas as pl
from jax.experimental.pallas import tpu as pltpu

F32 = jnp.float32
BF16 = jnp.bfloat16
I32 = jnp.int32

D_MODEL = 1024
SEQ = 16384
BATCH = 2
DEC_BATCH = 16
DEC_SEQ = 32
PAST_LEN = 1024
POOL_WIDTH = 512
POOL_GROUP_DIM = 128
POOL_WINDOWS = (2, 4, 8, 16)
HIST_ROWS = 16
GMLP_WIDTH = 512
GMLP_HEADS = 4
GMLP_HEAD_DIM = 128
GMLP_CHUNK = 128
IN_COLS = POOL_WIDTH + 2 * GMLP_WIDTH + 2 * D_MODEL
N_GROUPS = 4
EXPERTS_PER_GROUP = 4
N_EXPERTS = 16
D_EXPERT = 512
EPS = 1e-6

LANES = 128
SUBLANES = 8
assert D_MODEL == SUBLANES * LANES
TL = 512
BM = 512
T_PROMPT = BATCH * SEQ
T_SAMPLE = DEC_BATCH * DEC_SEQ
T_TOTAL = T_PROMPT + T_SAMPLE
NT_PROMPT = T_PROMPT // TL
TILES_PER_SEQ = SEQ // TL
NT_TOTAL = T_TOTAL // TL
assert T_SAMPLE == TL
PAIR_LO = (0, 0, 0, 1, 1, 2)
PAIR_HI = (1, 2, 3, 2, 3, 3)
N_PAIRS = len(PAIR_LO)
N_BUCKETS = N_GROUPS * N_PAIRS
NB_MIN = T_TOTAL // BM
NB_MAX = (T_TOTAL + N_BUCKETS * (BM - 1)) // BM
N_SLOTS = NB_MAX * BM
ZERO_ROWS = BM // 2
TD = 2 * TL
ND_PROMPT = T_PROMPT // TD
DMA_UNROLL = 8
DMA_CHUNK = 64
VMEM_LIMIT = 56 * 1024 * 1024

REC_BUCKET, REC_RANK = 0, 1
REC_ROWS = 8
PAY_LO, PAY_HI = 0, 1


def _dot(a, b):
    return jnp.dot(a, b, preferred_element_type=F32)


def _rms(x, g):
    return x * lax.rsqrt(jnp.mean(x * x, axis=-1, keepdims=True) + EPS) * g


def _split_bf16(x):
    hi = x.astype(BF16)
    lo = (x - hi.astype(F32)).astype(BF16)
    return hi, lo


def _pack_bf16_pair(lo, hi):
    lo_bits = lax.bitcast_convert_type(lo.astype(BF16).astype(F32), I32)
    hi_bits = lax.bitcast_convert_type(hi.astype(BF16).astype(F32), I32)
    return lax.shift_right_logical(lo_bits, jnp.int32(16)) | hi_bits


def _unpack_bf16_pair(packed):
    lo = lax.bitcast_convert_type(lax.shift_left(packed, jnp.int32(16)), F32)
    hi = lax.bitcast_convert_type(packed & jnp.int32(-65536), F32)
    return lo, hi


def _row_copy(src_ref, s, dst_ref, d, sem):
    return pltpu.make_async_copy(src_ref.at[pl.ds(s, 1)], dst_ref.at[pl.ds(d, 1)], sem)


def _pool_windows(load_shifted, a, pos, wpool_ref, pscale_ref):
    outs = []
    for g, w in enumerate(POOL_WINDOWS):
        cols = slice(g * POOL_GROUP_DIM, (g + 1) * POOL_GROUP_DIM)
        acc = a[:, cols]
        for k in range(1, w):
            acc = acc + load_shifted(k, g)
        cnt = jnp.minimum(pos + 1, w).astype(F32)
        d = (acc / cnt - a[:, cols]).astype(BF16)
        outs.append(_dot(d, wpool_ref[g]))
    return jnp.concatenate(outs, axis=-1) * pscale_ref[...]


def _merge(pa, pb, ga, gb, wupp_ref, wupg_ref):
    return (jax.nn.sigmoid(ga) * _dot(pa.astype(BF16), wupp_ref[...])
            + jax.nn.sigmoid(gb) * _dot(pb.astype(BF16), wupg_ref[...]))


def _init_ltri(ltri_ref):
    r = lax.broadcasted_iota(I32, (TL, TL), 0)
    c = lax.broadcasted_iota(I32, (TL, TL), 1)
    ltri_ref[...] = jnp.where(c < r, 1.0, 0.0).astype(BF16)


def _router_logits(hn, wrt_ref, brt_ref):
    hi, lo = _split_bf16(hn)
    whi, wlo = _split_bf16(wrt_ref[...])
    hi_w = _dot(hi, jnp.concatenate([whi, wlo], axis=1))
    return hi_w[:, 0:LANES] + hi_w[:, LANES:2 * LANES] + _dot(lo, whi) + brt_ref[...]


def _assign(lg, ltri_ref, run_ref, rec_out_ref, pay_out_ref, counted=True):
    lane = lax.broadcasted_iota(I32, lg.shape, 1)
    neg = jnp.float32(-jnp.inf)
    big = jnp.int32(1 << 20)

    def first_argmax(v, vmax):
        return jnp.min(jnp.where(v == vmax, lane, big), axis=-1, keepdims=True)

    gmask = lane < N_GROUPS
    glm = jnp.where(gmask, lg, neg)
    gmax = jnp.max(glm, axis=-1, keepdims=True)
    grp = first_argmax(glm, gmax)
    p_g = 1.0 / jnp.sum(jnp.where(gmask, jnp.exp(lg - gmax), 0.0), axis=-1, keepdims=True)

    lo_lane = N_GROUPS + EXPERTS_PER_GROUP * grp
    elm = jnp.where((lane >= lo_lane) & (lane < lo_lane + EXPERTS_PER_GROUP), lg, neg)
    v1 = jnp.max(elm, axis=-1, keepdims=True)
    i1 = first_argmax(elm, v1)
    elm2 = jnp.where(lane == i1, neg, elm)
    v2 = jnp.max(elm2, axis=-1, keepdims=True)
    i2 = first_argmax(elm2, v2)
    ex2 = jnp.exp(v2 - v1)
    inv = 1.0 / (1.0 + ex2)
    w_top = inv * p_g
    w_second = ex2 * inv * p_g

    loc1 = i1 - lo_lane
    loc2 = i2 - lo_lane
    top_is_lo = loc1 < loc2
    e_lo = jnp.minimum(loc1, loc2)
    e_hi = jnp.maximum(loc1, loc2)
    pair = jnp.where(e_lo == 0, 0, jnp.where(e_lo == 1, 3, 5)) + e_hi - e_lo - 1
    bucket = grp * N_PAIRS + pair
    w_lo = jnp.where(top_is_lo, w_top, w_second)
    w_hi = jnp.where(top_is_lo, w_second, w_top)

    sel = lane == bucket
    onehot = jnp.where(sel, 1.0, 0.0)
    before = _dot(ltri_ref[...], onehot.astype(BF16)) + run_ref[...]
    rank = jnp.sum(jnp.where(sel, before, 0.0), axis=-1, keepdims=True)
    run_ref[...] = run_ref[...] + jnp.where(counted, jnp.sum(onehot, axis=0, keepdims=True), 0.0)

    rec = jnp.where(lane == REC_BUCKET, bucket.astype(F32), jnp.where(lane == REC_RANK, rank, 0.0))
    rec_out_ref[...] = rec.T[0:REC_ROWS, :].astype(I32)
    pay_out_ref[...] = jnp.where(lane == PAY_LO, w_lo, jnp.where(lane == PAY_HI, w_hi, 0.0))


def _mixer_prompt_kernel(x_ref, xprev_ref, gmix_ref, win_ref, wpool_ref, pscale_ref, gv_ref, ws_ref,
                         bsp_ref, wupp_ref, wupg_ref, wout_ref, gffn_ref, wrt_ref, brt_ref,
                         h_ref, rec_out_ref, pay_out_ref, pool_ref, cnt_ref,
                         aext_ref, s_ref, ltri_ref, run_ref, merged_ref):
    s = pl.program_id(0)

    @pl.when(s == 0)
    def _():
        _init_ltri(ltri_ref)
        run_ref[...] = jnp.zeros_like(run_ref)
        aext_ref[TL:TL + HIST_ROWS, :] = jnp.zeros((HIST_ROWS, POOL_WIDTH), F32)
        merged_ref[...] = jnp.zeros_like(merged_ref)

    j = jnp.minimum(s, NT_PROMPT - 1) % TILES_PER_SEQ

    h = xprev_ref[...] + _dot(merged_ref[...], wout_ref[...])
    xn = _rms(x_ref[...], gmix_ref[...]).astype(BF16)
    a = _dot(xn, win_ref[:, 0:512])
    h_ref[...] = h
    hn = _rms(h, gffn_ref[...])
    lg = _router_logits(hn, wrt_ref, brt_ref)

    aext_ref[0:HIST_ROWS, :] = jnp.where(j > 0, aext_ref[TL:TL + HIST_ROWS, :], 0.0)
    aext_ref[HIST_ROWS:HIST_ROWS + TL, :] = a
    pool_ref[...] = a[TL - HIST_ROWS:TL, :]
    pos = j * TL + lax.broadcasted_iota(I32, (TL, 1), 0)

    def load_shifted(k, g):
        return aext_ref[HIST_ROWS - k:HIST_ROWS - k + TL, g * POOL_GROUP_DIM:(g + 1) * POOL_GROUP_DIM]

    v = _dot(xn, win_ref[:, 1024:1536])
    pa = _pool_windows(load_shifted, a, pos, wpool_ref, pscale_ref)
    u = _dot(xn, win_ref[:, 512:1024])
    _assign(lg, ltri_ref, run_ref, rec_out_ref, pay_out_ref, counted=s > 0)
    cnt_ref[...] = run_ref[...]
    ga = _dot(xn, win_ref[:, 1536:2560])

    vn = _rms(v, gv_ref[...]).astype(BF16)
    rr = lax.broadcasted_iota(I32, (GMLP_CHUNK, GMLP_CHUNK), 0)
    cc = lax.broadcasted_iota(I32, (GMLP_CHUNK, GMLP_CHUNK), 1)
    for hd in range(GMLP_HEADS):
        cols = slice(hd * GMLP_HEAD_DIM, (hd + 1) * GMLP_HEAD_DIM)
        wsm = jnp.where(cc <= rr, ws_ref[hd], 0.0).astype(BF16)
        chunks = [vn[c * GMLP_CHUNK:(c + 1) * GMLP_CHUNK, cols] for c in range(TL // GMLP_CHUNK)]
        mixed = _dot(wsm, jnp.concatenate(chunks, axis=1))
        for c in range(TL // GMLP_CHUNK):
            rows = slice(c * GMLP_CHUNK, (c + 1) * GMLP_CHUNK)
            s_ref[rows, cols] = (mixed[:, c * GMLP_HEAD_DIM:(c + 1) * GMLP_HEAD_DIM]
                                 + bsp_ref[rows, hd:hd + 1])
    pb = u * s_ref[...]
    gb = _dot(xn, win_ref[:, 2560:3584])

    merged_ref[...] = _merge(pa, pb, ga, gb, wupp_ref, wupg_ref).astype(BF16)


def _mixer_sample_kernel(x_ref, hist_ref, run0_ref, gmix_ref, win_ref, wpool_ref, pscale_ref, gv_ref,
                         ws_ref, bsp_ref, wupp_ref, wupg_ref, wout_ref, gffn_ref, wrt_ref, brt_ref,
                         h_ref, rec_out_ref, pay_out_ref, pool_ref, vn_ref, cnt_ref,
                         aext_ref, ltri_ref, run_ref):
    _init_ltri(ltri_ref)
    run_ref[...] = run0_ref[...]

    x = x_ref[...]
    xn = _rms(x, gmix_ref[...]).astype(BF16)
    a = _dot(xn, win_ref[:, 0:512])
    u = _dot(xn, win_ref[:, 512:1024])
    v = _dot(xn, win_ref[:, 1024:1536])
    ga = _dot(xn, win_ref[:, 1536:2560])
    gb = _dot(xn, win_ref[:, 2560:3584])

    a3 = a.reshape(DEC_BATCH, DEC_SEQ, POOL_WIDTH)
    aext_ref[:, 0:HIST_ROWS, :] = hist_ref[...]
    aext_ref[:, HIST_ROWS:HIST_ROWS + DEC_SEQ, :] = a3
    pool_ref[...] = a3[:, DEC_SEQ - HIST_ROWS:DEC_SEQ, :]
    row = lax.broadcasted_iota(I32, (TL, 1), 0)
    pos = PAST_LEN + row % DEC_SEQ

    def load_shifted(k, g):
        sl = aext_ref[:, HIST_ROWS - k:HIST_ROWS - k + DEC_SEQ, g * POOL_GROUP_DIM:(g + 1) * POOL_GROUP_DIM]
        return sl.reshape(TL, POOL_GROUP_DIM)

    pa = _pool_windows(load_shifted, a, pos, wpool_ref, pscale_ref)

    vnf = _rms(v, gv_ref[...])
    vn_ref[...] = vnf
    vn = vnf.astype(BF16)
    rsel = (lax.broadcasted_iota(I32, (TL, GMLP_CHUNK), 1)
            == lax.broadcasted_iota(I32, (TL, GMLP_CHUNK), 0) % DEC_SEQ)
    rsel_b = jnp.where(rsel, 1.0, 0.0).astype(BF16)
    csel = (lax.broadcasted_iota(I32, (GMLP_CHUNK, TL), 0)
            == lax.broadcasted_iota(I32, (GMLP_CHUNK, TL), 1) % DEC_SEQ)
    csel_b = jnp.where(csel, 1.0, 0.0).astype(BF16)
    rr = lax.broadcasted_iota(I32, (TL, TL), 0)
    cc = lax.broadcasted_iota(I32, (TL, TL), 1)
    keep = (rr // DEC_SEQ == cc // DEC_SEQ) & (cc <= rr)
    s_parts = []
    for hd in range(GMLP_HEADS):
        cols = slice(hd * GMLP_HEAD_DIM, (hd + 1) * GMLP_HEAD_DIM)
        wrow = _dot(rsel_b, ws_ref[hd].astype(BF16)).astype(BF16)
        wfull = _dot(wrow, csel_b)
        wblk = jnp.where(keep, wfull, 0.0).astype(BF16)
        s_parts.append(_dot(wblk, vn[:, cols]) + bsp_ref[:, hd:hd + 1])
    pb = u * jnp.concatenate(s_parts, axis=-1)

    h = x + _dot(_merge(pa, pb, ga, gb, wupp_ref, wupg_ref).astype(BF16), wout_ref[...])
    h_ref[...] = h
    lg = _router_logits(_rms(h, gffn_ref[...]), wrt_ref, brt_ref)
    _assign(lg, ltri_ref, run_ref, rec_out_ref, pay_out_ref)
    cnt_ref[...] = run_ref[...]


def _dispatch_kernel(pad_start_ref, pad_len_ref, nvalid_ref,
                     dp_ref, ds_ref, hp_ref, hs_ref, gffn_ref, xb_ref, row_ref, zero_ref, sems):
    i = pl.program_id(0)
    slot = i % 2
    rows = row_ref.at[slot]

    def fill_and_scatter(h_ref, d_ref, n_rows):
        for c in range(n_rows // DMA_CHUNK):
            r0 = c * DMA_CHUNK
            hn = _rms(h_ref[r0:r0 + DMA_CHUNK, :], gffn_ref[...])
            rows[r0:r0 + DMA_CHUNK] = hn.reshape(DMA_CHUNK, SUBLANES, LANES)
            for t in range(r0, r0 + DMA_CHUNK):
                _row_copy(rows, t, xb_ref, d_ref[0, t], sems.at[slot]).start(priority=t % 2)

    @pl.when(i < ND_PROMPT)
    def _():
        fill_and_scatter(hp_ref, dp_ref, TD)

    @pl.when(i == ND_PROMPT)
    def _():
        fill_and_scatter(hs_ref, ds_ref, TL)

    def drain(s, n_rows):
        def body(t, c):
            _row_copy(row_ref.at[s], t, xb_ref, 0, sems.at[s]).wait()
            return c
        lax.fori_loop(0, n_rows, body, 0, unroll=DMA_UNROLL)

    @pl.when(i > 0)
    def _():
        drain(1 - slot, TD)

    @pl.when(i == ND_PROMPT)
    def _():
        drain(slot, TL)
        zero_ref[...] = jnp.zeros_like(zero_ref)
        bits = [1 << k for k in reversed(range(BM.bit_length() - 1))]

        def pad_copies(fn):
            for e in range(N_BUCKETS):
                off = pad_start_ref[e]
                n = pad_len_ref[e]
                for bit in bits:
                    take = jnp.bitwise_and(n, bit)

                    @pl.when(take != 0)
                    def _(off=off, bit=bit):
                        fn(pltpu.make_async_copy(zero_ref.at[pl.ds(0, bit)], xb_ref.at[pl.ds(off, bit)],
                                                 sems.at[slot]))
                    off = off + take
            for blk in range(NB_MIN, NB_MAX):
                @pl.when(blk >= nvalid_ref[0])
                def _(blk=blk):
                    for half in range(BM // ZERO_ROWS):
                        fn(pltpu.make_async_copy(
                            zero_ref, xb_ref.at[pl.ds(blk * BM + half * ZERO_ROWS, ZERO_ROWS)], sems.at[slot]))

        pad_copies(lambda cp: cp.start())
        pad_copies(lambda cp: cp.wait())


def _experts_kernel(blk_lo_ref, blk_hi_ref, blk_used_ref, nvalid_ref, xb_ref,
                    wg_lo_ref, wu_lo_ref, wd_lo_ref, wg_hi_ref, wu_hi_ref, wd_hi_ref,
                    yb_ref, wgb_lo_ref, wub_lo_ref, wdb_lo_ref, wgb_hi_ref, wub_hi_ref, wdb_hi_ref):
    i = pl.program_id(0)
    prev = jnp.maximum(i - 1, 0)

    @pl.when((i == 0) | (blk_lo_ref[i] != blk_lo_ref[prev]))
    def _():
        wgb_lo_ref[...] = wg_lo_ref[...].astype(BF16)
        wub_lo_ref[...] = wu_lo_ref[...].astype(BF16)
        wdb_lo_ref[...] = wd_lo_ref[...].astype(BF16)

    @pl.when((i == 0) | (blk_hi_ref[i] != blk_hi_ref[prev]))
    def _():
        wgb_hi_ref[...] = wg_hi_ref[...].astype(BF16)
        wub_hi_ref[...] = wu_hi_ref[...].astype(BF16)
        wdb_hi_ref[...] = wd_hi_ref[...].astype(BF16)

    def run(rows):
        x = xb_ref[0:rows].reshape(rows, D_MODEL).astype(BF16)

        def hidden(wgb, wub):
            hg = _dot(x, wgb[...])
            return ((hg * jax.nn.sigmoid(hg)) * _dot(x, wub[...])).astype(BF16)

        act_lo = hidden(wgb_lo_ref, wub_lo_ref)
        act_hi = hidden(wgb_hi_ref, wub_hi_ref)
        y_lo = _dot(act_lo, wdb_lo_ref[...])
        y_hi = _dot(act_hi, wdb_hi_ref[...])
        yb_ref[0:rows] = _pack_bf16_pair(y_lo, y_hi).reshape(rows, SUBLANES, LANES)

    used = blk_used_ref[i]

    @pl.when(used > BM // 2)
    def _():
        run(BM)

    @pl.when((used > 0) & (used <= BM // 2))
    def _():
        run(BM // 2)
        yb_ref[BM // 2:BM] = jnp.zeros((BM // 2, SUBLANES, LANES), I32)

    @pl.when(used == 0)
    def _():
        yb_ref[...] = jnp.zeros_like(yb_ref)


def _combine_kernel(d_ref, dn_ref, payp_ref, pays_ref, hp_ref, hs_ref, gfin_ref, yb_ref,
                    yp_ref, ys_ref, buf_ref, sems):
    i = pl.program_id(0)
    slot = i % 2

    @pl.when(i == 0)
    def _():
        def body(t, c):
            _row_copy(yb_ref, d_ref[0, t], buf_ref.at[slot], t, sems.at[slot]).start()
            return c
        lax.fori_loop(0, TD, body, 0, unroll=DMA_UNROLL)

    def drain(t, c):
        _row_copy(yb_ref, 0, buf_ref.at[slot], t, sems.at[slot]).wait()
        return c

    lax.fori_loop(0, TD, drain, 0, unroll=DMA_UNROLL)

    def finish_chunk(h_ref, pay_ref, out_ref, r0):
        rows = slice(r0, r0 + DMA_CHUNK)
        y_lo, y_hi = _unpack_bf16_pair(buf_ref.at[slot][rows].reshape(DMA_CHUNK, D_MODEL))
        pay = pay_ref[rows, :]
        moe = y_lo * pay[:, PAY_LO:PAY_LO + 1] + y_hi * pay[:, PAY_HI:PAY_HI + 1]
        out_ref[rows, :] = _rms(h_ref[rows, :] + moe, gfin_ref[...])

    @pl.when(i < ND_PROMPT)
    def _():
        for c in range(TD // DMA_CHUNK):
            r0 = c * DMA_CHUNK
            finish_chunk(hp_ref, payp_ref, yp_ref, r0)
            for t in range(r0, r0 + DMA_CHUNK):
                _row_copy(yb_ref, dn_ref[0, t], buf_ref.at[1 - slot], t, sems.at[1 - slot]).start(priority=t % 2)

    @pl.when(i == ND_PROMPT)
    def _():
        for c in range(TL // DMA_CHUNK):
            finish_chunk(hs_ref, pays_ref, ys_ref, c * DMA_CHUNK)


def _const_spec(shape):
    zeros = (0,) * len(shape)
    return pl.BlockSpec(shape, lambda *_: zeros, pipeline_mode=pl.Buffered(1))


def kernel(x_prompt, x_sample, state_pool, g_mix, w_in, w_pool, pool_scale, g_v, w_spatial, b_spatial,
           w_up_pool, w_up_gmlp, w_out, g_ffn, w_group, b_group, w_router, b_router, w_gate, w_up,
           w_down, g_final):
    gmix = g_mix[0][None, :]
    win = w_in[0].astype(BF16)
    wpool = w_pool[0].astype(BF16)
    pscale = pool_scale[0][None, :]
    gv = g_v[0][None, :]
    ws = w_spatial[0]
    bsp_t = b_spatial[0].T
    wupp = w_up_pool[0].astype(BF16)
    wupg = w_up_gmlp[0].astype(BF16)
    wout = w_out[0].astype(BF16)
    gffn = g_ffn[0][None, :]
    n_route = N_GROUPS + N_EXPERTS
    wrt = jnp.pad(jnp.concatenate([w_group[0], w_router[0]], axis=1), ((0, 0), (0, LANES - n_route)))
    brt = jnp.pad(jnp.concatenate([b_group[0], b_router[0]]), (0, LANES - n_route))[None, :]
    wg = w_gate[0]
    wu = w_up[0]
    wd = w_down[0]
    gfin = g_final[None, :]
    bsp_prompt = jnp.tile(bsp_t, (TL // GMLP_CHUNK, 1))
    bsp_sample = jnp.tile(bsp_t[:DEC_SEQ], (DEC_BATCH, 1))
    hist = jnp.pad(state_pool[0], ((0, 0), (1, 0), (0, 0)))

    weight_specs = [
        _const_spec((1, D_MODEL)),
        _const_spec((D_MODEL, IN_COLS)),
        _const_spec((4, POOL_GROUP_DIM, POOL_GROUP_DIM)),
        _const_spec((1, POOL_WIDTH)),
        _const_spec((1, GMLP_WIDTH)),
        _const_spec((GMLP_HEADS, GMLP_CHUNK, GMLP_CHUNK)),
        _const_spec((TL, GMLP_HEADS)),
        _const_spec((POOL_WIDTH, D_MODEL)),
        _const_spec((GMLP_WIDTH, D_MODEL)),
        _const_spec((D_MODEL, D_MODEL)),
        _const_spec((1, D_MODEL)),
        _const_spec((D_MODEL, LANES)),
        _const_spec((1, LANES)),
    ]
    any_spec = pl.BlockSpec(memory_space=pl.ANY)
    route_scratch = [
        pltpu.VMEM((TL, TL), BF16),
        pltpu.VMEM((1, LANES), F32),
    ]

    def started(s):
        return jnp.minimum(s, NT_PROMPT - 1)

    def finished(s):
        return jnp.maximum(s - 1, 0)

    x_tiles = x_prompt.reshape(NT_PROMPT, TL, D_MODEL)
    h_p, rec_p, pay_p, pool_p, cnt_p = pl.pallas_call(
        _mixer_prompt_kernel,
        grid=(NT_PROMPT + 1,),
        in_specs=[
            pl.BlockSpec((None, TL, D_MODEL), lambda s: (started(s), 0, 0)),
            pl.BlockSpec((None, TL, D_MODEL), lambda s: (finished(s), 0, 0)),
        ] + weight_specs,
        out_specs=[
            pl.BlockSpec((None, TL, D_MODEL), lambda s: (finished(s), 0, 0)),
            pl.BlockSpec((None, REC_ROWS, TL), lambda s: (finished(s), 0, 0)),
            pl.BlockSpec((TL, LANES), lambda s: (finished(s), 0)),
            pl.BlockSpec((None, HIST_ROWS, POOL_WIDTH), lambda s: (started(s) // TILES_PER_SEQ, 0, 0)),
            pl.BlockSpec((1, LANES), lambda s: (0, 0)),
        ],
        out_shape=[
            jax.ShapeDtypeStruct((NT_PROMPT, TL, D_MODEL), F32),
            jax.ShapeDtypeStruct((NT_PROMPT, REC_ROWS, TL), I32),
            jax.ShapeDtypeStruct((T_PROMPT, LANES), F32),
            jax.ShapeDtypeStruct((BATCH, HIST_ROWS, POOL_WIDTH), F32),
            jax.ShapeDtypeStruct((1, LANES), F32),
        ],
        scratch_shapes=[
            pltpu.VMEM((HIST_ROWS + TL, POOL_WIDTH), F32),
            pltpu.VMEM((TL, GMLP_WIDTH), F32),
        ] + route_scratch + [pltpu.VMEM((TL, D_MODEL), BF16)],
        compiler_params=pltpu.CompilerParams(
            dimension_semantics=("arbitrary",), vmem_limit_bytes=VMEM_LIMIT),
        name="mixer_prompt",
    )(x_tiles, x_tiles, gmix, win, wpool, pscale, gv, ws, bsp_prompt, wupp, wupg, wout, gffn, wrt, brt)

    h_s, rec_s, pay_s, pool_s, vn_s, cnt = pl.pallas_call(
        _mixer_sample_kernel,
        grid=(1,),
        in_specs=[
            _const_spec((TL, D_MODEL)),
            _const_spec((DEC_BATCH, HIST_ROWS, POOL_WIDTH)),
            _const_spec((1, LANES)),
        ] + weight_specs,
        out_specs=[
            _const_spec((TL, D_MODEL)),
            pl.BlockSpec((None, REC_ROWS, TL), lambda i: (0, 0, 0)),
            _const_spec((TL, LANES)),
            _const_spec((DEC_BATCH, HIST_ROWS, POOL_WIDTH)),
            _const_spec((TL, GMLP_WIDTH)),
            _const_spec((1, LANES)),
        ],
        out_shape=[
            jax.ShapeDtypeStruct((T_SAMPLE, D_MODEL), F32),
            jax.ShapeDtypeStruct((1, REC_ROWS, TL), I32),
            jax.ShapeDtypeStruct((T_SAMPLE, LANES), F32),
            jax.ShapeDtypeStruct((DEC_BATCH, HIST_ROWS, POOL_WIDTH), F32),
            jax.ShapeDtypeStruct((T_SAMPLE, GMLP_WIDTH), F32),
            jax.ShapeDtypeStruct((1, LANES), F32),
        ],
        scratch_shapes=[
            pltpu.VMEM((DEC_BATCH, HIST_ROWS + DEC_SEQ, POOL_WIDTH), F32),
        ] + route_scratch,
        compiler_params=pltpu.CompilerParams(
            dimension_semantics=("arbitrary",), vmem_limit_bytes=VMEM_LIMIT),
        name="mixer_sample",
    )(x_sample.reshape(T_SAMPLE, D_MODEL), hist, cnt_p, gmix, win, wpool, pscale, gv, ws, bsp_sample,
      wupp, wupg, wout, gffn, wrt, brt)

    counts = cnt[0, :N_BUCKETS].astype(I32)
    nblk = (counts + BM - 1) // BM
    bend = jnp.cumsum(nblk)
    bstart = bend - nblk
    nvalid = bend[-1:]
    step = jnp.minimum(jnp.arange(NB_MAX, dtype=I32), nvalid - 1)
    bucket_ids = jnp.arange(N_BUCKETS, dtype=I32)
    blk_bucket = jnp.minimum(jnp.sum(step[:, None] >= bend[None, :], axis=1), N_BUCKETS - 1)
    expert_lo = jnp.array([g * EXPERTS_PER_GROUP + p for g in range(N_GROUPS) for p in PAIR_LO], I32)
    expert_hi = jnp.array([g * EXPERTS_PER_GROUP + p for g in range(N_GROUPS) for p in PAIR_HI], I32)
    in_bucket = blk_bucket[:, None] == bucket_ids[None, :]
    blk_lo = jnp.sum(jnp.where(in_bucket, expert_lo[None, :], 0), axis=1).astype(I32)
    blk_hi = jnp.sum(jnp.where(in_bucket, expert_hi[None, :], 0), axis=1).astype(I32)
    blk_first = jnp.sum(jnp.where(in_bucket, bstart[None, :], 0), axis=1)
    blk_count = jnp.sum(jnp.where(in_bucket, counts[None, :], 0), axis=1)
    every_step = jnp.arange(NB_MAX, dtype=I32)
    blk_used = jnp.where(every_step < nvalid, jnp.clip(blk_count - (every_step - blk_first) * BM, 0, BM),
                         0).astype(I32)
    pad_start = bstart * BM + counts
    pad_len = nblk * BM - counts
    rec = jnp.concatenate([rec_p, rec_s], axis=0)
    first_slot = jnp.sum(jnp.where(rec[:, REC_BUCKET, :, None] == bucket_ids[None, None, :],
                                   (bstart * BM)[None, None, :], 0), axis=-1)
    slot_of = first_slot + rec[:, REC_RANK]
    d_prompt = slot_of[:NT_PROMPT].reshape(ND_PROMPT, 1, TD)
    d_sample = slot_of[NT_PROMPT:].reshape(1, 1, TL)
    d_gather = jnp.concatenate([d_prompt, jnp.pad(d_sample, ((0, 0), (0, 0), (0, TD - TL)))], axis=0)
    h_moves = h_p.reshape(ND_PROMPT, TD, D_MODEL)
    last_p = ND_PROMPT - 1

    def hp_map(i, *_):
        return (jnp.minimum(i, last_p), 0, 0)

    def smem_tile(rows, index_map):
        return pl.BlockSpec((None, 1, rows), index_map, memory_space=pltpu.SMEM)

    xb = pl.pallas_call(
        _dispatch_kernel,
        grid_spec=pltpu.PrefetchScalarGridSpec(
            num_scalar_prefetch=3,
            grid=(ND_PROMPT + 1,),
            in_specs=[
                smem_tile(TD, hp_map),
                smem_tile(TL, lambda i, *_: (0, 0, 0)),
                pl.BlockSpec((None, TD, D_MODEL), hp_map),
                pl.BlockSpec((TL, D_MODEL), lambda i, *_: (0, 0)),
                pl.BlockSpec((1, D_MODEL), lambda i, *_: (0, 0)),
            ],
            out_specs=any_spec,
            scratch_shapes=[
                pltpu.VMEM((2, TD, SUBLANES, LANES), F32),
                pltpu.VMEM((ZERO_ROWS, SUBLANES, LANES), F32),
                pltpu.SemaphoreType.DMA((2,)),
            ],
        ),
        out_shape=jax.ShapeDtypeStruct((N_SLOTS, SUBLANES, LANES), F32),
        compiler_params=pltpu.CompilerParams(
            dimension_semantics=("arbitrary",), vmem_limit_bytes=VMEM_LIMIT),
        name="dispatch",
    )(pad_start, pad_len, nvalid, d_prompt, d_sample, h_moves, h_s, gffn)

    def xb_map(i, lo, hi, used, nv):
        return (jnp.minimum(i, nv[0] - 1), 0, 0)

    def lo_map(i, lo, hi, used, nv):
        return (lo[i], 0, 0)

    def hi_map(i, lo, hi, used, nv):
        return (hi[i], 0, 0)

    in_proj = (None, D_MODEL, D_EXPERT)
    out_proj = (None, D_EXPERT, D_MODEL)
    yb = pl.pallas_call(
        _experts_kernel,
        grid_spec=pltpu.PrefetchScalarGridSpec(
            num_scalar_prefetch=4,
            grid=(NB_MAX,),
            in_specs=[
                pl.BlockSpec((BM, SUBLANES, LANES), xb_map),
                pl.BlockSpec(in_proj, lo_map), pl.BlockSpec(in_proj, lo_map), pl.BlockSpec(out_proj, lo_map),
                pl.BlockSpec(in_proj, hi_map), pl.BlockSpec(in_proj, hi_map), pl.BlockSpec(out_proj, hi_map),
            ],
            out_specs=pl.BlockSpec((BM, SUBLANES, LANES), lambda i, lo, hi, used, nv: (i, 0, 0)),
            scratch_shapes=2 * [
                pltpu.VMEM((D_MODEL, D_EXPERT), BF16),
                pltpu.VMEM((D_MODEL, D_EXPERT), BF16),
                pltpu.VMEM((D_EXPERT, D_MODEL), BF16),
            ],
        ),
        out_shape=jax.ShapeDtypeStruct((N_SLOTS, SUBLANES, LANES), I32),
        compiler_params=pltpu.CompilerParams(
            dimension_semantics=("arbitrary",), vmem_limit_bytes=VMEM_LIMIT),
        name="experts",
    )(blk_lo, blk_hi, blk_used, nvalid, xb, wg, wu, wd, wg, wu, wd)

    def next_tile(i):
        return (jnp.minimum(i + 1, ND_PROMPT), 0, 0)

    y_p, y_s = pl.pallas_call(
        _combine_kernel,
        grid=(ND_PROMPT + 1,),
        in_specs=[
            smem_tile(TD, lambda i: (i, 0, 0)),
            smem_tile(TD, next_tile),
            pl.BlockSpec((TD, LANES), lambda i: (jnp.minimum(i, last_p), 0)),
            pl.BlockSpec((TL, LANES), lambda i: (0, 0)),
            pl.BlockSpec((None, TD, D_MODEL), hp_map),
            pl.BlockSpec((TL, D_MODEL), lambda i: (0, 0)),
            pl.BlockSpec((1, D_MODEL), lambda i: (0, 0)),
            any_spec,
        ],
        out_specs=[
            pl.BlockSpec((None, TD, D_MODEL), hp_map),
            pl.BlockSpec((TL, D_MODEL), lambda i: (0, 0)),
        ],
        out_shape=[
            jax.ShapeDtypeStruct((ND_PROMPT, TD, D_MODEL), F32),
            jax.ShapeDtypeStruct((T_SAMPLE, D_MODEL), F32),
        ],
        scratch_shapes=[
            pltpu.VMEM((2, TD, SUBLANES, LANES), I32),
            pltpu.SemaphoreType.DMA((2,)),
        ],
        compiler_params=pltpu.CompilerParams(
            dimension_semantics=("arbitrary",), vmem_limit_bytes=VMEM_LIMIT),
        name="combine",
    )(d_gather, d_gather, pay_p, pay_s, h_moves, h_s, gfin, yb)

    y_sample = y_s.reshape(DEC_BATCH, DEC_SEQ, D_MODEL)
    new_pool_prompt = pool_p[None, :, 1:, :]
    new_pool_sample = pool_s[None, :, 1:, :]
    new_gmlp_v_sample = vn_s.reshape(1, DEC_BATCH, DEC_SEQ, GMLP_WIDTH)
    y_prompt = y_p.reshape(BATCH, SEQ, D_MODEL)
    return (y_prompt, y_sample, new_pool_prompt, new_pool_sample, new_gmlp_v_sample)
```

```python
import jax
import jax.numpy as jnp
from jax import lax
from jax.experimental import pallas as pl
from jax.experimental.pallas import tpu as pltpu

F32 = jnp.float32
BF16 = jnp.bfloat16
I32 = jnp.int32

D_MODEL = 1024
SEQ = 16384
BATCH = 2
DEC_BATCH = 16
DEC_SEQ = 32
PAST_LEN = 1024
POOL_WIDTH = 512
POOL_GROUP_DIM = 128
POOL_WINDOWS = (2, 4, 8, 16)
HIST_ROWS = 16
GMLP_WIDTH = 512
GMLP_HEADS = 4
GMLP_HEAD_DIM = 128
GMLP_CHUNK = 128
IN_COLS = POOL_WIDTH + 2 * GMLP_WIDTH + 2 * D_MODEL
N_GROUPS = 4
EXPERTS_PER_GROUP = 4
N_EXPERTS = 16
D_EXPERT = 512
EPS = 1e-6

LANES = 128
SUBLANES = 8
assert D_MODEL == SUBLANES * LANES
TL = 512
BM = 512
T_PROMPT = BATCH * SEQ
T_SAMPLE = DEC_BATCH * DEC_SEQ
T_TOTAL = T_PROMPT + T_SAMPLE
NT_PROMPT = T_PROMPT // TL
TILES_PER_SEQ = SEQ // TL
NT_TOTAL = T_TOTAL // TL
assert T_SAMPLE == TL
PAIR_LO = (0, 0, 0, 1, 1, 2)
PAIR_HI = (1, 2, 3, 2, 3, 3)
N_PAIRS = len(PAIR_LO)
N_BUCKETS = N_GROUPS * N_PAIRS
NB_MIN = T_TOTAL // BM
NB_MAX = (T_TOTAL + N_BUCKETS * (BM - 1)) // BM
N_SLOTS = NB_MAX * BM
ZERO_ROWS = BM // 2
TD = 2 * TL
ND_PROMPT = T_PROMPT // TD
DMA_UNROLL = 8
DMA_CHUNK = 64
VMEM_LIMIT = 56 * 1024 * 1024

REC_BUCKET, REC_RANK = 0, 1
REC_ROWS = 8
PAY_LO, PAY_HI = 0, 1


def _dot(a, b):
    return jnp.dot(a, b, preferred_element_type=F32)


def _rms(x, g):
    return x * lax.rsqrt(jnp.mean(x * x, axis=-1, keepdims=True) + EPS) * g


def _split_bf16(x):
    hi = x.astype(BF16)
    lo = (x - hi.astype(F32)).astype(BF16)
    return hi, lo


def _pack_bf16_pair(lo, hi):
    lo_bits = lax.bitcast_convert_type(lo.astype(BF16).astype(F32), I32)
    hi_bits = lax.bitcast_convert_type(hi.astype(BF16).astype(F32), I32)
    return lax.shift_right_logical(lo_bits, jnp.int32(16)) | hi_bits


def _unpack_bf16_pair(packed):
    lo = lax.bitcast_convert_type(lax.shift_left(packed, jnp.int32(16)), F32)
    hi = lax.bitcast_convert_type(packed & jnp.int32(-65536), F32)
    return lo, hi


def _row_copy(src_ref, s, dst_ref, d, sem):
    return pltpu.make_async_copy(src_ref.at[pl.ds(s, 1)], dst_ref.at[pl.ds(d, 1)], sem)


def _pool_windows(load_shifted, a, pos, wpool_ref, pscale_ref):
    outs = []
    for g, w in enumerate(POOL_WINDOWS):
        cols = slice(g * POOL_GROUP_DIM, (g + 1) * POOL_GROUP_DIM)
        acc = a[:, cols]
        for k in range(1, w):
            acc = acc + load_shifted(k, g)
        cnt = jnp.minimum(pos + 1, w).astype(F32)
        d = (acc / cnt - a[:, cols]).astype(BF16)
        outs.append(_dot(d, wpool_ref[g]))
    return jnp.concatenate(outs, axis=-1) * pscale_ref[...]


def _merge(pa, pb, ga, gb, wupp_ref, wupg_ref):
    return (jax.nn.sigmoid(ga) * _dot(pa.astype(BF16), wupp_ref[...])
            + jax.nn.sigmoid(gb) * _dot(pb.astype(BF16), wupg_ref[...]))


def _init_ltri(ltri_ref):
    r = lax.broadcasted_iota(I32, (TL, TL), 0)
    c = lax.broadcasted_iota(I32, (TL, TL), 1)
    ltri_ref[...] = jnp.where(c < r, 1.0, 0.0).astype(BF16)


def _router_logits(hn, wrt_ref, brt_ref):
    hi, lo = _split_bf16(hn)
    whi, wlo = _split_bf16(wrt_ref[...])
    hi_w = _dot(hi, jnp.concatenate([whi, wlo], axis=1))
    return hi_w[:, 0:LANES] + hi_w[:, LANES:2 * LANES] + _dot(lo, whi) + brt_ref[...]


def _assign(lg, ltri_ref, run_ref, rec_out_ref, pay_out_ref, counted=True):
    lane = lax.broadcasted_iota(I32, lg.shape, 1)
    neg = jnp.float32(-jnp.inf)
    big = jnp.int32(1 << 20)

    def first_argmax(v, vmax):
        return jnp.min(jnp.where(v == vmax, lane, big), axis=-1, keepdims=True)

    gmask = lane < N_GROUPS
    glm = jnp.where(gmask, lg, neg)
    gmax = jnp.max(glm, axis=-1, keepdims=True)
    grp = first_argmax(glm, gmax)
    p_g = 1.0 / jnp.sum(jnp.where(gmask, jnp.exp(lg - gmax), 0.0), axis=-1, keepdims=True)

    lo_lane = N_GROUPS + EXPERTS_PER_GROUP * grp
    elm = jnp.where((lane >= lo_lane) & (lane < lo_lane + EXPERTS_PER_GROUP), lg, neg)
    v1 = jnp.max(elm, axis=-1, keepdims=True)
    i1 = first_argmax(elm, v1)
    elm2 = jnp.where(lane == i1, neg, elm)
    v2 = jnp.max(elm2, axis=-1, keepdims=True)
    i2 = first_argmax(elm2, v2)
    ex2 = jnp.exp(v2 - v1)
    inv = 1.0 / (1.0 + ex2)
    w_top = inv * p_g
    w_second = ex2 * inv * p_g

    loc1 = i1 - lo_lane
    loc2 = i2 - lo_lane
    top_is_lo = loc1 < loc2
    e_lo = jnp.minimum(loc1, loc2)
    e_hi = jnp.maximum(loc1, loc2)
    pair = jnp.where(e_lo == 0, 0, jnp.where(e_lo == 1, 3, 5)) + e_hi - e_lo - 1
    bucket = grp * N_PAIRS + pair
    w_lo = jnp.where(top_is_lo, w_top, w_second)
    w_hi = jnp.where(top_is_lo, w_second, w_top)

    sel = lane == bucket
    onehot = jnp.where(sel, 1.0, 0.0)
    before = _dot(ltri_ref[...], onehot.astype(BF16)) + run_ref[...]
    rank = jnp.sum(jnp.where(sel, before, 0.0), axis=-1, keepdims=True)
    run_ref[...] = run_ref[...] + jnp.where(counted, jnp.sum(onehot, axis=0, keepdims=True), 0.0)

    rec = jnp.where(lane == REC_BUCKET, bucket.astype(F32), jnp.where(lane == REC_RANK, rank, 0.0))
    rec_out_ref[...] = rec.T[0:REC_ROWS, :].astype(I32)
    pay_out_ref[...] = jnp.where(lane == PAY_LO, w_lo, jnp.where(lane == PAY_HI, w_hi, 0.0))


def _mixer_prompt_kernel(x_ref, xprev_ref, gmix_ref, win_ref, wpool_ref, pscale_ref, gv_ref, ws_ref,
                         bsp_ref, wupp_ref, wupg_ref, wout_ref, gffn_ref, wrt_ref, brt_ref,
                         h_ref, rec_out_ref, pay_out_ref, pool_ref, cnt_ref,
                         aext_ref, s_ref, ltri_ref, run_ref, merged_ref):
    s = pl.program_id(0)

    @pl.when(s == 0)
    def _():
        _init_ltri(ltri_ref)
        run_ref[...] = jnp.zeros_like(run_ref)
        aext_ref[TL:TL + HIST_ROWS, :] = jnp.zeros((HIST_ROWS, POOL_WIDTH), F32)
        merged_ref[...] = jnp.zeros_like(merged_ref)

    j = jnp.minimum(s, NT_PROMPT - 1) % TILES_PER_SEQ

    h = xprev_ref[...] + _dot(merged_ref[...], wout_ref[...])
    xn = _rms(x_ref[...], gmix_ref[...]).astype(BF16)
    a = _dot(xn, win_ref[:, 0:512])
    h_ref[...] = h
    hn = _rms(h, gffn_ref[...])
    lg = _router_logits(hn, wrt_ref, brt_ref)

    aext_ref[0:HIST_ROWS, :] = jnp.where(j > 0, aext_ref[TL:TL + HIST_ROWS, :], 0.0)
    aext_ref[HIST_ROWS:HIST_ROWS + TL, :] = a
    pool_ref[...] = a[TL - HIST_ROWS:TL, :]
    pos = j * TL + lax.broadcasted_iota(I32, (TL, 1), 0)

    def load_shifted(k, g):
        return aext_ref[HIST_ROWS - k:HIST_ROWS - k + TL, g * POOL_GROUP_DIM:(g + 1) * POOL_GROUP_DIM]

    v = _dot(xn, win_ref[:, 1024:1536])
    pa = _pool_windows(load_shifted, a, pos, wpool_ref, pscale_ref)
    u = _dot(xn, win_ref[:, 512:1024])
    _assign(lg, ltri_ref, run_ref, rec_out_ref, pay_out_ref, counted=s > 0)
    cnt_ref[...] = run_ref[...]
    ga = _dot(xn, win_ref[:, 1536:2560])

    vn = _rms(v, gv_ref[...]).astype(BF16)
    rr = lax.broadcasted_iota(I32, (GMLP_CHUNK, GMLP_CHUNK), 0)
    cc = lax.broadcasted_iota(I32, (GMLP_CHUNK, GMLP_CHUNK), 1)
    for hd in range(GMLP_HEADS):
        cols = slice(hd * GMLP_HEAD_DIM, (hd + 1) * GMLP_HEAD_DIM)
        wsm = jnp.where(cc <= rr, ws_ref[hd], 0.0).astype(BF16)
        chunks = [vn[c * GMLP_CHUNK:(c + 1) * GMLP_CHUNK, cols] for c in range(TL // GMLP_CHUNK)]
        mixed = _dot(wsm, jnp.concatenate(chunks, axis=1))
        for c in range(TL // GMLP_CHUNK):
            rows = slice(c * GMLP_CHUNK, (c + 1) * GMLP_CHUNK)
            s_ref[rows, cols] = (mixed[:, c * GMLP_HEAD_DIM:(c + 1) * GMLP_HEAD_DIM]
                                 + bsp_ref[rows, hd:hd + 1])
    pb = u * s_ref[...]
    gb = _dot(xn, win_ref[:, 2560:3584])

    merged_ref[...] = _merge(pa, pb, ga, gb, wupp_ref, wupg_ref).astype(BF16)


def _mixer_sample_kernel(x_ref, hist_ref, run0_ref, gmix_ref, win_ref, wpool_ref, pscale_ref, gv_ref,
                         ws_ref, bsp_ref, wupp_ref, wupg_ref, wout_ref, gffn_ref, wrt_ref, brt_ref,
                         h_ref, rec_out_ref, pay_out_ref, pool_ref, vn_ref, cnt_ref,
                         aext_ref, ltri_ref, run_ref):
    _init_ltri(ltri_ref)
    run_ref[...] = run0_ref[...]

    x = x_ref[...]
    xn = _rms(x, gmix_ref[...]).astype(BF16)
    a = _dot(xn, win_ref[:, 0:512])
    u = _dot(xn, win_ref[:, 512:1024])
    v = _dot(xn, win_ref[:, 1024:1536])
    ga = _dot(xn, win_ref[:, 1536:2560])
    gb = _dot(xn, win_ref[:, 2560:3584])

    a3 = a.reshape(DEC_BATCH, DEC_SEQ, POOL_WIDTH)
    aext_ref[:, 0:HIST_ROWS, :] = hist_ref[...]
    aext_ref[:, HIST_ROWS:HIST_ROWS + DEC_SEQ, :] = a3
    pool_ref[...] = a3[:, DEC_SEQ - HIST_ROWS:DEC_SEQ, :]
    row = lax.broadcasted_iota(I32, (TL, 1), 0)
    pos = PAST_LEN + row % DEC_SEQ

    def load_shifted(k, g):
        sl = aext_ref[:, HIST_ROWS - k:HIST_ROWS - k + DEC_SEQ, g * POOL_GROUP_DIM:(g + 1) * POOL_GROUP_DIM]
        return sl.reshape(TL, POOL_GROUP_DIM)

    pa = _pool_windows(load_shifted, a, pos, wpool_ref, pscale_ref)

    vnf = _rms(v, gv_ref[...])
    vn_ref[...] = vnf
    vn = vnf.astype(BF16)
    rsel = (lax.broadcasted_iota(I32, (TL, GMLP_CHUNK), 1)
            == lax.broadcasted_iota(I32, (TL, GMLP_CHUNK), 0) % DEC_SEQ)
    rsel_b = jnp.where(rsel, 1.0, 0.0).astype(BF16)
    csel = (lax.broadcasted_iota(I32, (GMLP_CHUNK, TL), 0)
            == lax.broadcasted_iota(I32, (GMLP_CHUNK, TL), 1) % DEC_SEQ)
    csel_b = jnp.where(csel, 1.0, 0.0).astype(BF16)
    rr = lax.broadcasted_iota(I32, (TL, TL), 0)
    cc = lax.broadcasted_iota(I32, (TL, TL), 1)
    keep = (rr // DEC_SEQ == cc // DEC_SEQ) & (cc <= rr)
    s_parts = []
    for hd in range(GMLP_HEADS):
        cols = slice(hd * GMLP_HEAD_DIM, (hd + 1) * GMLP_HEAD_DIM)
        wrow = _dot(rsel_b, ws_ref[hd].astype(BF16)).astype(BF16)
        wfull = _dot(wrow, csel_b)
        wblk = jnp.where(keep, wfull, 0.0).astype(BF16)
        s_parts.append(_dot(wblk, vn[:, cols]) + bsp_ref[:, hd:hd + 1])
    pb = u * jnp.concatenate(s_parts, axis=-1)

    h = x + _dot(_merge(pa, pb, ga, gb, wupp_ref, wupg_ref).astype(BF16), wout_ref[...])
    h_ref[...] = h
    lg = _router_logits(_rms(h, gffn_ref[...]), wrt_ref, brt_ref)
    _assign(lg, ltri_ref, run_ref, rec_out_ref, pay_out_ref)
    cnt_ref[...] = run_ref[...]


def _dispatch_kernel(pad_start_ref, pad_len_ref, nvalid_ref,
                     dp_ref, ds_ref, hp_ref, hs_ref, gffn_ref, xb_ref, row_ref, zero_ref, sems):
    i = pl.program_id(0)
    slot = i % 2
    rows = row_ref.at[slot]

    def fill_and_scatter(h_ref, d_ref, n_rows):
        for c in range(n_rows // DMA_CHUNK):
            r0 = c * DMA_CHUNK
            hn = _rms(h_ref[r0:r0 + DMA_CHUNK, :], gffn_ref[...])
            rows[r0:r0 + DMA_CHUNK] = hn.reshape(DMA_CHUNK, SUBLANES, LANES)
            for t in range(r0, r0 + DMA_CHUNK):
                _row_copy(rows, t, xb_ref, d_ref[0, t], sems.at[slot]).start(priority=t % 2)

    @pl.when(i < ND_PROMPT)
    def _():
        fill_and_scatter(hp_ref, dp_ref, TD)

    @pl.when(i == ND_PROMPT)
    def _():
        fill_and_scatter(hs_ref, ds_ref, TL)

    def drain(s, n_rows):
        def body(t, c):
            _row_copy(row_ref.at[s], t, xb_ref, 0, sems.at[s]).wait()
            return c
        lax.fori_loop(0, n_rows, body, 0, unroll=DMA_UNROLL)

    @pl.when(i > 0)
    def _():
        drain(1 - slot, TD)

    @pl.when(i == ND_PROMPT)
    def _():
        drain(slot, TL)
        zero_ref[...] = jnp.zeros_like(zero_ref)
        bits = [1 << k for k in reversed(range(BM.bit_length() - 1))]

        def pad_copies(fn):
            for e in range(N_BUCKETS):
                off = pad_start_ref[e]
                n = pad_len_ref[e]
                for bit in bits:
                    take = jnp.bitwise_and(n, bit)

                    @pl.when(take != 0)
                    def _(off=off, bit=bit):
                        fn(pltpu.make_async_copy(zero_ref.at[pl.ds(0, bit)], xb_ref.at[pl.ds(off, bit)],
                                                 sems.at[slot]))
                    off = off + take
            for blk in range(NB_MIN, NB_MAX):
                @pl.when(blk >= nvalid_ref[0])
                def _(blk=blk):
                    for half in range(BM // ZERO_ROWS):
                        fn(pltpu.make_async_copy(
                            zero_ref, xb_ref.at[pl.ds(blk * BM + half * ZERO_ROWS, ZERO_ROWS)], sems.at[slot]))

        pad_copies(lambda cp: cp.start())
        pad_copies(lambda cp: cp.wait())


def _experts_kernel(blk_lo_ref, blk_hi_ref, nvalid_ref, xb_ref,
                    wg_lo_ref, wu_lo_ref, wd_lo_ref, wg_hi_ref, wu_hi_ref, wd_hi_ref,
                    yb_ref, wgb_lo_ref, wub_lo_ref, wdb_lo_ref, wgb_hi_ref, wub_hi_ref, wdb_hi_ref):
    i = pl.program_id(0)
    prev = jnp.maximum(i - 1, 0)

    @pl.when((i == 0) | (blk_lo_ref[i] != blk_lo_ref[prev]))
    def _():
        wgb_lo_ref[...] = wg_lo_ref[...].astype(BF16)
        wub_lo_ref[...] = wu_lo_ref[...].astype(BF16)
        wdb_lo_ref[...] = wd_lo_ref[...].astype(BF16)

    @pl.when((i == 0) | (blk_hi_ref[i] != blk_hi_ref[prev]))
    def _():
        wgb_hi_ref[...] = wg_hi_ref[...].astype(BF16)
        wub_hi_ref[...] = wu_hi_ref[...].astype(BF16)
        wdb_hi_ref[...] = wd_hi_ref[...].astype(BF16)

    @pl.when(i < nvalid_ref[0])
    def _():
        x = xb_ref[...].reshape(BM, D_MODEL).astype(BF16)

        def hidden(wgb, wub):
            hg = _dot(x, wgb[...])
            return ((hg * jax.nn.sigmoid(hg)) * _dot(x, wub[...])).astype(BF16)

        act_lo = hidden(wgb_lo_ref, wub_lo_ref)
        act_hi = hidden(wgb_hi_ref, wub_hi_ref)
        y_lo = _dot(act_lo, wdb_lo_ref[...])
        y_hi = _dot(act_hi, wdb_hi_ref[...])
        yb_ref[...] = _pack_bf16_pair(y_lo, y_hi).reshape(BM, SUBLANES, LANES)

    @pl.when(i >= nvalid_ref[0])
    def _():
        yb_ref[...] = jnp.zeros_like(yb_ref)


def _combine_kernel(d_ref, dn_ref, payp_ref, pays_ref, hp_ref, hs_ref, gfin_ref, yb_ref,
                    yp_ref, ys_ref, buf_ref, sems):
    i = pl.program_id(0)
    slot = i % 2

    @pl.when(i == 0)
    def _():
        def body(t, c):
            _row_copy(yb_ref, d_ref[0, t], buf_ref.at[slot], t, sems.at[slot]).start()
            return c
        lax.fori_loop(0, TL, body, 0, unroll=DMA_UNROLL)

    def drain(t, c):
        _row_copy(yb_ref, 0, buf_ref.at[slot], t, sems.at[slot]).wait()
        return c

    lax.fori_loop(0, TL, drain, 0, unroll=DMA_UNROLL)

    def finish_chunk(h_ref, pay_ref, out_ref, r0):
        rows = slice(r0, r0 + DMA_CHUNK)
        y_lo, y_hi = _unpack_bf16_pair(buf_ref.at[slot][rows].reshape(DMA_CHUNK, D_MODEL))
        pay = pay_ref[rows, :]
        moe = y_lo * pay[:, PAY_LO:PAY_LO + 1] + y_hi * pay[:, PAY_HI:PAY_HI + 1]
        out_ref[rows, :] = _rms(h_ref[rows, :] + moe, gfin_ref[...])

    @pl.when(i < NT_PROMPT)
    def _():
        for c in range(TL // DMA_CHUNK):
            r0 = c * DMA_CHUNK
            finish_chunk(hp_ref, payp_ref, yp_ref, r0)
            for t in range(r0, r0 + DMA_CHUNK):
                _row_copy(yb_ref, dn_ref[0, t], buf_ref.at[1 - slot], t, sems.at[1 - slot]).start(priority=t % 2)

    @pl.when(i == NT_PROMPT)
    def _():
        for c in range(TL // DMA_CHUNK):
            finish_chunk(hs_ref, pays_ref, ys_ref, c * DMA_CHUNK)


def _const_spec(shape):
    zeros = (0,) * len(shape)
    return pl.BlockSpec(shape, lambda *_: zeros, pipeline_mode=pl.Buffered(1))


def kernel(x_prompt, x_sample, state_pool, g_mix, w_in, w_pool, pool_scale, g_v, w_spatial, b_spatial,
           w_up_pool, w_up_gmlp, w_out, g_ffn, w_group, b_group, w_router, b_router, w_gate, w_up,
           w_down, g_final):
    gmix = g_mix[0][None, :]
    win = w_in[0].astype(BF16)
    wpool = w_pool[0].astype(BF16)
    pscale = pool_scale[0][None, :]
    gv = g_v[0][None, :]
    ws = w_spatial[0]
    bsp_t = b_spatial[0].T
    wupp = w_up_pool[0].astype(BF16)
    wupg = w_up_gmlp[0].astype(BF16)
    wout = w_out[0].astype(BF16)
    gffn = g_ffn[0][None, :]
    n_route = N_GROUPS + N_EXPERTS
    wrt = jnp.pad(jnp.concatenate([w_group[0], w_router[0]], axis=1), ((0, 0), (0, LANES - n_route)))
    brt = jnp.pad(jnp.concatenate([b_group[0], b_router[0]]), (0, LANES - n_route))[None, :]
    wg = w_gate[0]
    wu = w_up[0]
    wd = w_down[0]
    gfin = g_final[None, :]
    bsp_prompt = jnp.tile(bsp_t, (TL // GMLP_CHUNK, 1))
    bsp_sample = jnp.tile(bsp_t[:DEC_SEQ], (DEC_BATCH, 1))
    hist = jnp.pad(state_pool[0], ((0, 0), (1, 0), (0, 0)))

    weight_specs = [
        _const_spec((1, D_MODEL)),
        _const_spec((D_MODEL, IN_COLS)),
        _const_spec((4, POOL_GROUP_DIM, POOL_GROUP_DIM)),
        _const_spec((1, POOL_WIDTH)),
        _const_spec((1, GMLP_WIDTH)),
        _const_spec((GMLP_HEADS, GMLP_CHUNK, GMLP_CHUNK)),
        _const_spec((TL, GMLP_HEADS)),
        _const_spec((POOL_WIDTH, D_MODEL)),
        _const_spec((GMLP_WIDTH, D_MODEL)),
        _const_spec((D_MODEL, D_MODEL)),
        _const_spec((1, D_MODEL)),
        _const_spec((D_MODEL, LANES)),
        _const_spec((1, LANES)),
    ]
    any_spec = pl.BlockSpec(memory_space=pl.ANY)
    route_scratch = [
        pltpu.VMEM((TL, TL), BF16),
        pltpu.VMEM((1, LANES), F32),
    ]

    def started(s):
        return jnp.minimum(s, NT_PROMPT - 1)

    def finished(s):
        return jnp.maximum(s - 1, 0)

    x_tiles = x_prompt.reshape(NT_PROMPT, TL, D_MODEL)
    h_p, rec_p, pay_p, pool_p, cnt_p = pl.pallas_call(
        _mixer_prompt_kernel,
        grid=(NT_PROMPT + 1,),
        in_specs=[
            pl.BlockSpec((None, TL, D_MODEL), lambda s: (started(s), 0, 0)),
            pl.BlockSpec((None, TL, D_MODEL), lambda s: (finished(s), 0, 0)),
        ] + weight_specs,
        out_specs=[
            pl.BlockSpec((None, TL, D_MODEL), lambda s: (finished(s), 0, 0)),
            pl.BlockSpec((None, REC_ROWS, TL), lambda s: (finished(s), 0, 0)),
            pl.BlockSpec((TL, LANES), lambda s: (finished(s), 0)),
            pl.BlockSpec((None, HIST_ROWS, POOL_WIDTH), lambda s: (started(s) // TILES_PER_SEQ, 0, 0)),
            pl.BlockSpec((1, LANES), lambda s: (0, 0)),
        ],
        out_shape=[
            jax.ShapeDtypeStruct((NT_PROMPT, TL, D_MODEL), F32),
            jax.ShapeDtypeStruct((NT_PROMPT, REC_ROWS, TL), I32),
            jax.ShapeDtypeStruct((T_PROMPT, LANES), F32),
            jax.ShapeDtypeStruct((BATCH, HIST_ROWS, POOL_WIDTH), F32),
            jax.ShapeDtypeStruct((1, LANES), F32),
        ],
        scratch_shapes=[
            pltpu.VMEM((HIST_ROWS + TL, POOL_WIDTH), F32),
            pltpu.VMEM((TL, GMLP_WIDTH), F32),
        ] + route_scratch + [pltpu.VMEM((TL, D_MODEL), BF16)],
        compiler_params=pltpu.CompilerParams(
            dimension_semantics=("arbitrary",), vmem_limit_bytes=VMEM_LIMIT),
        name="mixer_prompt",
    )(x_tiles, x_tiles, gmix, win, wpool, pscale, gv, ws, bsp_prompt, wupp, wupg, wout, gffn, wrt, brt)

    h_s, rec_s, pay_s, pool_s, vn_s, cnt = pl.pallas_call(
        _mixer_sample_kernel,
        grid=(1,),
        in_specs=[
            _const_spec((TL, D_MODEL)),
            _const_spec((DEC_BATCH, HIST_ROWS, POOL_WIDTH)),
            _const_spec((1, LANES)),
        ] + weight_specs,
        out_specs=[
            _const_spec((TL, D_MODEL)),
            pl.BlockSpec((None, REC_ROWS, TL), lambda i: (0, 0, 0)),
            _const_spec((TL, LANES)),
            _const_spec((DEC_BATCH, HIST_ROWS, POOL_WIDTH)),
            _const_spec((TL, GMLP_WIDTH)),
            _const_spec((1, LANES)),
        ],
        out_shape=[
            jax.ShapeDtypeStruct((T_SAMPLE, D_MODEL), F32),
            jax.ShapeDtypeStruct((1, REC_ROWS, TL), I32),
            jax.ShapeDtypeStruct((T_SAMPLE, LANES), F32),
            jax.ShapeDtypeStruct((DEC_BATCH, HIST_ROWS, POOL_WIDTH), F32),
            jax.ShapeDtypeStruct((T_SAMPLE, GMLP_WIDTH), F32),
            jax.ShapeDtypeStruct((1, LANES), F32),
        ],
        scratch_shapes=[
            pltpu.VMEM((DEC_BATCH, HIST_ROWS + DEC_SEQ, POOL_WIDTH), F32),
        ] + route_scratch,
        compiler_params=pltpu.CompilerParams(
            dimension_semantics=("arbitrary",), vmem_limit_bytes=VMEM_LIMIT),
        name="mixer_sample",
    )(x_sample.reshape(T_SAMPLE, D_MODEL), hist, cnt_p, gmix, win, wpool, pscale, gv, ws, bsp_sample,
      wupp, wupg, wout, gffn, wrt, brt)

    counts = cnt[0, :N_BUCKETS].astype(I32)
    nblk = (counts + BM - 1) // BM
    bend = jnp.cumsum(nblk)
    bstart = bend - nblk
    nvalid = bend[-1:]
    step = jnp.minimum(jnp.arange(NB_MAX, dtype=I32), nvalid - 1)
    bucket_ids = jnp.arange(N_BUCKETS, dtype=I32)
    blk_bucket = jnp.minimum(jnp.sum(step[:, None] >= bend[None, :], axis=1), N_BUCKETS - 1)
    expert_lo = jnp.array([g * EXPERTS_PER_GROUP + p for g in range(N_GROUPS) for p in PAIR_LO], I32)
    expert_hi = jnp.array([g * EXPERTS_PER_GROUP + p for g in range(N_GROUPS) for p in PAIR_HI], I32)
    in_bucket = blk_bucket[:, None] == bucket_ids[None, :]
    blk_lo = jnp.sum(jnp.where(in_bucket, expert_lo[None, :], 0), axis=1).astype(I32)
    blk_hi = jnp.sum(jnp.where(in_bucket, expert_hi[None, :], 0), axis=1).astype(I32)
    pad_start = bstart * BM + counts
    pad_len = nblk * BM - counts
    rec = jnp.concatenate([rec_p, rec_s], axis=0)
    first_slot = jnp.sum(jnp.where(rec[:, REC_BUCKET, :, None] == bucket_ids[None, None, :],
                                   (bstart * BM)[None, None, :], 0), axis=-1)
    slot_of = first_slot + rec[:, REC_RANK]
    d_prompt = slot_of[:NT_PROMPT].reshape(ND_PROMPT, 1, TD)
    d_sample = slot_of[NT_PROMPT:].reshape(1, 1, TL)
    d_tiles = slot_of[:, None, :]
    h_moves = h_p.reshape(ND_PROMPT, TD, D_MODEL)

    def move_map(i, *_):
        return (jnp.minimum(i, ND_PROMPT - 1), 0, 0)

    def hp_map(i, *_):
        return (jnp.minimum(i, NT_PROMPT - 1), 0, 0)

    def smem_tile(rows, index_map):
        return pl.BlockSpec((None, 1, rows), index_map, memory_space=pltpu.SMEM)

    xb = pl.pallas_call(
        _dispatch_kernel,
        grid_spec=pltpu.PrefetchScalarGridSpec(
            num_scalar_prefetch=3,
            grid=(ND_PROMPT + 1,),
            in_specs=[
                smem_tile(TD, move_map),
                smem_tile(TL, lambda i, *_: (0, 0, 0)),
                pl.BlockSpec((None, TD, D_MODEL), move_map),
                pl.BlockSpec((TL, D_MODEL), lambda i, *_: (0, 0)),
                pl.BlockSpec((1, D_MODEL), lambda i, *_: (0, 0)),
            ],
            out_specs=any_spec,
            scratch_shapes=[
                pltpu.VMEM((2, TD, SUBLANES, LANES), F32),
                pltpu.VMEM((ZERO_ROWS, SUBLANES, LANES), F32),
                pltpu.SemaphoreType.DMA((2,)),
            ],
        ),
        out_shape=jax.ShapeDtypeStruct((N_SLOTS, SUBLANES, LANES), F32),
        compiler_params=pltpu.CompilerParams(
            dimension_semantics=("arbitrary",), vmem_limit_bytes=VMEM_LIMIT),
        name="dispatch",
    )(pad_start, pad_len, nvalid, d_prompt, d_sample, h_moves, h_s, gffn)

    def xb_map(i, lo, hi, nv):
        return (jnp.minimum(i, nv[0] - 1), 0, 0)

    def lo_map(i, lo, hi, nv):
        return (lo[i], 0, 0)

    def hi_map(i, lo, hi, nv):
        return (hi[i], 0, 0)

    in_proj = (None, D_MODEL, D_EXPERT)
    out_proj = (None, D_EXPERT, D_MODEL)
    yb = pl.pallas_call(
        _experts_kernel,
        grid_spec=pltpu.PrefetchScalarGridSpec(
            num_scalar_prefetch=3,
            grid=(NB_MAX,),
            in_specs=[
                pl.BlockSpec((BM, SUBLANES, LANES), xb_map),
                pl.BlockSpec(in_proj, lo_map), pl.BlockSpec(in_proj, lo_map), pl.BlockSpec(out_proj, lo_map),
                pl.BlockSpec(in_proj, hi_map), pl.BlockSpec(in_proj, hi_map), pl.BlockSpec(out_proj, hi_map),
            ],
            out_specs=pl.BlockSpec((BM, SUBLANES, LANES), lambda i, lo, hi, nv: (i, 0, 0)),
            scratch_shapes=2 * [
                pltpu.VMEM((D_MODEL, D_EXPERT), BF16),
                pltpu.VMEM((D_MODEL, D_EXPERT), BF16),
                pltpu.VMEM((D_EXPERT, D_MODEL), BF16),
            ],
        ),
        out_shape=jax.ShapeDtypeStruct((N_SLOTS, SUBLANES, LANES), I32),
        compiler_params=pltpu.CompilerParams(
            dimension_semantics=("arbitrary",), vmem_limit_bytes=VMEM_LIMIT),
        name="experts",
    )(blk_lo, blk_hi, nvalid, xb, wg, wu, wd, wg, wu, wd)

    def next_tile(i):
        return (jnp.minimum(i + 1, NT_TOTAL - 1), 0, 0)

    y_p, y_s = pl.pallas_call(
        _combine_kernel,
        grid=(NT_TOTAL,),
        in_specs=[
            smem_tile(TL, lambda i: (i, 0, 0)),
            smem_tile(TL, next_tile),
            pl.BlockSpec((TL, LANES), lambda i: (jnp.minimum(i, NT_PROMPT - 1), 0)),
            pl.BlockSpec((TL, LANES), lambda i: (0, 0)),
            pl.BlockSpec((None, TL, D_MODEL), hp_map),
            pl.BlockSpec((TL, D_MODEL), lambda i: (0, 0)),
            pl.BlockSpec((1, D_MODEL), lambda i: (0, 0)),
            any_spec,
        ],
        out_specs=[
            pl.BlockSpec((None, TL, D_MODEL), hp_map),
            pl.BlockSpec((TL, D_MODEL), lambda i: (0, 0)),
        ],
        out_shape=[
            jax.ShapeDtypeStruct((NT_PROMPT, TL, D_MODEL), F32),
            jax.ShapeDtypeStruct((T_SAMPLE, D_MODEL), F32),
        ],
        scratch_shapes=[
            pltpu.VMEM((2, TL, SUBLANES, LANES), I32),
            pltpu.SemaphoreType.DMA((2,)),
        ],
        compiler_params=pltpu.CompilerParams(
            dimension_semantics=("arbitrary",), vmem_limit_bytes=VMEM_LIMIT),
        name="combine",
    )(d_tiles, d_tiles, pay_p, pay_s, h_p, h_s, gfin, yb)

    y_sample = y_s.reshape(DEC_BATCH, DEC_SEQ, D_MODEL)
    new_pool_prompt = pool_p[None, :, 1:, :]
    new_pool_sample = pool_s[None, :, 1:, :]
    new_gmlp_v_sample = vn_s.reshape(1, DEC_BATCH, DEC_SEQ, GMLP_WIDTH)
    y_prompt = y_p.reshape(BATCH, SEQ, D_MODEL)
    return (y_prompt, y_sample, new_pool_prompt, new_pool_sample, new_gmlp_v_sample)
```

```python
import jax
import jax.numpy as jnp
from jax import lax
from jax.experimental import pallas as pl
from jax.experimental.pallas import tpu as pltpu

F32 = jnp.float32
BF16 = jnp.bfloat16
I32 = jnp.int32

D_MODEL = 1024
SEQ = 16384
BATCH = 2
DEC_BATCH = 16
DEC_SEQ = 32
PAST_LEN = 1024
POOL_WIDTH = 512
POOL_GROUP_DIM = 128
POOL_WINDOWS = (2, 4, 8, 16)
HIST_ROWS = 16
GMLP_WIDTH = 512
GMLP_HEADS = 4
GMLP_HEAD_DIM = 128
GMLP_CHUNK = 128
IN_COLS = POOL_WIDTH + 2 * GMLP_WIDTH + 2 * D_MODEL
N_GROUPS = 4
EXPERTS_PER_GROUP = 4
N_EXPERTS = 16
D_EXPERT = 512
EPS = 1e-6

LANES = 128
SUBLANES = 8
assert D_MODEL == SUBLANES * LANES
TL = 512
BM = 512
T_PROMPT = BATCH * SEQ
T_SAMPLE = DEC_BATCH * DEC_SEQ
T_TOTAL = T_PROMPT + T_SAMPLE
NT_PROMPT = T_PROMPT // TL
TILES_PER_SEQ = SEQ // TL
NT_TOTAL = T_TOTAL // TL
assert T_SAMPLE == TL
PAIR_LO = (0, 0, 0, 1, 1, 2)
PAIR_HI = (1, 2, 3, 2, 3, 3)
N_PAIRS = len(PAIR_LO)
N_BUCKETS = N_GROUPS * N_PAIRS
NB_MIN = T_TOTAL // BM
NB_MAX = (T_TOTAL + N_BUCKETS * (BM - 1)) // BM
N_SLOTS = NB_MAX * BM
ZERO_ROWS = BM // 2
TD = 2 * TL
ND_PROMPT = T_PROMPT // TD
DMA_UNROLL = 8
DMA_CHUNK = 64
SCATTER_CHUNK = 32
VMEM_LIMIT = 56 * 1024 * 1024

REC_BUCKET, REC_RANK = 0, 1
REC_ROWS = 8
PAY_LO, PAY_HI = 0, 1


def _dot(a, b):
    return jnp.dot(a, b, preferred_element_type=F32)


def _rms(x, g):
    return x * lax.rsqrt(jnp.mean(x * x, axis=-1, keepdims=True) + EPS) * g


def _split_bf16(x):
    hi = x.astype(BF16)
    lo = (x - hi.astype(F32)).astype(BF16)
    return hi, lo


def _pack_bf16_pair(lo, hi):
    lo_bits = lax.bitcast_convert_type(lo.astype(BF16).astype(F32), I32)
    hi_bits = lax.bitcast_convert_type(hi.astype(BF16).astype(F32), I32)
    return lax.shift_right_logical(lo_bits, jnp.int32(16)) | hi_bits


def _unpack_bf16_pair(packed):
    lo = lax.bitcast_convert_type(lax.shift_left(packed, jnp.int32(16)), F32)
    hi = lax.bitcast_convert_type(packed & jnp.int32(-65536), F32)
    return lo, hi


def _row_copy(src_ref, s, dst_ref, d, sem):
    return pltpu.make_async_copy(src_ref.at[pl.ds(s, 1)], dst_ref.at[pl.ds(d, 1)], sem)


def _pool_windows(load_shifted, a, pos, wpool_ref, pscale_ref):
    outs = []
    for g, w in enumerate(POOL_WINDOWS):
        cols = slice(g * POOL_GROUP_DIM, (g + 1) * POOL_GROUP_DIM)
        acc = a[:, cols]
        for k in range(1, w):
            acc = acc + load_shifted(k, g)
        cnt = jnp.minimum(pos + 1, w).astype(F32)
        d = (acc / cnt - a[:, cols]).astype(BF16)
        outs.append(_dot(d, wpool_ref[g]))
    return jnp.concatenate(outs, axis=-1) * pscale_ref[...]


def _merge(pa, pb, ga, gb, wupp_ref, wupg_ref):
    return (jax.nn.sigmoid(ga) * _dot(pa.astype(BF16), wupp_ref[...])
            + jax.nn.sigmoid(gb) * _dot(pb.astype(BF16), wupg_ref[...]))


def _init_ltri(ltri_ref):
    r = lax.broadcasted_iota(I32, (TL, TL), 0)
    c = lax.broadcasted_iota(I32, (TL, TL), 1)
    ltri_ref[...] = jnp.where(c < r, 1.0, 0.0).astype(BF16)


def _router_logits(hn, wrt_ref, brt_ref):
    hi, lo = _split_bf16(hn)
    whi, wlo = _split_bf16(wrt_ref[...])
    hi_w = _dot(hi, jnp.concatenate([whi, wlo], axis=1))
    return hi_w[:, 0:LANES] + hi_w[:, LANES:2 * LANES] + _dot(lo, whi) + brt_ref[...]


def _assign(lg, ltri_ref, run_ref, rec_out_ref, pay_out_ref, counted=True):
    lane = lax.broadcasted_iota(I32, lg.shape, 1)
    neg = jnp.float32(-jnp.inf)
    big = jnp.int32(1 << 20)

    def first_argmax(v, vmax):
        return jnp.min(jnp.where(v == vmax, lane, big), axis=-1, keepdims=True)

    gmask = lane < N_GROUPS
    glm = jnp.where(gmask, lg, neg)
    gmax = jnp.max(glm, axis=-1, keepdims=True)
    grp = first_argmax(glm, gmax)
    p_g = 1.0 / jnp.sum(jnp.where(gmask, jnp.exp(lg - gmax), 0.0), axis=-1, keepdims=True)

    lo_lane = N_GROUPS + EXPERTS_PER_GROUP * grp
    elm = jnp.where((lane >= lo_lane) & (lane < lo_lane + EXPERTS_PER_GROUP), lg, neg)
    v1 = jnp.max(elm, axis=-1, keepdims=True)
    i1 = first_argmax(elm, v1)
    elm2 = jnp.where(lane == i1, neg, elm)
    v2 = jnp.max(elm2, axis=-1, keepdims=True)
    i2 = first_argmax(elm2, v2)
    ex2 = jnp.exp(v2 - v1)
    inv = 1.0 / (1.0 + ex2)
    w_top = inv * p_g
    w_second = ex2 * inv * p_g

    loc1 = i1 - lo_lane
    loc2 = i2 - lo_lane
    top_is_lo = loc1 < loc2
    e_lo = jnp.minimum(loc1, loc2)
    e_hi = jnp.maximum(loc1, loc2)
    pair = jnp.where(e_lo == 0, 0, jnp.where(e_lo == 1, 3, 5)) + e_hi - e_lo - 1
    bucket = grp * N_PAIRS + pair
    w_lo = jnp.where(top_is_lo, w_top, w_second)
    w_hi = jnp.where(top_is_lo, w_second, w_top)

    sel = lane == bucket
    onehot = jnp.where(sel, 1.0, 0.0)
    before = _dot(ltri_ref[...], onehot.astype(BF16)) + run_ref[...]
    rank = jnp.sum(jnp.where(sel, before, 0.0), axis=-1, keepdims=True)
    run_ref[...] = run_ref[...] + jnp.where(counted, jnp.sum(onehot, axis=0, keepdims=True), 0.0)

    rec = jnp.where(lane == REC_BUCKET, bucket.astype(F32), jnp.where(lane == REC_RANK, rank, 0.0))
    rec_out_ref[...] = rec.T[0:REC_ROWS, :].astype(I32)
    pay_out_ref[...] = jnp.where(lane == PAY_LO, w_lo, jnp.where(lane == PAY_HI, w_hi, 0.0))


def _mixer_prompt_kernel(x_ref, xprev_ref, gmix_ref, win_ref, wpool_ref, pscale_ref, gv_ref, ws_ref,
                         bsp_ref, wupp_ref, wupg_ref, wout_ref, gffn_ref, wrt_ref, brt_ref,
                         h_ref, rec_out_ref, pay_out_ref, pool_ref, cnt_ref,
                         aext_ref, s_ref, ltri_ref, run_ref, merged_ref):
    s = pl.program_id(0)

    @pl.when(s == 0)
    def _():
        _init_ltri(ltri_ref)
        run_ref[...] = jnp.zeros_like(run_ref)
        aext_ref[TL:TL + HIST_ROWS, :] = jnp.zeros((HIST_ROWS, POOL_WIDTH), F32)
        merged_ref[...] = jnp.zeros_like(merged_ref)

    j = jnp.minimum(s, NT_PROMPT - 1) % TILES_PER_SEQ

    h = xprev_ref[...] + _dot(merged_ref[...], wout_ref[...])
    xn = _rms(x_ref[...], gmix_ref[...]).astype(BF16)
    a = _dot(xn, win_ref[:, 0:512])
    h_ref[...] = h
    hn = _rms(h, gffn_ref[...])
    lg = _router_logits(hn, wrt_ref, brt_ref)

    aext_ref[0:HIST_ROWS, :] = jnp.where(j > 0, aext_ref[TL:TL + HIST_ROWS, :], 0.0)
    aext_ref[HIST_ROWS:HIST_ROWS + TL, :] = a
    pool_ref[...] = a[TL - HIST_ROWS:TL, :]
    pos = j * TL + lax.broadcasted_iota(I32, (TL, 1), 0)

    def load_shifted(k, g):
        return aext_ref[HIST_ROWS - k:HIST_ROWS - k + TL, g * POOL_GROUP_DIM:(g + 1) * POOL_GROUP_DIM]

    v = _dot(xn, win_ref[:, 1024:1536])
    pa = _pool_windows(load_shifted, a, pos, wpool_ref, pscale_ref)
    u = _dot(xn, win_ref[:, 512:1024])
    _assign(lg, ltri_ref, run_ref, rec_out_ref, pay_out_ref, counted=s > 0)
    cnt_ref[...] = run_ref[...]
    ga = _dot(xn, win_ref[:, 1536:2560])

    vn = _rms(v, gv_ref[...]).astype(BF16)
    rr = lax.broadcasted_iota(I32, (GMLP_CHUNK, GMLP_CHUNK), 0)
    cc = lax.broadcasted_iota(I32, (GMLP_CHUNK, GMLP_CHUNK), 1)
    for hd in range(GMLP_HEADS):
        cols = slice(hd * GMLP_HEAD_DIM, (hd + 1) * GMLP_HEAD_DIM)
        wsm = jnp.where(cc <= rr, ws_ref[hd], 0.0).astype(BF16)
        chunks = [vn[c * GMLP_CHUNK:(c + 1) * GMLP_CHUNK, cols] for c in range(TL // GMLP_CHUNK)]
        mixed = _dot(wsm, jnp.concatenate(chunks, axis=1))
        for c in range(TL // GMLP_CHUNK):
            rows = slice(c * GMLP_CHUNK, (c + 1) * GMLP_CHUNK)
            s_ref[rows, cols] = (mixed[:, c * GMLP_HEAD_DIM:(c + 1) * GMLP_HEAD_DIM]
                                 + bsp_ref[rows, hd:hd + 1])
    pb = u * s_ref[...]
    gb = _dot(xn, win_ref[:, 2560:3584])

    merged_ref[...] = _merge(pa, pb, ga, gb, wupp_ref, wupg_ref).astype(BF16)


def _mixer_sample_kernel(x_ref, hist_ref, run0_ref, gmix_ref, win_ref, wpool_ref, pscale_ref, gv_ref,
                         ws_ref, bsp_ref, wupp_ref, wupg_ref, wout_ref, gffn_ref, wrt_ref, brt_ref,
                         h_ref, rec_out_ref, pay_out_ref, pool_ref, vn_ref, cnt_ref,
                         aext_ref, ltri_ref, run_ref):
    _init_ltri(ltri_ref)
    run_ref[...] = run0_ref[...]

    x = x_ref[...]
    xn = _rms(x, gmix_ref[...]).astype(BF16)
    a = _dot(xn, win_ref[:, 0:512])
    u = _dot(xn, win_ref[:, 512:1024])
    v = _dot(xn, win_ref[:, 1024:1536])
    ga = _dot(xn, win_ref[:, 1536:2560])
    gb = _dot(xn, win_ref[:, 2560:3584])

    a3 = a.reshape(DEC_BATCH, DEC_SEQ, POOL_WIDTH)
    aext_ref[:, 0:HIST_ROWS, :] = hist_ref[...]
    aext_ref[:, HIST_ROWS:HIST_ROWS + DEC_SEQ, :] = a3
    pool_ref[...] = a3[:, DEC_SEQ - HIST_ROWS:DEC_SEQ, :]
    row = lax.broadcasted_iota(I32, (TL, 1), 0)
    pos = PAST_LEN + row % DEC_SEQ

    def load_shifted(k, g):
        sl = aext_ref[:, HIST_ROWS - k:HIST_ROWS - k + DEC_SEQ, g * POOL_GROUP_DIM:(g + 1) * POOL_GROUP_DIM]
        return sl.reshape(TL, POOL_GROUP_DIM)

    pa = _pool_windows(load_shifted, a, pos, wpool_ref, pscale_ref)

    vnf = _rms(v, gv_ref[...])
    vn_ref[...] = vnf
    vn = vnf.astype(BF16)
    rsel = (lax.broadcasted_iota(I32, (TL, GMLP_CHUNK), 1)
            == lax.broadcasted_iota(I32, (TL, GMLP_CHUNK), 0) % DEC_SEQ)
    rsel_b = jnp.where(rsel, 1.0, 0.0).astype(BF16)
    csel = (lax.broadcasted_iota(I32, (GMLP_CHUNK, TL), 0)
            == lax.broadcasted_iota(I32, (GMLP_CHUNK, TL), 1) % DEC_SEQ)
    csel_b = jnp.where(csel, 1.0, 0.0).astype(BF16)
    rr = lax.broadcasted_iota(I32, (TL, TL), 0)
    cc = lax.broadcasted_iota(I32, (TL, TL), 1)
    keep = (rr // DEC_SEQ == cc // DEC_SEQ) & (cc <= rr)
    s_parts = []
    for hd in range(GMLP_HEADS):
        cols = slice(hd * GMLP_HEAD_DIM, (hd + 1) * GMLP_HEAD_DIM)
        wrow = _dot(rsel_b, ws_ref[hd].astype(BF16)).astype(BF16)
        wfull = _dot(wrow, csel_b)
        wblk = jnp.where(keep, wfull, 0.0).astype(BF16)
        s_parts.append(_dot(wblk, vn[:, cols]) + bsp_ref[:, hd:hd + 1])
    pb = u * jnp.concatenate(s_parts, axis=-1)

    h = x + _dot(_merge(pa, pb, ga, gb, wupp_ref, wupg_ref).astype(BF16), wout_ref[...])
    h_ref[...] = h
    lg = _router_logits(_rms(h, gffn_ref[...]), wrt_ref, brt_ref)
    _assign(lg, ltri_ref, run_ref, rec_out_ref, pay_out_ref)
    cnt_ref[...] = run_ref[...]


def _dispatch_kernel(pad_start_ref, pad_len_ref, nvalid_ref,
                     dp_ref, ds_ref, hp_ref, hs_ref, gffn_ref, xb_ref, row_ref, zero_ref, sems):
    i = pl.program_id(0)
    slot = i % 2
    rows = row_ref.at[slot]

    def fill_and_scatter(h_ref, d_ref, n_rows):
        for c in range(n_rows // SCATTER_CHUNK):
            r0 = c * SCATTER_CHUNK
            hn = _rms(h_ref[r0:r0 + SCATTER_CHUNK, :], gffn_ref[...])
            rows[r0:r0 + SCATTER_CHUNK] = hn.reshape(SCATTER_CHUNK, SUBLANES, LANES)
            for t in range(r0, r0 + SCATTER_CHUNK):
                _row_copy(rows, t, xb_ref, d_ref[0, t], sems.at[slot]).start(priority=t % 2)

    @pl.when(i < ND_PROMPT)
    def _():
        fill_and_scatter(hp_ref, dp_ref, TD)

    @pl.when(i == ND_PROMPT)
    def _():
        fill_and_scatter(hs_ref, ds_ref, TL)

    def drain(s, n_rows):
        def body(t, c):
            _row_copy(row_ref.at[s], t, xb_ref, 0, sems.at[s]).wait()
            return c
        lax.fori_loop(0, n_rows, body, 0, unroll=DMA_UNROLL)

    @pl.when(i > 0)
    def _():
        drain(1 - slot, TD)

    @pl.when(i == ND_PROMPT)
    def _():
        drain(slot, TL)
        zero_ref[...] = jnp.zeros_like(zero_ref)
        bits = [1 << k for k in reversed(range(BM.bit_length() - 1))]

        def pad_copies(fn):
            for e in range(N_BUCKETS):
                off = pad_start_ref[e]
                n = pad_len_ref[e]
                for bit in bits:
                    take = jnp.bitwise_and(n, bit)

                    @pl.when(take != 0)
                    def _(off=off, bit=bit):
                        fn(pltpu.make_async_copy(zero_ref.at[pl.ds(0, bit)], xb_ref.at[pl.ds(off, bit)],
                                                 sems.at[slot]))
                    off = off + take
            for blk in range(NB_MIN, NB_MAX):
                @pl.when(blk >= nvalid_ref[0])
                def _(blk=blk):
                    for half in range(BM // ZERO_ROWS):
                        fn(pltpu.make_async_copy(
                            zero_ref, xb_ref.at[pl.ds(blk * BM + half * ZERO_ROWS, ZERO_ROWS)], sems.at[slot]))

        pad_copies(lambda cp: cp.start())
        pad_copies(lambda cp: cp.wait())


def _experts_kernel(blk_lo_ref, blk_hi_ref, nvalid_ref, xb_ref,
                    wg_lo_ref, wu_lo_ref, wd_lo_ref, wg_hi_ref, wu_hi_ref, wd_hi_ref,
                    yb_ref, wgb_lo_ref, wub_lo_ref, wdb_lo_ref, wgb_hi_ref, wub_hi_ref, wdb_hi_ref):
    i = pl.program_id(0)
    prev = jnp.maximum(i - 1, 0)

    @pl.when((i == 0) | (blk_lo_ref[i] != blk_lo_ref[prev]))
    def _():
        wgb_lo_ref[...] = wg_lo_ref[...].astype(BF16)
        wub_lo_ref[...] = wu_lo_ref[...].astype(BF16)
        wdb_lo_ref[...] = wd_lo_ref[...].astype(BF16)

    @pl.when((i == 0) | (blk_hi_ref[i] != blk_hi_ref[prev]))
    def _():
        wgb_hi_ref[...] = wg_hi_ref[...].astype(BF16)
        wub_hi_ref[...] = wu_hi_ref[...].astype(BF16)
        wdb_hi_ref[...] = wd_hi_ref[...].astype(BF16)

    @pl.when(i < nvalid_ref[0])
    def _():
        x = xb_ref[...].reshape(BM, D_MODEL).astype(BF16)

        def hidden(wgb, wub):
            hg = _dot(x, wgb[...])
            return ((hg * jax.nn.sigmoid(hg)) * _dot(x, wub[...])).astype(BF16)

        act_lo = hidden(wgb_lo_ref, wub_lo_ref)
        act_hi = hidden(wgb_hi_ref, wub_hi_ref)
        y_lo = _dot(act_lo, wdb_lo_ref[...])
        y_hi = _dot(act_hi, wdb_hi_ref[...])
        yb_ref[...] = _pack_bf16_pair(y_lo, y_hi).reshape(BM, SUBLANES, LANES)

    @pl.when(i >= nvalid_ref[0])
    def _():
        yb_ref[...] = jnp.zeros_like(yb_ref)


def _combine_kernel(d_ref, dn_ref, payp_ref, pays_ref, hp_ref, hs_ref, gfin_ref, yb_ref,
                    yp_ref, ys_ref, buf_ref, sems):
    i = pl.program_id(0)
    slot = i % 2

    @pl.when(i == 0)
    def _():
        def body(t, c):
            _row_copy(yb_ref, d_ref[0, t], buf_ref.at[slot], t, sems.at[slot]).start()
            return c
        lax.fori_loop(0, TL, body, 0, unroll=DMA_UNROLL)

    def drain(t, c):
        _row_copy(yb_ref, 0, buf_ref.at[slot], t, sems.at[slot]).wait()
        return c

    lax.fori_loop(0, TL, drain, 0, unroll=DMA_UNROLL)

    def finish_chunk(h_ref, pay_ref, out_ref, r0):
        rows = slice(r0, r0 + DMA_CHUNK)
        y_lo, y_hi = _unpack_bf16_pair(buf_ref.at[slot][rows].reshape(DMA_CHUNK, D_MODEL))
        pay = pay_ref[rows, :]
        moe = y_lo * pay[:, PAY_LO:PAY_LO + 1] + y_hi * pay[:, PAY_HI:PAY_HI + 1]
        out_ref[rows, :] = _rms(h_ref[rows, :] + moe, gfin_ref[...])

    @pl.when(i < NT_PROMPT)
    def _():
        for c in range(TL // DMA_CHUNK):
            r0 = c * DMA_CHUNK
            finish_chunk(hp_ref, payp_ref, yp_ref, r0)
            for t in range(r0, r0 + DMA_CHUNK):
                _row_copy(yb_ref, dn_ref[0, t], buf_ref.at[1 - slot], t, sems.at[1 - slot]).start(priority=t % 2)

    @pl.when(i == NT_PROMPT)
    def _():
        for c in range(TL // DMA_CHUNK):
            finish_chunk(hs_ref, pays_ref, ys_ref, c * DMA_CHUNK)


def _const_spec(shape):
    zeros = (0,) * len(shape)
    return pl.BlockSpec(shape, lambda *_: zeros, pipeline_mode=pl.Buffered(1))


def kernel(x_prompt, x_sample, state_pool, g_mix, w_in, w_pool, pool_scale, g_v, w_spatial, b_spatial,
           w_up_pool, w_up_gmlp, w_out, g_ffn, w_group, b_group, w_router, b_router, w_gate, w_up,
           w_down, g_final):
    gmix = g_mix[0][None, :]
    win = w_in[0].astype(BF16)
    wpool = w_pool[0].astype(BF16)
    pscale = pool_scale[0][None, :]
    gv = g_v[0][None, :]
    ws = w_spatial[0]
    bsp_t = b_spatial[0].T
    wupp = w_up_pool[0].astype(BF16)
    wupg = w_up_gmlp[0].astype(BF16)
    wout = w_out[0].astype(BF16)
    gffn = g_ffn[0][None, :]
    n_route = N_GROUPS + N_EXPERTS
    wrt = jnp.pad(jnp.concatenate([w_group[0], w_router[0]], axis=1), ((0, 0), (0, LANES - n_route)))
    brt = jnp.pad(jnp.concatenate([b_group[0], b_router[0]]), (0, LANES - n_route))[None, :]
    wg = w_gate[0]
    wu = w_up[0]
    wd = w_down[0]
    gfin = g_final[None, :]
    bsp_prompt = jnp.tile(bsp_t, (TL // GMLP_CHUNK, 1))
    bsp_sample = jnp.tile(bsp_t[:DEC_SEQ], (DEC_BATCH, 1))
    hist = jnp.pad(state_pool[0], ((0, 0), (1, 0), (0, 0)))

    weight_specs = [
        _const_spec((1, D_MODEL)),
        _const_spec((D_MODEL, IN_COLS)),
        _const_spec((4, POOL_GROUP_DIM, POOL_GROUP_DIM)),
        _const_spec((1, POOL_WIDTH)),
        _const_spec((1, GMLP_WIDTH)),
        _const_spec((GMLP_HEADS, GMLP_CHUNK, GMLP_CHUNK)),
        _const_spec((TL, GMLP_HEADS)),
        _const_spec((POOL_WIDTH, D_MODEL)),
        _const_spec((GMLP_WIDTH, D_MODEL)),
        _const_spec((D_MODEL, D_MODEL)),
        _const_spec((1, D_MODEL)),
        _const_spec((D_MODEL, LANES)),
        _const_spec((1, LANES)),
    ]
    any_spec = pl.BlockSpec(memory_space=pl.ANY)
    route_scratch = [
        pltpu.VMEM((TL, TL), BF16),
        pltpu.VMEM((1, LANES), F32),
    ]

    def started(s):
        return jnp.minimum(s, NT_PROMPT - 1)

    def finished(s):
        return jnp.maximum(s - 1, 0)

    x_tiles = x_prompt.reshape(NT_PROMPT, TL, D_MODEL)
    h_p, rec_p, pay_p, pool_p, cnt_p = pl.pallas_call(
        _mixer_prompt_kernel,
        grid=(NT_PROMPT + 1,),
        in_specs=[
            pl.BlockSpec((None, TL, D_MODEL), lambda s: (started(s), 0, 0)),
            pl.BlockSpec((None, TL, D_MODEL), lambda s: (finished(s), 0, 0)),
        ] + weight_specs,
        out_specs=[
            pl.BlockSpec((None, TL, D_MODEL), lambda s: (finished(s), 0, 0)),
            pl.BlockSpec((None, REC_ROWS, TL), lambda s: (finished(s), 0, 0)),
            pl.BlockSpec((TL, LANES), lambda s: (finished(s), 0)),
            pl.BlockSpec((None, HIST_ROWS, POOL_WIDTH), lambda s: (started(s) // TILES_PER_SEQ, 0, 0)),
            pl.BlockSpec((1, LANES), lambda s: (0, 0)),
        ],
        out_shape=[
            jax.ShapeDtypeStruct((NT_PROMPT, TL, D_MODEL), F32),
            jax.ShapeDtypeStruct((NT_PROMPT, REC_ROWS, TL), I32),
            jax.ShapeDtypeStruct((T_PROMPT, LANES), F32),
            jax.ShapeDtypeStruct((BATCH, HIST_ROWS, POOL_WIDTH), F32),
            jax.ShapeDtypeStruct((1, LANES), F32),
        ],
        scratch_shapes=[
            pltpu.VMEM((HIST_ROWS + TL, POOL_WIDTH), F32),
            pltpu.VMEM((TL, GMLP_WIDTH), F32),
        ] + route_scratch + [pltpu.VMEM((TL, D_MODEL), BF16)],
        compiler_params=pltpu.CompilerParams(
            dimension_semantics=("arbitrary",), vmem_limit_bytes=VMEM_LIMIT),
        name="mixer_prompt",
    )(x_tiles, x_tiles, gmix, win, wpool, pscale, gv, ws, bsp_prompt, wupp, wupg, wout, gffn, wrt, brt)

    h_s, rec_s, pay_s, pool_s, vn_s, cnt = pl.pallas_call(
        _mixer_sample_kernel,
        grid=(1,),
        in_specs=[
            _const_spec((TL, D_MODEL)),
            _const_spec((DEC_BATCH, HIST_ROWS, POOL_WIDTH)),
            _const_spec((1, LANES)),
        ] + weight_specs,
        out_specs=[
            _const_spec((TL, D_MODEL)),
            pl.BlockSpec((None, REC_ROWS, TL), lambda i: (0, 0, 0)),
            _const_spec((TL, LANES)),
            _const_spec((DEC_BATCH, HIST_ROWS, POOL_WIDTH)),
            _const_spec((TL, GMLP_WIDTH)),
            _const_spec((1, LANES)),
        ],
        out_shape=[
            jax.ShapeDtypeStruct((T_SAMPLE, D_MODEL), F32),
            jax.ShapeDtypeStruct((1, REC_ROWS, TL), I32),
            jax.ShapeDtypeStruct((T_SAMPLE, LANES), F32),
            jax.ShapeDtypeStruct((DEC_BATCH, HIST_ROWS, POOL_WIDTH), F32),
            jax.ShapeDtypeStruct((T_SAMPLE, GMLP_WIDTH), F32),
            jax.ShapeDtypeStruct((1, LANES), F32),
        ],
        scratch_shapes=[
            pltpu.VMEM((DEC_BATCH, HIST_ROWS + DEC_SEQ, POOL_WIDTH), F32),
        ] + route_scratch,
        compiler_params=pltpu.CompilerParams(
            dimension_semantics=("arbitrary",), vmem_limit_bytes=VMEM_LIMIT),
        name="mixer_sample",
    )(x_sample.reshape(T_SAMPLE, D_MODEL), hist, cnt_p, gmix, win, wpool, pscale, gv, ws, bsp_sample,
      wupp, wupg, wout, gffn, wrt, brt)

    counts = cnt[0, :N_BUCKETS].astype(I32)
    nblk = (counts + BM - 1) // BM
    bend = jnp.cumsum(nblk)
    bstart = bend - nblk
    nvalid = bend[-1:]
    step = jnp.minimum(jnp.arange(NB_MAX, dtype=I32), nvalid - 1)
    bucket_ids = jnp.arange(N_BUCKETS, dtype=I32)
    blk_bucket = jnp.minimum(jnp.sum(step[:, None] >= bend[None, :], axis=1), N_BUCKETS - 1)
    expert_lo = jnp.array([g * EXPERTS_PER_GROUP + p for g in range(N_GROUPS) for p in PAIR_LO], I32)
    expert_hi = jnp.array([g * EXPERTS_PER_GROUP + p for g in range(N_GROUPS) for p in PAIR_HI], I32)
    in_bucket = blk_bucket[:, None] == bucket_ids[None, :]
    blk_lo = jnp.sum(jnp.where(in_bucket, expert_lo[None, :], 0), axis=1).astype(I32)
    blk_hi = jnp.sum(jnp.where(in_bucket, expert_hi[None, :], 0), axis=1).astype(I32)
    pad_start = bstart * BM + counts
    pad_len = nblk * BM - counts
    rec = jnp.concatenate([rec_p, rec_s], axis=0)
    first_slot = jnp.sum(jnp.where(rec[:, REC_BUCKET, :, None] == bucket_ids[None, None, :],
                                   (bstart * BM)[None, None, :], 0), axis=-1)
    slot_of = first_slot + rec[:, REC_RANK]
    d_prompt = slot_of[:NT_PROMPT].reshape(ND_PROMPT, 1, TD)
    d_sample = slot_of[NT_PROMPT:].reshape(1, 1, TL)
    d_tiles = slot_of[:, None, :]
    h_moves = h_p.reshape(ND_PROMPT, TD, D_MODEL)

    def move_map(i, *_):
        return (jnp.minimum(i, ND_PROMPT - 1), 0, 0)

    def hp_map(i, *_):
        return (jnp.minimum(i, NT_PROMPT - 1), 0, 0)

    def smem_tile(rows, index_map):
        return pl.BlockSpec((None, 1, rows), index_map, memory_space=pltpu.SMEM)

    xb = pl.pallas_call(
        _dispatch_kernel,
        grid_spec=pltpu.PrefetchScalarGridSpec(
            num_scalar_prefetch=3,
            grid=(ND_PROMPT + 1,),
            in_specs=[
                smem_tile(TD, move_map),
                smem_tile(TL, lambda i, *_: (0, 0, 0)),
                pl.BlockSpec((None, TD, D_MODEL), move_map),
                pl.BlockSpec((TL, D_MODEL), lambda i, *_: (0, 0)),
                pl.BlockSpec((1, D_MODEL), lambda i, *_: (0, 0)),
            ],
            out_specs=any_spec,
            scratch_shapes=[
                pltpu.VMEM((2, TD, SUBLANES, LANES), F32),
                pltpu.VMEM((ZERO_ROWS, SUBLANES, LANES), F32),
                pltpu.SemaphoreType.DMA((2,)),
            ],
        ),
        out_shape=jax.ShapeDtypeStruct((N_SLOTS, SUBLANES, LANES), F32),
        compiler_params=pltpu.CompilerParams(
            dimension_semantics=("arbitrary",), vmem_limit_bytes=VMEM_LIMIT),
        name="dispatch",
    )(pad_start, pad_len, nvalid, d_prompt, d_sample, h_moves, h_s, gffn)

    def xb_map(i, lo, hi, nv):
        return (jnp.minimum(i, nv[0] - 1), 0, 0)

    def lo_map(i, lo, hi, nv):
        return (lo[i], 0, 0)

    def hi_map(i, lo, hi, nv):
        return (hi[i], 0, 0)

    in_proj = (None, D_MODEL, D_EXPERT)
    out_proj = (None, D_EXPERT, D_MODEL)
    yb = pl.pallas_call(
        _experts_kernel,
        grid_spec=pltpu.PrefetchScalarGridSpec(
            num_scalar_prefetch=3,
            grid=(NB_MAX,),
            in_specs=[
                pl.BlockSpec((BM, SUBLANES, LANES), xb_map),
                pl.BlockSpec(in_proj, lo_map), pl.BlockSpec(in_proj, lo_map), pl.BlockSpec(out_proj, lo_map),
                pl.BlockSpec(in_proj, hi_map), pl.BlockSpec(in_proj, hi_map), pl.BlockSpec(out_proj, hi_map),
            ],
            out_specs=pl.BlockSpec((BM, SUBLANES, LANES), lambda i, lo, hi, nv: (i, 0, 0)),
            scratch_shapes=2 * [
                pltpu.VMEM((D_MODEL, D_EXPERT), BF16),
                pltpu.VMEM((D_MODEL, D_EXPERT), BF16),
                pltpu.VMEM((D_EXPERT, D_MODEL), BF16),
            ],
        ),
        out_shape=jax.ShapeDtypeStruct((N_SLOTS, SUBLANES, LANES), I32),
        compiler_params=pltpu.CompilerParams(
            dimension_semantics=("arbitrary",), vmem_limit_bytes=VMEM_LIMIT),
        name="experts",
    )(blk_lo, blk_hi, nvalid, xb, wg, wu, wd, wg, wu, wd)

    def next_tile(i):
        return (jnp.minimum(i + 1, NT_TOTAL - 1), 0, 0)

    y_p, y_s = pl.pallas_call(
        _combine_kernel,
        grid=(NT_TOTAL,),
        in_specs=[
            smem_tile(TL, lambda i: (i, 0, 0)),
            smem_tile(TL, next_tile),
            pl.BlockSpec((TL, LANES), lambda i: (jnp.minimum(i, NT_PROMPT - 1), 0)),
            pl.BlockSpec((TL, LANES), lambda i: (0, 0)),
            pl.BlockSpec((None, TL, D_MODEL), hp_map),
            pl.BlockSpec((TL, D_MODEL), lambda i: (0, 0)),
            pl.BlockSpec((1, D_MODEL), lambda i: (0, 0)),
            any_spec,
        ],
        out_specs=[
            pl.BlockSpec((None, TL, D_MODEL), hp_map),
            pl.BlockSpec((TL, D_MODEL), lambda i: (0, 0)),
        ],
        out_shape=[
            jax.ShapeDtypeStruct((NT_PROMPT, TL, D_MODEL), F32),
            jax.ShapeDtypeStruct((T_SAMPLE, D_MODEL), F32),
        ],
        scratch_shapes=[
            pltpu.VMEM((2, TL, SUBLANES, LANES), I32),
            pltpu.SemaphoreType.DMA((2,)),
        ],
        compiler_params=pltpu.CompilerParams(
            dimension_semantics=("arbitrary",), vmem_limit_bytes=VMEM_LIMIT),
        name="combine",
    )(d_tiles, d_tiles, pay_p, pay_s, h_p, h_s, gfin, yb)

    y_sample = y_s.reshape(DEC_BATCH, DEC_SEQ, D_MODEL)
    new_pool_prompt = pool_p[None, :, 1:, :]
    new_pool_sample = pool_s[None, :, 1:, :]
    new_gmlp_v_sample = vn_s.reshape(1, DEC_BATCH, DEC_SEQ, GMLP_WIDTH)
    y_prompt = y_p.reshape(BATCH, SEQ, D_MODEL)
    return (y_prompt, y_sample, new_pool_prompt, new_pool_sample, new_gmlp_v_sample)
```

```python
import jax
import jax.numpy as jnp
from jax import lax
from jax.experimental import pallas as pl
from jax.experimental.pallas import tpu as pltpu

F32 = jnp.float32
BF16 = jnp.bfloat16
I32 = jnp.int32

D_MODEL = 1024
SEQ = 16384
BATCH = 2
DEC_BATCH = 16
DEC_SEQ = 32
PAST_LEN = 1024
POOL_WIDTH = 512
POOL_GROUP_DIM = 128
POOL_WINDOWS = (2, 4, 8, 16)
HIST_ROWS = 16
GMLP_WIDTH = 512
GMLP_HEADS = 4
GMLP_HEAD_DIM = 128
GMLP_CHUNK = 128
IN_COLS = POOL_WIDTH + 2 * GMLP_WIDTH + 2 * D_MODEL
N_GROUPS = 4
EXPERTS_PER_GROUP = 4
N_EXPERTS = 16
D_EXPERT = 512
EPS = 1e-6

LANES = 128
SUBLANES = 8
assert D_MODEL == SUBLANES * LANES
TL = 512
BM = 512
T_PROMPT = BATCH * SEQ
T_SAMPLE = DEC_BATCH * DEC_SEQ
T_TOTAL = T_PROMPT + T_SAMPLE
NT_PROMPT = T_PROMPT // TL
TILES_PER_SEQ = SEQ // TL
NT_TOTAL = T_TOTAL // TL
assert T_SAMPLE == TL
PAIR_LO = (0, 0, 0, 1, 1, 2)
PAIR_HI = (1, 2, 3, 2, 3, 3)
N_PAIRS = len(PAIR_LO)
N_BUCKETS = N_GROUPS * N_PAIRS
NB_MIN = T_TOTAL // BM
NB_MAX = (T_TOTAL + N_BUCKETS * (BM - 1)) // BM
N_SLOTS = NB_MAX * BM
ZERO_ROWS = BM // 2
TD = 4 * TL
ND_PROMPT = T_PROMPT // TD
DMA_UNROLL = 8
DMA_CHUNK = 64
SCATTER_CHUNK = 32
VMEM_LIMIT = 56 * 1024 * 1024

REC_BUCKET, REC_RANK = 0, 1
REC_ROWS = 8
PAY_LO, PAY_HI = 0, 1


def _dot(a, b):
    return jnp.dot(a, b, preferred_element_type=F32)


def _rms(x, g):
    return x * lax.rsqrt(jnp.mean(x * x, axis=-1, keepdims=True) + EPS) * g


def _split_bf16(x):
    hi = x.astype(BF16)
    lo = (x - hi.astype(F32)).astype(BF16)
    return hi, lo


def _pack_bf16_pair(lo, hi):
    lo_bits = lax.bitcast_convert_type(lo.astype(BF16).astype(F32), I32)
    hi_bits = lax.bitcast_convert_type(hi.astype(BF16).astype(F32), I32)
    return lax.shift_right_logical(lo_bits, jnp.int32(16)) | hi_bits


def _unpack_bf16_pair(packed):
    lo = lax.bitcast_convert_type(lax.shift_left(packed, jnp.int32(16)), F32)
    hi = lax.bitcast_convert_type(packed & jnp.int32(-65536), F32)
    return lo, hi


def _row_copy(src_ref, s, dst_ref, d, sem):
    return pltpu.make_async_copy(src_ref.at[pl.ds(s, 1)], dst_ref.at[pl.ds(d, 1)], sem)


def _pool_windows(load_shifted, a, pos, wpool_ref, pscale_ref):
    outs = []
    for g, w in enumerate(POOL_WINDOWS):
        cols = slice(g * POOL_GROUP_DIM, (g + 1) * POOL_GROUP_DIM)
        acc = a[:, cols]
        for k in range(1, w):
            acc = acc + load_shifted(k, g)
        cnt = jnp.minimum(pos + 1, w).astype(F32)
        d = (acc / cnt - a[:, cols]).astype(BF16)
        outs.append(_dot(d, wpool_ref[g]))
    return jnp.concatenate(outs, axis=-1) * pscale_ref[...]


def _merge(pa, pb, ga, gb, wupp_ref, wupg_ref):
    return (jax.nn.sigmoid(ga) * _dot(pa.astype(BF16), wupp_ref[...])
            + jax.nn.sigmoid(gb) * _dot(pb.astype(BF16), wupg_ref[...]))


def _init_ltri(ltri_ref):
    r = lax.broadcasted_iota(I32, (TL, TL), 0)
    c = lax.broadcasted_iota(I32, (TL, TL), 1)
    ltri_ref[...] = jnp.where(c < r, 1.0, 0.0).astype(BF16)


def _router_logits(hn, wrt_ref, brt_ref):
    hi, lo = _split_bf16(hn)
    whi, wlo = _split_bf16(wrt_ref[...])
    hi_w = _dot(hi, jnp.concatenate([whi, wlo], axis=1))
    return hi_w[:, 0:LANES] + hi_w[:, LANES:2 * LANES] + _dot(lo, whi) + brt_ref[...]


def _assign(lg, ltri_ref, run_ref, rec_out_ref, pay_out_ref, counted=True):
    lane = lax.broadcasted_iota(I32, lg.shape, 1)
    neg = jnp.float32(-jnp.inf)
    big = jnp.int32(1 << 20)

    def first_argmax(v, vmax):
        return jnp.min(jnp.where(v == vmax, lane, big), axis=-1, keepdims=True)

    gmask = lane < N_GROUPS
    glm = jnp.where(gmask, lg, neg)
    gmax = jnp.max(glm, axis=-1, keepdims=True)
    grp = first_argmax(glm, gmax)
    p_g = 1.0 / jnp.sum(jnp.where(gmask, jnp.exp(lg - gmax), 0.0), axis=-1, keepdims=True)

    lo_lane = N_GROUPS + EXPERTS_PER_GROUP * grp
    elm = jnp.where((lane >= lo_lane) & (lane < lo_lane + EXPERTS_PER_GROUP), lg, neg)
    v1 = jnp.max(elm, axis=-1, keepdims=True)
    i1 = first_argmax(elm, v1)
    elm2 = jnp.where(lane == i1, neg, elm)
    v2 = jnp.max(elm2, axis=-1, keepdims=True)
    i2 = first_argmax(elm2, v2)
    ex2 = jnp.exp(v2 - v1)
    inv = 1.0 / (1.0 + ex2)
    w_top = inv * p_g
    w_second = ex2 * inv * p_g

    loc1 = i1 - lo_lane
    loc2 = i2 - lo_lane
    top_is_lo = loc1 < loc2
    e_lo = jnp.minimum(loc1, loc2)
    e_hi = jnp.maximum(loc1, loc2)
    pair = jnp.where(e_lo == 0, 0, jnp.where(e_lo == 1, 3, 5)) + e_hi - e_lo - 1
    bucket = grp * N_PAIRS + pair
    w_lo = jnp.where(top_is_lo, w_top, w_second)
    w_hi = jnp.where(top_is_lo, w_second, w_top)

    sel = lane == bucket
    onehot = jnp.where(sel, 1.0, 0.0)
    before = _dot(ltri_ref[...], onehot.astype(BF16)) + run_ref[...]
    rank = jnp.sum(jnp.where(sel, before, 0.0), axis=-1, keepdims=True)
    run_ref[...] = run_ref[...] + jnp.where(counted, jnp.sum(onehot, axis=0, keepdims=True), 0.0)

    rec = jnp.where(lane == REC_BUCKET, bucket.astype(F32), jnp.where(lane == REC_RANK, rank, 0.0))
    rec_out_ref[...] = rec.T[0:REC_ROWS, :].astype(I32)
    pay_out_ref[...] = jnp.where(lane == PAY_LO, w_lo, jnp.where(lane == PAY_HI, w_hi, 0.0))


def _mixer_prompt_kernel(x_ref, xprev_ref, gmix_ref, win_ref, wpool_ref, pscale_ref, gv_ref, ws_ref,
                         bsp_ref, wupp_ref, wupg_ref, wout_ref, gffn_ref, wrt_ref, brt_ref,
                         h_ref, rec_out_ref, pay_out_ref, pool_ref, cnt_ref,
                         aext_ref, s_ref, ltri_ref, run_ref, merged_ref):
    s = pl.program_id(0)

    @pl.when(s == 0)
    def _():
        _init_ltri(ltri_ref)
        run_ref[...] = jnp.zeros_like(run_ref)
        aext_ref[TL:TL + HIST_ROWS, :] = jnp.zeros((HIST_ROWS, POOL_WIDTH), F32)
        merged_ref[...] = jnp.zeros_like(merged_ref)

    j = jnp.minimum(s, NT_PROMPT - 1) % TILES_PER_SEQ

    h = xprev_ref[...] + _dot(merged_ref[...], wout_ref[...])
    xn = _rms(x_ref[...], gmix_ref[...]).astype(BF16)
    a = _dot(xn, win_ref[:, 0:512])
    h_ref[...] = h
    hn = _rms(h, gffn_ref[...])
    lg = _router_logits(hn, wrt_ref, brt_ref)

    aext_ref[0:HIST_ROWS, :] = jnp.where(j > 0, aext_ref[TL:TL + HIST_ROWS, :], 0.0)
    aext_ref[HIST_ROWS:HIST_ROWS + TL, :] = a
    pool_ref[...] = a[TL - HIST_ROWS:TL, :]
    pos = j * TL + lax.broadcasted_iota(I32, (TL, 1), 0)

    def load_shifted(k, g):
        return aext_ref[HIST_ROWS - k:HIST_ROWS - k + TL, g * POOL_GROUP_DIM:(g + 1) * POOL_GROUP_DIM]

    v = _dot(xn, win_ref[:, 1024:1536])
    pa = _pool_windows(load_shifted, a, pos, wpool_ref, pscale_ref)
    u = _dot(xn, win_ref[:, 512:1024])
    _assign(lg, ltri_ref, run_ref, rec_out_ref, pay_out_ref, counted=s > 0)
    cnt_ref[...] = run_ref[...]
    ga = _dot(xn, win_ref[:, 1536:2560])

    vn = _rms(v, gv_ref[...]).astype(BF16)
    rr = lax.broadcasted_iota(I32, (GMLP_CHUNK, GMLP_CHUNK), 0)
    cc = lax.broadcasted_iota(I32, (GMLP_CHUNK, GMLP_CHUNK), 1)
    for hd in range(GMLP_HEADS):
        cols = slice(hd * GMLP_HEAD_DIM, (hd + 1) * GMLP_HEAD_DIM)
        wsm = jnp.where(cc <= rr, ws_ref[hd], 0.0).astype(BF16)
        chunks = [vn[c * GMLP_CHUNK:(c + 1) * GMLP_CHUNK, cols] for c in range(TL // GMLP_CHUNK)]
        mixed = _dot(wsm, jnp.concatenate(chunks, axis=1))
        for c in range(TL // GMLP_CHUNK):
            rows = slice(c * GMLP_CHUNK, (c + 1) * GMLP_CHUNK)
            s_ref[rows, cols] = (mixed[:, c * GMLP_HEAD_DIM:(c + 1) * GMLP_HEAD_DIM]
                                 + bsp_ref[rows, hd:hd + 1])
    pb = u * s_ref[...]
    gb = _dot(xn, win_ref[:, 2560:3584])

    merged_ref[...] = _merge(pa, pb, ga, gb, wupp_ref, wupg_ref).astype(BF16)


def _mixer_sample_kernel(x_ref, hist_ref, run0_ref, gmix_ref, win_ref, wpool_ref, pscale_ref, gv_ref,
                         ws_ref, bsp_ref, wupp_ref, wupg_ref, wout_ref, gffn_ref, wrt_ref, brt_ref,
                         h_ref, rec_out_ref, pay_out_ref, pool_ref, vn_ref, cnt_ref,
                         aext_ref, ltri_ref, run_ref):
    _init_ltri(ltri_ref)
    run_ref[...] = run0_ref[...]

    x = x_ref[...]
    xn = _rms(x, gmix_ref[...]).astype(BF16)
    a = _dot(xn, win_ref[:, 0:512])
    u = _dot(xn, win_ref[:, 512:1024])
    v = _dot(xn, win_ref[:, 1024:1536])
    ga = _dot(xn, win_ref[:, 1536:2560])
    gb = _dot(xn, win_ref[:, 2560:3584])

    a3 = a.reshape(DEC_BATCH, DEC_SEQ, POOL_WIDTH)
    aext_ref[:, 0:HIST_ROWS, :] = hist_ref[...]
    aext_ref[:, HIST_ROWS:HIST_ROWS + DEC_SEQ, :] = a3
    pool_ref[...] = a3[:, DEC_SEQ - HIST_ROWS:DEC_SEQ, :]
    row = lax.broadcasted_iota(I32, (TL, 1), 0)
    pos = PAST_LEN + row % DEC_SEQ

    def load_shifted(k, g):
        sl = aext_ref[:, HIST_ROWS - k:HIST_ROWS - k + DEC_SEQ, g * POOL_GROUP_DIM:(g + 1) * POOL_GROUP_DIM]
        return sl.reshape(TL, POOL_GROUP_DIM)

    pa = _pool_windows(load_shifted, a, pos, wpool_ref, pscale_ref)

    vnf = _rms(v, gv_ref[...])
    vn_ref[...] = vnf
    vn = vnf.astype(BF16)
    rsel = (lax.broadcasted_iota(I32, (TL, GMLP_CHUNK), 1)
            == lax.broadcasted_iota(I32, (TL, GMLP_CHUNK), 0) % DEC_SEQ)
    rsel_b = jnp.where(rsel, 1.0, 0.0).astype(BF16)
    csel = (lax.broadcasted_iota(I32, (GMLP_CHUNK, TL), 0)
            == lax.broadcasted_iota(I32, (GMLP_CHUNK, TL), 1) % DEC_SEQ)
    csel_b = jnp.where(csel, 1.0, 0.0).astype(BF16)
    rr = lax.broadcasted_iota(I32, (TL, TL), 0)
    cc = lax.broadcasted_iota(I32, (TL, TL), 1)
    keep = (rr // DEC_SEQ == cc // DEC_SEQ) & (cc <= rr)
    s_parts = []
    for hd in range(GMLP_HEADS):
        cols = slice(hd * GMLP_HEAD_DIM, (hd + 1) * GMLP_HEAD_DIM)
        wrow = _dot(rsel_b, ws_ref[hd].astype(BF16)).astype(BF16)
        wfull = _dot(wrow, csel_b)
        wblk = jnp.where(keep, wfull, 0.0).astype(BF16)
        s_parts.append(_dot(wblk, vn[:, cols]) + bsp_ref[:, hd:hd + 1])
    pb = u * jnp.concatenate(s_parts, axis=-1)

    h = x + _dot(_merge(pa, pb, ga, gb, wupp_ref, wupg_ref).astype(BF16), wout_ref[...])
    h_ref[...] = h
    lg = _router_logits(_rms(h, gffn_ref[...]), wrt_ref, brt_ref)
    _assign(lg, ltri_ref, run_ref, rec_out_ref, pay_out_ref)
    cnt_ref[...] = run_ref[...]


def _dispatch_kernel(pad_start_ref, pad_len_ref, nvalid_ref,
                     dp_ref, ds_ref, hp_ref, hs_ref, gffn_ref, xb_ref, row_ref, zero_ref, sems):
    i = pl.program_id(0)
    slot = i % 2
    rows = row_ref.at[slot]

    def fill_and_scatter(h_ref, d_ref, n_rows):
        for c in range(n_rows // SCATTER_CHUNK):
            r0 = c * SCATTER_CHUNK
            hn = _rms(h_ref[r0:r0 + SCATTER_CHUNK, :], gffn_ref[...])
            rows[r0:r0 + SCATTER_CHUNK] = hn.reshape(SCATTER_CHUNK, SUBLANES, LANES)
            for t in range(r0, r0 + SCATTER_CHUNK):
                _row_copy(rows, t, xb_ref, d_ref[0, t], sems.at[slot]).start(priority=t % 2)

    @pl.when(i < ND_PROMPT)
    def _():
        fill_and_scatter(hp_ref, dp_ref, TD)

    @pl.when(i == ND_PROMPT)
    def _():
        fill_and_scatter(hs_ref, ds_ref, TL)

    def drain(s, n_rows):
        def body(t, c):
            _row_copy(row_ref.at[s], t, xb_ref, 0, sems.at[s]).wait()
            return c
        lax.fori_loop(0, n_rows, body, 0, unroll=DMA_UNROLL)

    @pl.when(i > 0)
    def _():
        drain(1 - slot, TD)

    @pl.when(i == ND_PROMPT)
    def _():
        drain(slot, TL)
        zero_ref[...] = jnp.zeros_like(zero_ref)
        bits = [1 << k for k in reversed(range(BM.bit_length() - 1))]

        def pad_copies(fn):
            for e in range(N_BUCKETS):
                off = pad_start_ref[e]
                n = pad_len_ref[e]
                for bit in bits:
                    take = jnp.bitwise_and(n, bit)

                    @pl.when(take != 0)
                    def _(off=off, bit=bit):
                        fn(pltpu.make_async_copy(zero_ref.at[pl.ds(0, bit)], xb_ref.at[pl.ds(off, bit)],
                                                 sems.at[slot]))
                    off = off + take
            for blk in range(NB_MIN, NB_MAX):
                @pl.when(blk >= nvalid_ref[0])
                def _(blk=blk):
                    for half in range(BM // ZERO_ROWS):
                        fn(pltpu.make_async_copy(
                            zero_ref, xb_ref.at[pl.ds(blk * BM + half * ZERO_ROWS, ZERO_ROWS)], sems.at[slot]))

        pad_copies(lambda cp: cp.start())
        pad_copies(lambda cp: cp.wait())


def _experts_kernel(blk_lo_ref, blk_hi_ref, nvalid_ref, xb_ref,
                    wg_lo_ref, wu_lo_ref, wd_lo_ref, wg_hi_ref, wu_hi_ref, wd_hi_ref,
                    yb_ref, wgb_lo_ref, wub_lo_ref, wdb_lo_ref, wgb_hi_ref, wub_hi_ref, wdb_hi_ref):
    i = pl.program_id(0)
    prev = jnp.maximum(i - 1, 0)

    @pl.when((i == 0) | (blk_lo_ref[i] != blk_lo_ref[prev]))
    def _():
        wgb_lo_ref[...] = wg_lo_ref[...].astype(BF16)
        wub_lo_ref[...] = wu_lo_ref[...].astype(BF16)
        wdb_lo_ref[...] = wd_lo_ref[...].astype(BF16)

    @pl.when((i == 0) | (blk_hi_ref[i] != blk_hi_ref[prev]))
    def _():
        wgb_hi_ref[...] = wg_hi_ref[...].astype(BF16)
        wub_hi_ref[...] = wu_hi_ref[...].astype(BF16)
        wdb_hi_ref[...] = wd_hi_ref[...].astype(BF16)

    @pl.when(i < nvalid_ref[0])
    def _():
        x = xb_ref[...].reshape(BM, D_MODEL).astype(BF16)

        def hidden(wgb, wub):
            hg = _dot(x, wgb[...])
            return ((hg * jax.nn.sigmoid(hg)) * _dot(x, wub[...])).astype(BF16)

        act_lo = hidden(wgb_lo_ref, wub_lo_ref)
        act_hi = hidden(wgb_hi_ref, wub_hi_ref)
        y_lo = _dot(act_lo, wdb_lo_ref[...])
        y_hi = _dot(act_hi, wdb_hi_ref[...])
        yb_ref[...] = _pack_bf16_pair(y_lo, y_hi).reshape(BM, SUBLANES, LANES)

    @pl.when(i >= nvalid_ref[0])
    def _():
        yb_ref[...] = jnp.zeros_like(yb_ref)


def _combine_kernel(d_ref, dn_ref, payp_ref, pays_ref, hp_ref, hs_ref, gfin_ref, yb_ref,
                    yp_ref, ys_ref, buf_ref, sems):
    i = pl.program_id(0)
    slot = i % 2

    @pl.when(i == 0)
    def _():
        def body(t, c):
            _row_copy(yb_ref, d_ref[0, t], buf_ref.at[slot], t, sems.at[slot]).start()
            return c
        lax.fori_loop(0, TL, body, 0, unroll=DMA_UNROLL)

    def drain(t, c):
        _row_copy(yb_ref, 0, buf_ref.at[slot], t, sems.at[slot]).wait()
        return c

    lax.fori_loop(0, TL, drain, 0, unroll=DMA_UNROLL)

    def finish_chunk(h_ref, pay_ref, out_ref, r0):
        rows = slice(r0, r0 + DMA_CHUNK)
        y_lo, y_hi = _unpack_bf16_pair(buf_ref.at[slot][rows].reshape(DMA_CHUNK, D_MODEL))
        pay = pay_ref[rows, :]
        moe = y_lo * pay[:, PAY_LO:PAY_LO + 1] + y_hi * pay[:, PAY_HI:PAY_HI + 1]
        out_ref[rows, :] = _rms(h_ref[rows, :] + moe, gfin_ref[...])

    @pl.when(i < NT_PROMPT)
    def _():
        for c in range(TL // DMA_CHUNK):
            r0 = c * DMA_CHUNK
            finish_chunk(hp_ref, payp_ref, yp_ref, r0)
            for t in range(r0, r0 + DMA_CHUNK):
                _row_copy(yb_ref, dn_ref[0, t], buf_ref.at[1 - slot], t, sems.at[1 - slot]).start(priority=t % 2)

    @pl.when(i == NT_PROMPT)
    def _():
        for c in range(TL // DMA_CHUNK):
            finish_chunk(hs_ref, pays_ref, ys_ref, c * DMA_CHUNK)


def _const_spec(shape):
    zeros = (0,) * len(shape)
    return pl.BlockSpec(shape, lambda *_: zeros, pipeline_mode=pl.Buffered(1))


def kernel(x_prompt, x_sample, state_pool, g_mix, w_in, w_pool, pool_scale, g_v, w_spatial, b_spatial,
           w_up_pool, w_up_gmlp, w_out, g_ffn, w_group, b_group, w_router, b_router, w_gate, w_up,
           w_down, g_final):
    gmix = g_mix[0][None, :]
    win = w_in[0].astype(BF16)
    wpool = w_pool[0].astype(BF16)
    pscale = pool_scale[0][None, :]
    gv = g_v[0][None, :]
    ws = w_spatial[0]
    bsp_t = b_spatial[0].T
    wupp = w_up_pool[0].astype(BF16)
    wupg = w_up_gmlp[0].astype(BF16)
    wout = w_out[0].astype(BF16)
    gffn = g_ffn[0][None, :]
    n_route = N_GROUPS + N_EXPERTS
    wrt = jnp.pad(jnp.concatenate([w_group[0], w_router[0]], axis=1), ((0, 0), (0, LANES - n_route)))
    brt = jnp.pad(jnp.concatenate([b_group[0], b_router[0]]), (0, LANES - n_route))[None, :]
    wg = w_gate[0]
    wu = w_up[0]
    wd = w_down[0]
    gfin = g_final[None, :]
    bsp_prompt = jnp.tile(bsp_t, (TL // GMLP_CHUNK, 1))
    bsp_sample = jnp.tile(bsp_t[:DEC_SEQ], (DEC_BATCH, 1))
    hist = jnp.pad(state_pool[0], ((0, 0), (1, 0), (0, 0)))

    weight_specs = [
        _const_spec((1, D_MODEL)),
        _const_spec((D_MODEL, IN_COLS)),
        _const_spec((4, POOL_GROUP_DIM, POOL_GROUP_DIM)),
        _const_spec((1, POOL_WIDTH)),
        _const_spec((1, GMLP_WIDTH)),
        _const_spec((GMLP_HEADS, GMLP_CHUNK, GMLP_CHUNK)),
        _const_spec((TL, GMLP_HEADS)),
        _const_spec((POOL_WIDTH, D_MODEL)),
        _const_spec((GMLP_WIDTH, D_MODEL)),
        _const_spec((D_MODEL, D_MODEL)),
        _const_spec((1, D_MODEL)),
        _const_spec((D_MODEL, LANES)),
        _const_spec((1, LANES)),
    ]
    any_spec = pl.BlockSpec(memory_space=pl.ANY)
    route_scratch = [
        pltpu.VMEM((TL, TL), BF16),
        pltpu.VMEM((1, LANES), F32),
    ]

    def started(s):
        return jnp.minimum(s, NT_PROMPT - 1)

    def finished(s):
        return jnp.maximum(s - 1, 0)

    x_tiles = x_prompt.reshape(NT_PROMPT, TL, D_MODEL)
    h_p, rec_p, pay_p, pool_p, cnt_p = pl.pallas_call(
        _mixer_prompt_kernel,
        grid=(NT_PROMPT + 1,),
        in_specs=[
            pl.BlockSpec((None, TL, D_MODEL), lambda s: (started(s), 0, 0)),
            pl.BlockSpec((None, TL, D_MODEL), lambda s: (finished(s), 0, 0)),
        ] + weight_specs,
        out_specs=[
            pl.BlockSpec((None, TL, D_MODEL), lambda s: (finished(s), 0, 0)),
            pl.BlockSpec((None, REC_ROWS, TL), lambda s: (finished(s), 0, 0)),
            pl.BlockSpec((TL, LANES), lambda s: (finished(s), 0)),
            pl.BlockSpec((None, HIST_ROWS, POOL_WIDTH), lambda s: (started(s) // TILES_PER_SEQ, 0, 0)),
            pl.BlockSpec((1, LANES), lambda s: (0, 0)),
        ],
        out_shape=[
            jax.ShapeDtypeStruct((NT_PROMPT, TL, D_MODEL), F32),
            jax.ShapeDtypeStruct((NT_PROMPT, REC_ROWS, TL), I32),
            jax.ShapeDtypeStruct((T_PROMPT, LANES), F32),
            jax.ShapeDtypeStruct((BATCH, HIST_ROWS, POOL_WIDTH), F32),
            jax.ShapeDtypeStruct((1, LANES), F32),
        ],
        scratch_shapes=[
            pltpu.VMEM((HIST_ROWS + TL, POOL_WIDTH), F32),
            pltpu.VMEM((TL, GMLP_WIDTH), F32),
        ] + route_scratch + [pltpu.VMEM((TL, D_MODEL), BF16)],
        compiler_params=pltpu.CompilerParams(
            dimension_semantics=("arbitrary",), vmem_limit_bytes=VMEM_LIMIT),
        name="mixer_prompt",
    )(x_tiles, x_tiles, gmix, win, wpool, pscale, gv, ws, bsp_prompt, wupp, wupg, wout, gffn, wrt, brt)

    h_s, rec_s, pay_s, pool_s, vn_s, cnt = pl.pallas_call(
        _mixer_sample_kernel,
        grid=(1,),
        in_specs=[
            _const_spec((TL, D_MODEL)),
            _const_spec((DEC_BATCH, HIST_ROWS, POOL_WIDTH)),
            _const_spec((1, LANES)),
        ] + weight_specs,
        out_specs=[
            _const_spec((TL, D_MODEL)),
            pl.BlockSpec((None, REC_ROWS, TL), lambda i: (0, 0, 0)),
            _const_spec((TL, LANES)),
            _const_spec((DEC_BATCH, HIST_ROWS, POOL_WIDTH)),
            _const_spec((TL, GMLP_WIDTH)),
            _const_spec((1, LANES)),
        ],
        out_shape=[
            jax.ShapeDtypeStruct((T_SAMPLE, D_MODEL), F32),
            jax.ShapeDtypeStruct((1, REC_ROWS, TL), I32),
            jax.ShapeDtypeStruct((T_SAMPLE, LANES), F32),
            jax.ShapeDtypeStruct((DEC_BATCH, HIST_ROWS, POOL_WIDTH), F32),
            jax.ShapeDtypeStruct((T_SAMPLE, GMLP_WIDTH), F32),
            jax.ShapeDtypeStruct((1, LANES), F32),
        ],
        scratch_shapes=[
            pltpu.VMEM((DEC_BATCH, HIST_ROWS + DEC_SEQ, POOL_WIDTH), F32),
        ] + route_scratch,
        compiler_params=pltpu.CompilerParams(
            dimension_semantics=("arbitrary",), vmem_limit_bytes=VMEM_LIMIT),
        name="mixer_sample",
    )(x_sample.reshape(T_SAMPLE, D_MODEL), hist, cnt_p, gmix, win, wpool, pscale, gv, ws, bsp_sample,
      wupp, wupg, wout, gffn, wrt, brt)

    counts = cnt[0, :N_BUCKETS].astype(I32)
    nblk = (counts + BM - 1) // BM
    bend = jnp.cumsum(nblk)
    bstart = bend - nblk
    nvalid = bend[-1:]
    step = jnp.minimum(jnp.arange(NB_MAX, dtype=I32), nvalid - 1)
    bucket_ids = jnp.arange(N_BUCKETS, dtype=I32)
    blk_bucket = jnp.minimum(jnp.sum(step[:, None] >= bend[None, :], axis=1), N_BUCKETS - 1)
    expert_lo = jnp.array([g * EXPERTS_PER_GROUP + p for g in range(N_GROUPS) for p in PAIR_LO], I32)
    expert_hi = jnp.array([g * EXPERTS_PER_GROUP + p for g in range(N_GROUPS) for p in PAIR_HI], I32)
    in_bucket = blk_bucket[:, None] == bucket_ids[None, :]
    blk_lo = jnp.sum(jnp.where(in_bucket, expert_lo[None, :], 0), axis=1).astype(I32)
    blk_hi = jnp.sum(jnp.where(in_bucket, expert_hi[None, :], 0), axis=1).astype(I32)
    pad_start = bstart * BM + counts
    pad_len = nblk * BM - counts
    rec = jnp.concatenate([rec_p, rec_s], axis=0)
    first_slot = jnp.sum(jnp.where(rec[:, REC_BUCKET, :, None] == bucket_ids[None, None, :],
                                   (bstart * BM)[None, None, :], 0), axis=-1)
    slot_of = first_slot + rec[:, REC_RANK]
    d_prompt = slot_of[:NT_PROMPT].reshape(ND_PROMPT, 1, TD)
    d_sample = slot_of[NT_PROMPT:].reshape(1, 1, TL)
    d_tiles = slot_of[:, None, :]
    h_moves = h_p.reshape(ND_PROMPT, TD, D_MODEL)

    def move_map(i, *_):
        return (jnp.minimum(i, ND_PROMPT - 1), 0, 0)

    def hp_map(i, *_):
        return (jnp.minimum(i, NT_PROMPT - 1), 0, 0)

    def smem_tile(rows, index_map):
        return pl.BlockSpec((None, 1, rows), index_map, memory_space=pltpu.SMEM)

    xb = pl.pallas_call(
        _dispatch_kernel,
        grid_spec=pltpu.PrefetchScalarGridSpec(
            num_scalar_prefetch=3,
            grid=(ND_PROMPT + 1,),
            in_specs=[
                smem_tile(TD, move_map),
                smem_tile(TL, lambda i, *_: (0, 0, 0)),
                pl.BlockSpec((None, TD, D_MODEL), move_map),
                pl.BlockSpec((TL, D_MODEL), lambda i, *_: (0, 0)),
                pl.BlockSpec((1, D_MODEL), lambda i, *_: (0, 0)),
            ],
            out_specs=any_spec,
            scratch_shapes=[
                pltpu.VMEM((2, TD, SUBLANES, LANES), F32),
                pltpu.VMEM((ZERO_ROWS, SUBLANES, LANES), F32),
                pltpu.SemaphoreType.DMA((2,)),
            ],
        ),
        out_shape=jax.ShapeDtypeStruct((N_SLOTS, SUBLANES, LANES), F32),
        compiler_params=pltpu.CompilerParams(
            dimension_semantics=("arbitrary",), vmem_limit_bytes=VMEM_LIMIT),
        name="dispatch",
    )(pad_start, pad_len, nvalid, d_prompt, d_sample, h_moves, h_s, gffn)

    def xb_map(i, lo, hi, nv):
        return (jnp.minimum(i, nv[0] - 1), 0, 0)

    def lo_map(i, lo, hi, nv):
        return (lo[i], 0, 0)

    def hi_map(i, lo, hi, nv):
        return (hi[i], 0, 0)

    in_proj = (None, D_MODEL, D_EXPERT)
    out_proj = (None, D_EXPERT, D_MODEL)
    yb = pl.pallas_call(
        _experts_kernel,
        grid_spec=pltpu.PrefetchScalarGridSpec(
            num_scalar_prefetch=3,
            grid=(NB_MAX,),
            in_specs=[
                pl.BlockSpec((BM, SUBLANES, LANES), xb_map),
                pl.BlockSpec(in_proj, lo_map), pl.BlockSpec(in_proj, lo_map), pl.BlockSpec(out_proj, lo_map),
                pl.BlockSpec(in_proj, hi_map), pl.BlockSpec(in_proj, hi_map), pl.BlockSpec(out_proj, hi_map),
            ],
            out_specs=pl.BlockSpec((BM, SUBLANES, LANES), lambda i, lo, hi, nv: (i, 0, 0)),
            scratch_shapes=2 * [
                pltpu.VMEM((D_MODEL, D_EXPERT), BF16),
                pltpu.VMEM((D_MODEL, D_EXPERT), BF16),
                pltpu.VMEM((D_EXPERT, D_MODEL), BF16),
            ],
        ),
        out_shape=jax.ShapeDtypeStruct((N_SLOTS, SUBLANES, LANES), I32),
        compiler_params=pltpu.CompilerParams(
            dimension_semantics=("arbitrary",), vmem_limit_bytes=VMEM_LIMIT),
        name="experts",
    )(blk_lo, blk_hi, nvalid, xb, wg, wu, wd, wg, wu, wd)

    def next_tile(i):
        return (jnp.minimum(i + 1, NT_TOTAL - 1), 0, 0)

    y_p, y_s = pl.pallas_call(
        _combine_kernel,
        grid=(NT_TOTAL,),
        in_specs=[
            smem_tile(TL, lambda i: (i, 0, 0)),
            smem_tile(TL, next_tile),
            pl.BlockSpec((TL, LANES), lambda i: (jnp.minimum(i, NT_PROMPT - 1), 0)),
            pl.BlockSpec((TL, LANES), lambda i: (0, 0)),
            pl.BlockSpec((None, TL, D_MODEL), hp_map),
            pl.BlockSpec((TL, D_MODEL), lambda i: (0, 0)),
            pl.BlockSpec((1, D_MODEL), lambda i: (0, 0)),
            any_spec,
        ],
        out_specs=[
            pl.BlockSpec((None, TL, D_MODEL), hp_map),
            pl.BlockSpec((TL, D_MODEL), lambda i: (0, 0)),
        ],
        out_shape=[
            jax.ShapeDtypeStruct((NT_PROMPT, TL, D_MODEL), F32),
            jax.ShapeDtypeStruct((T_SAMPLE, D_MODEL), F32),
        ],
        scratch_shapes=[
            pltpu.VMEM((2, TL, SUBLANES, LANES), I32),
            pltpu.SemaphoreType.DMA((2,)),
        ],
        compiler_params=pltpu.CompilerParams(
            dimension_semantics=("arbitrary",), vmem_limit_bytes=VMEM_LIMIT),
        name="combine",
    )(d_tiles, d_tiles, pay_p, pay_s, h_p, h_s, gfin, yb)

    y_sample = y_s.reshape(DEC_BATCH, DEC_SEQ, D_MODEL)
    new_pool_prompt = pool_p[None, :, 1:, :]
    new_pool_sample = pool_s[None, :, 1:, :]
    new_gmlp_v_sample = vn_s.reshape(1, DEC_BATCH, DEC_SEQ, GMLP_WIDTH)
    y_prompt = y_p.reshape(BATCH, SEQ, D_MODEL)
    return (y_prompt, y_sample, new_pool_prompt, new_pool_sample, new_gmlp_v_sample)
```

```python
import jax
import jax.numpy as jnp
from jax import lax
from jax.experimental import pallas as pl
from jax.experimental.pallas import tpu as pltpu

F32 = jnp.float32
BF16 = jnp.bfloat16
I32 = jnp.int32

D_MODEL = 1024
SEQ = 16384
BATCH = 2
DEC_BATCH = 16
DEC_SEQ = 32
PAST_LEN = 1024
POOL_WIDTH = 512
POOL_GROUP_DIM = 128
POOL_WINDOWS = (2, 4, 8, 16)
HIST_ROWS = 16
GMLP_WIDTH = 512
GMLP_HEADS = 4
GMLP_HEAD_DIM = 128
GMLP_CHUNK = 128
IN_COLS = POOL_WIDTH + 2 * GMLP_WIDTH + 2 * D_MODEL
N_GROUPS = 4
EXPERTS_PER_GROUP = 4
N_EXPERTS = 16
D_EXPERT = 512
EPS = 1e-6

LANES = 128
SUBLANES = 8
assert D_MODEL == SUBLANES * LANES
TL = 512
BM = 512
T_PROMPT = BATCH * SEQ
T_SAMPLE = DEC_BATCH * DEC_SEQ
T_TOTAL = T_PROMPT + T_SAMPLE
NT_PROMPT = T_PROMPT // TL
TILES_PER_SEQ = SEQ // TL
NT_TOTAL = T_TOTAL // TL
assert T_SAMPLE == TL
PAIR_LO = (0, 0, 0, 1, 1, 2)
PAIR_HI = (1, 2, 3, 2, 3, 3)
N_PAIRS = len(PAIR_LO)
N_BUCKETS = N_GROUPS * N_PAIRS
NB_MIN = T_TOTAL // BM
NB_MAX = (T_TOTAL + N_BUCKETS * (BM - 1)) // BM
N_SLOTS = NB_MAX * BM
ZERO_ROWS = BM // 2
X_RING = 3
assert X_RING - 1 <= NB_MIN
TD = 4 * TL
ND_PROMPT = T_PROMPT // TD
DMA_UNROLL = 8
DMA_CHUNK = 64
SCATTER_CHUNK = 32
VMEM_LIMIT = 56 * 1024 * 1024

REC_BUCKET, REC_RANK = 0, 1
REC_ROWS = 8
PAY_LO, PAY_HI = 0, 1


def _dot(a, b):
    return jnp.dot(a, b, preferred_element_type=F32)


def _rms(x, g):
    return x * lax.rsqrt(jnp.mean(x * x, axis=-1, keepdims=True) + EPS) * g


def _split_bf16(x):
    hi = x.astype(BF16)
    lo = (x - hi.astype(F32)).astype(BF16)
    return hi, lo


def _pack_bf16_pair(lo, hi):
    lo_bits = lax.bitcast_convert_type(lo.astype(BF16).astype(F32), I32)
    hi_bits = lax.bitcast_convert_type(hi.astype(BF16).astype(F32), I32)
    return lax.shift_right_logical(lo_bits, jnp.int32(16)) | hi_bits


def _unpack_bf16_pair(packed):
    lo = lax.bitcast_convert_type(lax.shift_left(packed, jnp.int32(16)), F32)
    hi = lax.bitcast_convert_type(packed & jnp.int32(-65536), F32)
    return lo, hi


def _row_copy(src_ref, s, dst_ref, d, sem):
    return pltpu.make_async_copy(src_ref.at[pl.ds(s, 1)], dst_ref.at[pl.ds(d, 1)], sem)


def _pool_windows(load_shifted, a, pos, wpool_ref, pscale_ref):
    outs = []
    for g, w in enumerate(POOL_WINDOWS):
        cols = slice(g * POOL_GROUP_DIM, (g + 1) * POOL_GROUP_DIM)
        acc = a[:, cols]
        for k in range(1, w):
            acc = acc + load_shifted(k, g)
        cnt = jnp.minimum(pos + 1, w).astype(F32)
        d = (acc / cnt - a[:, cols]).astype(BF16)
        outs.append(_dot(d, wpool_ref[g]))
    return jnp.concatenate(outs, axis=-1) * pscale_ref[...]


def _merge(pa, pb, ga, gb, wupp_ref, wupg_ref):
    return (jax.nn.sigmoid(ga) * _dot(pa.astype(BF16), wupp_ref[...])
            + jax.nn.sigmoid(gb) * _dot(pb.astype(BF16), wupg_ref[...]))


def _init_ltri(ltri_ref):
    r = lax.broadcasted_iota(I32, (TL, TL), 0)
    c = lax.broadcasted_iota(I32, (TL, TL), 1)
    ltri_ref[...] = jnp.where(c < r, 1.0, 0.0).astype(BF16)


def _router_logits(hn, wrt_ref, brt_ref):
    hi, lo = _split_bf16(hn)
    whi, wlo = _split_bf16(wrt_ref[...])
    hi_w = _dot(hi, jnp.concatenate([whi, wlo], axis=1))
    return hi_w[:, 0:LANES] + hi_w[:, LANES:2 * LANES] + _dot(lo, whi) + brt_ref[...]


def _assign(lg, ltri_ref, run_ref, rec_out_ref, pay_out_ref, counted=True):
    lane = lax.broadcasted_iota(I32, lg.shape, 1)
    neg = jnp.float32(-jnp.inf)
    big = jnp.int32(1 << 20)

    def first_argmax(v, vmax):
        return jnp.min(jnp.where(v == vmax, lane, big), axis=-1, keepdims=True)

    gmask = lane < N_GROUPS
    glm = jnp.where(gmask, lg, neg)
    gmax = jnp.max(glm, axis=-1, keepdims=True)
    grp = first_argmax(glm, gmax)
    p_g = 1.0 / jnp.sum(jnp.where(gmask, jnp.exp(lg - gmax), 0.0), axis=-1, keepdims=True)

    lo_lane = N_GROUPS + EXPERTS_PER_GROUP * grp
    elm = jnp.where((lane >= lo_lane) & (lane < lo_lane + EXPERTS_PER_GROUP), lg, neg)
    v1 = jnp.max(elm, axis=-1, keepdims=True)
    i1 = first_argmax(elm, v1)
    elm2 = jnp.where(lane == i1, neg, elm)
    v2 = jnp.max(elm2, axis=-1, keepdims=True)
    i2 = first_argmax(elm2, v2)
    ex2 = jnp.exp(v2 - v1)
    inv = 1.0 / (1.0 + ex2)
    w_top = inv * p_g
    w_second = ex2 * inv * p_g

    loc1 = i1 - lo_lane
    loc2 = i2 - lo_lane
    top_is_lo = loc1 < loc2
    e_lo = jnp.minimum(loc1, loc2)
    e_hi = jnp.maximum(loc1, loc2)
    pair = jnp.where(e_lo == 0, 0, jnp.where(e_lo == 1, 3, 5)) + e_hi - e_lo - 1
    bucket = grp * N_PAIRS + pair
    w_lo = jnp.where(top_is_lo, w_top, w_second)
    w_hi = jnp.where(top_is_lo, w_second, w_top)

    sel = lane == bucket
    onehot = jnp.where(sel, 1.0, 0.0)
    before = _dot(ltri_ref[...], onehot.astype(BF16)) + run_ref[...]
    rank = jnp.sum(jnp.where(sel, before, 0.0), axis=-1, keepdims=True)
    run_ref[...] = run_ref[...] + jnp.where(counted, jnp.sum(onehot, axis=0, keepdims=True), 0.0)

    rec = jnp.where(lane == REC_BUCKET, bucket.astype(F32), jnp.where(lane == REC_RANK, rank, 0.0))
    rec_out_ref[...] = rec.T[0:REC_ROWS, :].astype(I32)
    pay_out_ref[...] = jnp.where(lane == PAY_LO, w_lo, jnp.where(lane == PAY_HI, w_hi, 0.0))


def _mixer_prompt_kernel(x_ref, xprev_ref, gmix_ref, win_ref, wpool_ref, pscale_ref, gv_ref, ws_ref,
                         bsp_ref, wupp_ref, wupg_ref, wout_ref, gffn_ref, wrt_ref, brt_ref,
                         h_ref, rec_out_ref, pay_out_ref, pool_ref, cnt_ref,
                         aext_ref, s_ref, ltri_ref, run_ref, merged_ref):
    s = pl.program_id(0)

    @pl.when(s == 0)
    def _():
        _init_ltri(ltri_ref)
        run_ref[...] = jnp.zeros_like(run_ref)
        aext_ref[TL:TL + HIST_ROWS, :] = jnp.zeros((HIST_ROWS, POOL_WIDTH), F32)
        merged_ref[...] = jnp.zeros_like(merged_ref)

    j = jnp.minimum(s, NT_PROMPT - 1) % TILES_PER_SEQ

    h = xprev_ref[...] + _dot(merged_ref[...], wout_ref[...])
    xn = _rms(x_ref[...], gmix_ref[...]).astype(BF16)
    a = _dot(xn, win_ref[:, 0:512])
    h_ref[...] = h
    hn = _rms(h, gffn_ref[...])
    lg = _router_logits(hn, wrt_ref, brt_ref)

    aext_ref[0:HIST_ROWS, :] = jnp.where(j > 0, aext_ref[TL:TL + HIST_ROWS, :], 0.0)
    aext_ref[HIST_ROWS:HIST_ROWS + TL, :] = a
    pool_ref[...] = a[TL - HIST_ROWS:TL, :]
    pos = j * TL + lax.broadcasted_iota(I32, (TL, 1), 0)

    def load_shifted(k, g):
        return aext_ref[HIST_ROWS - k:HIST_ROWS - k + TL, g * POOL_GROUP_DIM:(g + 1) * POOL_GROUP_DIM]

    v = _dot(xn, win_ref[:, 1024:1536])
    pa = _pool_windows(load_shifted, a, pos, wpool_ref, pscale_ref)
    u = _dot(xn, win_ref[:, 512:1024])
    _assign(lg, ltri_ref, run_ref, rec_out_ref, pay_out_ref, counted=s > 0)
    cnt_ref[...] = run_ref[...]
    ga = _dot(xn, win_ref[:, 1536:2560])

    vn = _rms(v, gv_ref[...]).astype(BF16)
    rr = lax.broadcasted_iota(I32, (GMLP_CHUNK, GMLP_CHUNK), 0)
    cc = lax.broadcasted_iota(I32, (GMLP_CHUNK, GMLP_CHUNK), 1)
    for hd in range(GMLP_HEADS):
        cols = slice(hd * GMLP_HEAD_DIM, (hd + 1) * GMLP_HEAD_DIM)
        wsm = jnp.where(cc <= rr, ws_ref[hd], 0.0).astype(BF16)
        chunks = [vn[c * GMLP_CHUNK:(c + 1) * GMLP_CHUNK, cols] for c in range(TL // GMLP_CHUNK)]
        mixed = _dot(wsm, jnp.concatenate(chunks, axis=1))
        for c in range(TL // GMLP_CHUNK):
            rows = slice(c * GMLP_CHUNK, (c + 1) * GMLP_CHUNK)
            s_ref[rows, cols] = (mixed[:, c * GMLP_HEAD_DIM:(c + 1) * GMLP_HEAD_DIM]
                                 + bsp_ref[rows, hd:hd + 1])
    pb = u * s_ref[...]
    gb = _dot(xn, win_ref[:, 2560:3584])

    merged_ref[...] = _merge(pa, pb, ga, gb, wupp_ref, wupg_ref).astype(BF16)


def _mixer_sample_kernel(x_ref, hist_ref, run0_ref, gmix_ref, win_ref, wpool_ref, pscale_ref, gv_ref,
                         ws_ref, bsp_ref, wupp_ref, wupg_ref, wout_ref, gffn_ref, wrt_ref, brt_ref,
                         h_ref, rec_out_ref, pay_out_ref, pool_ref, vn_ref, cnt_ref,
                         aext_ref, ltri_ref, run_ref):
    _init_ltri(ltri_ref)
    run_ref[...] = run0_ref[...]

    x = x_ref[...]
    xn = _rms(x, gmix_ref[...]).astype(BF16)
    a = _dot(xn, win_ref[:, 0:512])
    u = _dot(xn, win_ref[:, 512:1024])
    v = _dot(xn, win_ref[:, 1024:1536])
    ga = _dot(xn, win_ref[:, 1536:2560])
    gb = _dot(xn, win_ref[:, 2560:3584])

    a3 = a.reshape(DEC_BATCH, DEC_SEQ, POOL_WIDTH)
    aext_ref[:, 0:HIST_ROWS, :] = hist_ref[...]
    aext_ref[:, HIST_ROWS:HIST_ROWS + DEC_SEQ, :] = a3
    pool_ref[...] = a3[:, DEC_SEQ - HIST_ROWS:DEC_SEQ, :]
    row = lax.broadcasted_iota(I32, (TL, 1), 0)
    pos = PAST_LEN + row % DEC_SEQ

    def load_shifted(k, g):
        sl = aext_ref[:, HIST_ROWS - k:HIST_ROWS - k + DEC_SEQ, g * POOL_GROUP_DIM:(g + 1) * POOL_GROUP_DIM]
        return sl.reshape(TL, POOL_GROUP_DIM)

    pa = _pool_windows(load_shifted, a, pos, wpool_ref, pscale_ref)

    vnf = _rms(v, gv_ref[...])
    vn_ref[...] = vnf
    vn = vnf.astype(BF16)
    rsel = (lax.broadcasted_iota(I32, (TL, GMLP_CHUNK), 1)
            == lax.broadcasted_iota(I32, (TL, GMLP_CHUNK), 0) % DEC_SEQ)
    rsel_b = jnp.where(rsel, 1.0, 0.0).astype(BF16)
    csel = (lax.broadcasted_iota(I32, (GMLP_CHUNK, TL), 0)
            == lax.broadcasted_iota(I32, (GMLP_CHUNK, TL), 1) % DEC_SEQ)
    csel_b = jnp.where(csel, 1.0, 0.0).astype(BF16)
    rr = lax.broadcasted_iota(I32, (TL, TL), 0)
    cc = lax.broadcasted_iota(I32, (TL, TL), 1)
    keep = (rr // DEC_SEQ == cc // DEC_SEQ) & (cc <= rr)
    s_parts = []
    for hd in range(GMLP_HEADS):
        cols = slice(hd * GMLP_HEAD_DIM, (hd + 1) * GMLP_HEAD_DIM)
        wrow = _dot(rsel_b, ws_ref[hd].astype(BF16)).astype(BF16)
        wfull = _dot(wrow, csel_b)
        wblk = jnp.where(keep, wfull, 0.0).astype(BF16)
        s_parts.append(_dot(wblk, vn[:, cols]) + bsp_ref[:, hd:hd + 1])
    pb = u * jnp.concatenate(s_parts, axis=-1)

    h = x + _dot(_merge(pa, pb, ga, gb, wupp_ref, wupg_ref).astype(BF16), wout_ref[...])
    h_ref[...] = h
    lg = _router_logits(_rms(h, gffn_ref[...]), wrt_ref, brt_ref)
    _assign(lg, ltri_ref, run_ref, rec_out_ref, pay_out_ref)
    cnt_ref[...] = run_ref[...]


def _dispatch_kernel(pad_start_ref, pad_len_ref, nvalid_ref,
                     dp_ref, ds_ref, hp_ref, hs_ref, gffn_ref, xb_ref, row_ref, zero_ref, sems):
    i = pl.program_id(0)
    slot = i % 2
    rows = row_ref.at[slot]

    def fill_and_scatter(h_ref, d_ref, n_rows):
        for c in range(n_rows // SCATTER_CHUNK):
            r0 = c * SCATTER_CHUNK
            hn = _rms(h_ref[r0:r0 + SCATTER_CHUNK, :], gffn_ref[...])
            rows[r0:r0 + SCATTER_CHUNK] = hn.reshape(SCATTER_CHUNK, SUBLANES, LANES)
            for t in range(r0, r0 + SCATTER_CHUNK):
                _row_copy(rows, t, xb_ref, d_ref[0, t], sems.at[slot]).start(priority=t % 2)

    @pl.when(i < ND_PROMPT)
    def _():
        fill_and_scatter(hp_ref, dp_ref, TD)

    @pl.when(i == ND_PROMPT)
    def _():
        fill_and_scatter(hs_ref, ds_ref, TL)

    def drain(s, n_rows):
        def body(t, c):
            _row_copy(row_ref.at[s], t, xb_ref, 0, sems.at[s]).wait()
            return c
        lax.fori_loop(0, n_rows, body, 0, unroll=DMA_UNROLL)

    @pl.when(i > 0)
    def _():
        drain(1 - slot, TD)

    @pl.when(i == ND_PROMPT)
    def _():
        drain(slot, TL)
        zero_ref[...] = jnp.zeros_like(zero_ref)
        bits = [1 << k for k in reversed(range(BM.bit_length() - 1))]

        def pad_copies(fn):
            for e in range(N_BUCKETS):
                off = pad_start_ref[e]
                n = pad_len_ref[e]
                for bit in bits:
                    take = jnp.bitwise_and(n, bit)

                    @pl.when(take != 0)
                    def _(off=off, bit=bit):
                        fn(pltpu.make_async_copy(zero_ref.at[pl.ds(0, bit)], xb_ref.at[pl.ds(off, bit)],
                                                 sems.at[slot]))
                    off = off + take
            for blk in range(NB_MIN, NB_MAX):
                @pl.when(blk >= nvalid_ref[0])
                def _(blk=blk):
                    for half in range(BM // ZERO_ROWS):
                        fn(pltpu.make_async_copy(
                            zero_ref, xb_ref.at[pl.ds(blk * BM + half * ZERO_ROWS, ZERO_ROWS)], sems.at[slot]))

        pad_copies(lambda cp: cp.start())
        pad_copies(lambda cp: cp.wait())


def _experts_kernel(blk_lo_ref, blk_hi_ref, nvalid_ref, xb_ref,
                    wg_lo_ref, wu_lo_ref, wd_lo_ref, wg_hi_ref, wu_hi_ref, wd_hi_ref,
                    yb_ref, wgb_lo_ref, wub_lo_ref, wdb_lo_ref, wgb_hi_ref, wub_hi_ref, wdb_hi_ref,
                    xring_ref, xsems):
    i = pl.program_id(0)
    prev = jnp.maximum(i - 1, 0)
    n_valid = nvalid_ref[0]

    def fetch(b):
        return pltpu.make_async_copy(xb_ref.at[pl.ds(b * BM, BM)], xring_ref.at[b % X_RING], xsems.at[b % X_RING])

    @pl.when(i == 0)
    def _():
        for b in range(X_RING - 1):
            fetch(b).start()

    @pl.when(i + (X_RING - 1) < n_valid)
    def _():
        fetch(i + (X_RING - 1)).start()

    @pl.when((i == 0) | (blk_lo_ref[i] != blk_lo_ref[prev]))
    def _():
        wgb_lo_ref[...] = wg_lo_ref[...].astype(BF16)
        wub_lo_ref[...] = wu_lo_ref[...].astype(BF16)
        wdb_lo_ref[...] = wd_lo_ref[...].astype(BF16)

    @pl.when((i == 0) | (blk_hi_ref[i] != blk_hi_ref[prev]))
    def _():
        wgb_hi_ref[...] = wg_hi_ref[...].astype(BF16)
        wub_hi_ref[...] = wu_hi_ref[...].astype(BF16)
        wdb_hi_ref[...] = wd_hi_ref[...].astype(BF16)

    @pl.when(i < n_valid)
    def _():
        fetch(i).wait()
        x = xring_ref[i % X_RING].reshape(BM, D_MODEL).astype(BF16)

        def hidden(wgb, wub):
            hg = _dot(x, wgb[...])
            return ((hg * jax.nn.sigmoid(hg)) * _dot(x, wub[...])).astype(BF16)

        act_lo = hidden(wgb_lo_ref, wub_lo_ref)
        act_hi = hidden(wgb_hi_ref, wub_hi_ref)
        y_lo = _dot(act_lo, wdb_lo_ref[...])
        y_hi = _dot(act_hi, wdb_hi_ref[...])
        yb_ref[...] = _pack_bf16_pair(y_lo, y_hi).reshape(BM, SUBLANES, LANES)

    @pl.when(i >= n_valid)
    def _():
        yb_ref[...] = jnp.zeros_like(yb_ref)


def _combine_kernel(d_ref, dn_ref, payp_ref, pays_ref, hp_ref, hs_ref, gfin_ref, yb_ref,
                    yp_ref, ys_ref, buf_ref, sems):
    i = pl.program_id(0)
    slot = i % 2

    @pl.when(i == 0)
    def _():
        def body(t, c):
            _row_copy(yb_ref, d_ref[0, t], buf_ref.at[slot], t, sems.at[slot]).start()
            return c
        lax.fori_loop(0, TL, body, 0, unroll=DMA_UNROLL)

    def drain(t, c):
        _row_copy(yb_ref, 0, buf_ref.at[slot], t, sems.at[slot]).wait()
        return c

    lax.fori_loop(0, TL, drain, 0, unroll=DMA_UNROLL)

    def finish_chunk(h_ref, pay_ref, out_ref, r0):
        rows = slice(r0, r0 + DMA_CHUNK)
        y_lo, y_hi = _unpack_bf16_pair(buf_ref.at[slot][rows].reshape(DMA_CHUNK, D_MODEL))
        pay = pay_ref[rows, :]
        moe = y_lo * pay[:, PAY_LO:PAY_LO + 1] + y_hi * pay[:, PAY_HI:PAY_HI + 1]
        out_ref[rows, :] = _rms(h_ref[rows, :] + moe, gfin_ref[...])

    @pl.when(i < NT_PROMPT)
    def _():
        for c in range(TL // DMA_CHUNK):
            r0 = c * DMA_CHUNK
            finish_chunk(hp_ref, payp_ref, yp_ref, r0)
            for t in range(r0, r0 + DMA_CHUNK):
                _row_copy(yb_ref, dn_ref[0, t], buf_ref.at[1 - slot], t, sems.at[1 - slot]).start(priority=t % 2)

    @pl.when(i == NT_PROMPT)
    def _():
        for c in range(TL // DMA_CHUNK):
            finish_chunk(hs_ref, pays_ref, ys_ref, c * DMA_CHUNK)


def _const_spec(shape):
    zeros = (0,) * len(shape)
    return pl.BlockSpec(shape, lambda *_: zeros, pipeline_mode=pl.Buffered(1))


def kernel(x_prompt, x_sample, state_pool, g_mix, w_in, w_pool, pool_scale, g_v, w_spatial, b_spatial,
           w_up_pool, w_up_gmlp, w_out, g_ffn, w_group, b_group, w_router, b_router, w_gate, w_up,
           w_down, g_final):
    gmix = g_mix[0][None, :]
    win = w_in[0].astype(BF16)
    wpool = w_pool[0].astype(BF16)
    pscale = pool_scale[0][None, :]
    gv = g_v[0][None, :]
    ws = w_spatial[0]
    bsp_t = b_spatial[0].T
    wupp = w_up_pool[0].astype(BF16)
    wupg = w_up_gmlp[0].astype(BF16)
    wout = w_out[0].astype(BF16)
    gffn = g_ffn[0][None, :]
    n_route = N_GROUPS + N_EXPERTS
    wrt = jnp.pad(jnp.concatenate([w_group[0], w_router[0]], axis=1), ((0, 0), (0, LANES - n_route)))
    brt = jnp.pad(jnp.concatenate([b_group[0], b_router[0]]), (0, LANES - n_route))[None, :]
    wg = w_gate[0]
    wu = w_up[0]
    wd = w_down[0]
    gfin = g_final[None, :]
    bsp_prompt = jnp.tile(bsp_t, (TL // GMLP_CHUNK, 1))
    bsp_sample = jnp.tile(bsp_t[:DEC_SEQ], (DEC_BATCH, 1))
    hist = jnp.pad(state_pool[0], ((0, 0), (1, 0), (0, 0)))

    weight_specs = [
        _const_spec((1, D_MODEL)),
        _const_spec((D_MODEL, IN_COLS)),
        _const_spec((4, POOL_GROUP_DIM, POOL_GROUP_DIM)),
        _const_spec((1, POOL_WIDTH)),
        _const_spec((1, GMLP_WIDTH)),
        _const_spec((GMLP_HEADS, GMLP_CHUNK, GMLP_CHUNK)),
        _const_spec((TL, GMLP_HEADS)),
        _const_spec((POOL_WIDTH, D_MODEL)),
        _const_spec((GMLP_WIDTH, D_MODEL)),
        _const_spec((D_MODEL, D_MODEL)),
        _const_spec((1, D_MODEL)),
        _const_spec((D_MODEL, LANES)),
        _const_spec((1, LANES)),
    ]
    any_spec = pl.BlockSpec(memory_space=pl.ANY)
    route_scratch = [
        pltpu.VMEM((TL, TL), BF16),
        pltpu.VMEM((1, LANES), F32),
    ]

    def started(s):
        return jnp.minimum(s, NT_PROMPT - 1)

    def finished(s):
        return jnp.maximum(s - 1, 0)

    x_tiles = x_prompt.reshape(NT_PROMPT, TL, D_MODEL)
    h_p, rec_p, pay_p, pool_p, cnt_p = pl.pallas_call(
        _mixer_prompt_kernel,
        grid=(NT_PROMPT + 1,),
        in_specs=[
            pl.BlockSpec((None, TL, D_MODEL), lambda s: (started(s), 0, 0)),
            pl.BlockSpec((None, TL, D_MODEL), lambda s: (finished(s), 0, 0)),
        ] + weight_specs,
        out_specs=[
            pl.BlockSpec((None, TL, D_MODEL), lambda s: (finished(s), 0, 0)),
            pl.BlockSpec((None, REC_ROWS, TL), lambda s: (finished(s), 0, 0)),
            pl.BlockSpec((TL, LANES), lambda s: (finished(s), 0)),
            pl.BlockSpec((None, HIST_ROWS, POOL_WIDTH), lambda s: (started(s) // TILES_PER_SEQ, 0, 0)),
            pl.BlockSpec((1, LANES), lambda s: (0, 0)),
        ],
        out_shape=[
            jax.ShapeDtypeStruct((NT_PROMPT, TL, D_MODEL), F32),
            jax.ShapeDtypeStruct((NT_PROMPT, REC_ROWS, TL), I32),
            jax.ShapeDtypeStruct((T_PROMPT, LANES), F32),
            jax.ShapeDtypeStruct((BATCH, HIST_ROWS, POOL_WIDTH), F32),
            jax.ShapeDtypeStruct((1, LANES), F32),
        ],
        scratch_shapes=[
            pltpu.VMEM((HIST_ROWS + TL, POOL_WIDTH), F32),
            pltpu.VMEM((TL, GMLP_WIDTH), F32),
        ] + route_scratch + [pltpu.VMEM((TL, D_MODEL), BF16)],
        compiler_params=pltpu.CompilerParams(
            dimension_semantics=("arbitrary",), vmem_limit_bytes=VMEM_LIMIT),
        name="mixer_prompt",
    )(x_tiles, x_tiles, gmix, win, wpool, pscale, gv, ws, bsp_prompt, wupp, wupg, wout, gffn, wrt, brt)

    h_s, rec_s, pay_s, pool_s, vn_s, cnt = pl.pallas_call(
        _mixer_sample_kernel,
        grid=(1,),
        in_specs=[
            _const_spec((TL, D_MODEL)),
            _const_spec((DEC_BATCH, HIST_ROWS, POOL_WIDTH)),
            _const_spec((1, LANES)),
        ] + weight_specs,
        out_specs=[
            _const_spec((TL, D_MODEL)),
            pl.BlockSpec((None, REC_ROWS, TL), lambda i: (0, 0, 0)),
            _const_spec((TL, LANES)),
            _const_spec((DEC_BATCH, HIST_ROWS, POOL_WIDTH)),
            _const_spec((TL, GMLP_WIDTH)),
            _const_spec((1, LANES)),
        ],
        out_shape=[
            jax.ShapeDtypeStruct((T_SAMPLE, D_MODEL), F32),
            jax.ShapeDtypeStruct((1, REC_ROWS, TL), I32),
            jax.ShapeDtypeStruct((T_SAMPLE, LANES), F32),
            jax.ShapeDtypeStruct((DEC_BATCH, HIST_ROWS, POOL_WIDTH), F32),
            jax.ShapeDtypeStruct((T_SAMPLE, GMLP_WIDTH), F32),
            jax.ShapeDtypeStruct((1, LANES), F32),
        ],
        scratch_shapes=[
            pltpu.VMEM((DEC_BATCH, HIST_ROWS + DEC_SEQ, POOL_WIDTH), F32),
        ] + route_scratch,
        compiler_params=pltpu.CompilerParams(
            dimension_semantics=("arbitrary",), vmem_limit_bytes=VMEM_LIMIT),
        name="mixer_sample",
    )(x_sample.reshape(T_SAMPLE, D_MODEL), hist, cnt_p, gmix, win, wpool, pscale, gv, ws, bsp_sample,
      wupp, wupg, wout, gffn, wrt, brt)

    counts = cnt[0, :N_BUCKETS].astype(I32)
    nblk = (counts + BM - 1) // BM
    bend = jnp.cumsum(nblk)
    bstart = bend - nblk
    nvalid = bend[-1:]
    step = jnp.minimum(jnp.arange(NB_MAX, dtype=I32), nvalid - 1)
    bucket_ids = jnp.arange(N_BUCKETS, dtype=I32)
    blk_bucket = jnp.minimum(jnp.sum(step[:, None] >= bend[None, :], axis=1), N_BUCKETS - 1)
    expert_lo = jnp.array([g * EXPERTS_PER_GROUP + p for g in range(N_GROUPS) for p in PAIR_LO], I32)
    expert_hi = jnp.array([g * EXPERTS_PER_GROUP + p for g in range(N_GROUPS) for p in PAIR_HI], I32)
    in_bucket = blk_bucket[:, None] == bucket_ids[None, :]
    blk_lo = jnp.sum(jnp.where(in_bucket, expert_lo[None, :], 0), axis=1).astype(I32)
    blk_hi = jnp.sum(jnp.where(in_bucket, expert_hi[None, :], 0), axis=1).astype(I32)
    pad_start = bstart * BM + counts
    pad_len = nblk * BM - counts
    rec = jnp.concatenate([rec_p, rec_s], axis=0)
    first_slot = jnp.sum(jnp.where(rec[:, REC_BUCKET, :, None] == bucket_ids[None, None, :],
                                   (bstart * BM)[None, None, :], 0), axis=-1)
    slot_of = first_slot + rec[:, REC_RANK]
    d_prompt = slot_of[:NT_PROMPT].reshape(ND_PROMPT, 1, TD)
    d_sample = slot_of[NT_PROMPT:].reshape(1, 1, TL)
    d_tiles = slot_of[:, None, :]
    h_moves = h_p.reshape(ND_PROMPT, TD, D_MODEL)

    def move_map(i, *_):
        return (jnp.minimum(i, ND_PROMPT - 1), 0, 0)

    def hp_map(i, *_):
        return (jnp.minimum(i, NT_PROMPT - 1), 0, 0)

    def smem_tile(rows, index_map):
        return pl.BlockSpec((None, 1, rows), index_map, memory_space=pltpu.SMEM)

    xb = pl.pallas_call(
        _dispatch_kernel,
        grid_spec=pltpu.PrefetchScalarGridSpec(
            num_scalar_prefetch=3,
            grid=(ND_PROMPT + 1,),
            in_specs=[
                smem_tile(TD, move_map),
                smem_tile(TL, lambda i, *_: (0, 0, 0)),
                pl.BlockSpec((None, TD, D_MODEL), move_map),
                pl.BlockSpec((TL, D_MODEL), lambda i, *_: (0, 0)),
                pl.BlockSpec((1, D_MODEL), lambda i, *_: (0, 0)),
            ],
            out_specs=any_spec,
            scratch_shapes=[
                pltpu.VMEM((2, TD, SUBLANES, LANES), F32),
                pltpu.VMEM((ZERO_ROWS, SUBLANES, LANES), F32),
                pltpu.SemaphoreType.DMA((2,)),
            ],
        ),
        out_shape=jax.ShapeDtypeStruct((N_SLOTS, SUBLANES, LANES), F32),
        compiler_params=pltpu.CompilerParams(
            dimension_semantics=("arbitrary",), vmem_limit_bytes=VMEM_LIMIT),
        name="dispatch",
    )(pad_start, pad_len, nvalid, d_prompt, d_sample, h_moves, h_s, gffn)

    def lo_map(i, lo, hi, nv):
        return (lo[i], 0, 0)

    def hi_map(i, lo, hi, nv):
        return (hi[i], 0, 0)

    in_proj = (None, D_MODEL, D_EXPERT)
    out_proj = (None, D_EXPERT, D_MODEL)
    yb = pl.pallas_call(
        _experts_kernel,
        grid_spec=pltpu.PrefetchScalarGridSpec(
            num_scalar_prefetch=3,
            grid=(NB_MAX,),
            in_specs=[
                any_spec,
                pl.BlockSpec(in_proj, lo_map), pl.BlockSpec(in_proj, lo_map), pl.BlockSpec(out_proj, lo_map),
                pl.BlockSpec(in_proj, hi_map), pl.BlockSpec(in_proj, hi_map), pl.BlockSpec(out_proj, hi_map),
            ],
            out_specs=pl.BlockSpec((BM, SUBLANES, LANES), lambda i, lo, hi, nv: (i, 0, 0)),
            scratch_shapes=2 * [
                pltpu.VMEM((D_MODEL, D_EXPERT), BF16),
                pltpu.VMEM((D_MODEL, D_EXPERT), BF16),
                pltpu.VMEM((D_EXPERT, D_MODEL), BF16),
            ] + [
                pltpu.VMEM((X_RING, BM, SUBLANES, LANES), F32),
                pltpu.SemaphoreType.DMA((X_RING,)),
            ],
        ),
        out_shape=jax.ShapeDtypeStruct((N_SLOTS, SUBLANES, LANES), I32),
        compiler_params=pltpu.CompilerParams(
            dimension_semantics=("arbitrary",), vmem_limit_bytes=VMEM_LIMIT),
        name="experts",
    )(blk_lo, blk_hi, nvalid, xb, wg, wu, wd, wg, wu, wd)

    def next_tile(i):
        return (jnp.minimum(i + 1, NT_TOTAL - 1), 0, 0)

    y_p, y_s = pl.pallas_call(
        _combine_kernel,
        grid=(NT_TOTAL,),
        in_specs=[
            smem_tile(TL, lambda i: (i, 0, 0)),
            smem_tile(TL, next_tile),
            pl.BlockSpec((TL, LANES), lambda i: (jnp.minimum(i, NT_PROMPT - 1), 0)),
            pl.BlockSpec((TL, LANES), lambda i: (0, 0)),
            pl.BlockSpec((None, TL, D_MODEL), hp_map),
            pl.BlockSpec((TL, D_MODEL), lambda i: (0, 0)),
            pl.BlockSpec((1, D_MODEL), lambda i: (0, 0)),
            any_spec,
        ],
        out_specs=[
            pl.BlockSpec((None, TL, D_MODEL), hp_map),
            pl.BlockSpec((TL, D_MODEL), lambda i: (0, 0)),
        ],
        out_shape=[
            jax.ShapeDtypeStruct((NT_PROMPT, TL, D_MODEL), F32),
            jax.ShapeDtypeStruct((T_SAMPLE, D_MODEL), F32),
        ],
        scratch_shapes=[
            pltpu.VMEM((2, TL, SUBLANES, LANES), I32),
            pltpu.SemaphoreType.DMA((2,)),
        ],
        compiler_params=pltpu.CompilerParams(
            dimension_semantics=("arbitrary",), vmem_limit_bytes=VMEM_LIMIT),
        name="combine",
    )(d_tiles, d_tiles, pay_p, pay_s, h_p, h_s, gfin, yb)

    y_sample = y_s.reshape(DEC_BATCH, DEC_SEQ, D_MODEL)
    new_pool_prompt = pool_p[None, :, 1:, :]
    new_pool_sample = pool_s[None, :, 1:, :]
    new_gmlp_v_sample = vn_s.reshape(1, DEC_BATCH, DEC_SEQ, GMLP_WIDTH)
    y_prompt = y_p.reshape(BATCH, SEQ, D_MODEL)
    return (y_prompt, y_sample, new_pool_prompt, new_pool_sample, new_gmlp_v_sample)
```

```python
import jax
import jax.numpy as jnp
from jax import lax
from jax.experimental import pallas as pl
from jax.experimental.pallas import tpu as pltpu

F32 = jnp.float32
BF16 = jnp.bfloat16
I32 = jnp.int32

D_MODEL = 1024
SEQ = 16384
BATCH = 2
DEC_BATCH = 16
DEC_SEQ = 32
PAST_LEN = 1024
POOL_WIDTH = 512
POOL_GROUP_DIM = 128
POOL_WINDOWS = (2, 4, 8, 16)
HIST_ROWS = 16
GMLP_WIDTH = 512
GMLP_HEADS = 4
GMLP_HEAD_DIM = 128
GMLP_CHUNK = 128
IN_COLS = POOL_WIDTH + 2 * GMLP_WIDTH + 2 * D_MODEL
N_GROUPS = 4
EXPERTS_PER_GROUP = 4
N_EXPERTS = 16
D_EXPERT = 512
EPS = 1e-6

LANES = 128
SUBLANES = 8
assert D_MODEL == SUBLANES * LANES
TL = 512
BM = 512
T_PROMPT = BATCH * SEQ
T_SAMPLE = DEC_BATCH * DEC_SEQ
T_TOTAL = T_PROMPT + T_SAMPLE
NT_PROMPT = T_PROMPT // TL
TILES_PER_SEQ = SEQ // TL
NT_TOTAL = T_TOTAL // TL
assert T_SAMPLE == TL
PAIR_LO = (0, 0, 1, 1, 0, 2)
PAIR_HI = (1, 2, 2, 3, 3, 3)
N_PAIRS = len(PAIR_LO)
N_BUCKETS = N_GROUPS * N_PAIRS
NB_MIN = T_TOTAL // BM
NB_MAX = (T_TOTAL + N_BUCKETS * (BM - 1)) // BM
N_SLOTS = NB_MAX * BM
ZERO_ROWS = BM // 2
X_RING = 3
assert X_RING - 1 <= NB_MIN
TD = 4 * TL
ND_PROMPT = T_PROMPT // TD
DMA_UNROLL = 8
DMA_CHUNK = 64
SCATTER_CHUNK = 32
VMEM_LIMIT = 56 * 1024 * 1024

REC_BUCKET, REC_RANK = 0, 1
REC_ROWS = 8
PAY_LO, PAY_HI = 0, 1


def _dot(a, b):
    return jnp.dot(a, b, preferred_element_type=F32)


def _rms(x, g):
    return x * lax.rsqrt(jnp.mean(x * x, axis=-1, keepdims=True) + EPS) * g


def _split_bf16(x):
    hi = x.astype(BF16)
    lo = (x - hi.astype(F32)).astype(BF16)
    return hi, lo


def _pack_bf16_pair(lo, hi):
    lo_bits = lax.bitcast_convert_type(lo.astype(BF16).astype(F32), I32)
    hi_bits = lax.bitcast_convert_type(hi.astype(BF16).astype(F32), I32)
    return lax.shift_right_logical(lo_bits, jnp.int32(16)) | hi_bits


def _unpack_bf16_pair(packed):
    lo = lax.bitcast_convert_type(lax.shift_left(packed, jnp.int32(16)), F32)
    hi = lax.bitcast_convert_type(packed & jnp.int32(-65536), F32)
    return lo, hi


def _row_copy(src_ref, s, dst_ref, d, sem):
    return pltpu.make_async_copy(src_ref.at[pl.ds(s, 1)], dst_ref.at[pl.ds(d, 1)], sem)


def _pool_windows(load_shifted, a, pos, wpool_ref, pscale_ref):
    outs = []
    for g, w in enumerate(POOL_WINDOWS):
        cols = slice(g * POOL_GROUP_DIM, (g + 1) * POOL_GROUP_DIM)
        acc = a[:, cols]
        for k in range(1, w):
            acc = acc + load_shifted(k, g)
        cnt = jnp.minimum(pos + 1, w).astype(F32)
        d = (acc / cnt - a[:, cols]).astype(BF16)
        outs.append(_dot(d, wpool_ref[g]))
    return jnp.concatenate(outs, axis=-1) * pscale_ref[...]


def _merge(pa, pb, ga, gb, wupp_ref, wupg_ref):
    return (jax.nn.sigmoid(ga) * _dot(pa.astype(BF16), wupp_ref[...])
            + jax.nn.sigmoid(gb) * _dot(pb.astype(BF16), wupg_ref[...]))


def _init_ltri(ltri_ref):
    r = lax.broadcasted_iota(I32, (TL, TL), 0)
    c = lax.broadcasted_iota(I32, (TL, TL), 1)
    ltri_ref[...] = jnp.where(c < r, 1.0, 0.0).astype(BF16)


def _router_logits(hn, wrt_ref, brt_ref):
    hi, lo = _split_bf16(hn)
    whi, wlo = _split_bf16(wrt_ref[...])
    hi_w = _dot(hi, jnp.concatenate([whi, wlo], axis=1))
    return hi_w[:, 0:LANES] + hi_w[:, LANES:2 * LANES] + _dot(lo, whi) + brt_ref[...]


def _assign(lg, ltri_ref, run_ref, rec_out_ref, pay_out_ref, counted=True):
    lane = lax.broadcasted_iota(I32, lg.shape, 1)
    neg = jnp.float32(-jnp.inf)
    big = jnp.int32(1 << 20)

    def first_argmax(v, vmax):
        return jnp.min(jnp.where(v == vmax, lane, big), axis=-1, keepdims=True)

    gmask = lane < N_GROUPS
    glm = jnp.where(gmask, lg, neg)
    gmax = jnp.max(glm, axis=-1, keepdims=True)
    grp = first_argmax(glm, gmax)
    p_g = 1.0 / jnp.sum(jnp.where(gmask, jnp.exp(lg - gmax), 0.0), axis=-1, keepdims=True)

    lo_lane = N_GROUPS + EXPERTS_PER_GROUP * grp
    elm = jnp.where((lane >= lo_lane) & (lane < lo_lane + EXPERTS_PER_GROUP), lg, neg)
    v1 = jnp.max(elm, axis=-1, keepdims=True)
    i1 = first_argmax(elm, v1)
    elm2 = jnp.where(lane == i1, neg, elm)
    v2 = jnp.max(elm2, axis=-1, keepdims=True)
    i2 = first_argmax(elm2, v2)
    ex2 = jnp.exp(v2 - v1)
    inv = 1.0 / (1.0 + ex2)
    w_top = inv * p_g
    w_second = ex2 * inv * p_g

    loc1 = i1 - lo_lane
    loc2 = i2 - lo_lane
    top_is_lo = loc1 < loc2
    e_lo = jnp.minimum(loc1, loc2)
    e_hi = jnp.maximum(loc1, loc2)
    pair = jnp.zeros_like(e_lo)
    for index, (lo_k, hi_k) in enumerate(zip(PAIR_LO, PAIR_HI)):
        pair = jnp.where((e_lo == lo_k) & (e_hi == hi_k), index, pair)
    bucket = grp * N_PAIRS + pair
    w_lo = jnp.where(top_is_lo, w_top, w_second)
    w_hi = jnp.where(top_is_lo, w_second, w_top)

    sel = lane == bucket
    onehot = jnp.where(sel, 1.0, 0.0)
    before = _dot(ltri_ref[...], onehot.astype(BF16)) + run_ref[...]
    rank = jnp.sum(jnp.where(sel, before, 0.0), axis=-1, keepdims=True)
    run_ref[...] = run_ref[...] + jnp.where(counted, jnp.sum(onehot, axis=0, keepdims=True), 0.0)

    rec = jnp.where(lane == REC_BUCKET, bucket.astype(F32), jnp.where(lane == REC_RANK, rank, 0.0))
    rec_out_ref[...] = rec.T[0:REC_ROWS, :].astype(I32)
    pay_out_ref[...] = jnp.where(lane == PAY_LO, w_lo, jnp.where(lane == PAY_HI, w_hi, 0.0))


def _mixer_prompt_kernel(x_ref, xprev_ref, gmix_ref, win_ref, wpool_ref, pscale_ref, gv_ref, ws_ref,
                         bsp_ref, wupp_ref, wupg_ref, wout_ref, gffn_ref, wrt_ref, brt_ref,
                         h_ref, rec_out_ref, pay_out_ref, pool_ref, cnt_ref,
                         aext_ref, s_ref, ltri_ref, run_ref, merged_ref):
    s = pl.program_id(0)

    @pl.when(s == 0)
    def _():
        _init_ltri(ltri_ref)
        run_ref[...] = jnp.zeros_like(run_ref)
        aext_ref[TL:TL + HIST_ROWS, :] = jnp.zeros((HIST_ROWS, POOL_WIDTH), F32)
        merged_ref[...] = jnp.zeros_like(merged_ref)

    j = jnp.minimum(s, NT_PROMPT - 1) % TILES_PER_SEQ

    h = xprev_ref[...] + _dot(merged_ref[...], wout_ref[...])
    xn = _rms(x_ref[...], gmix_ref[...]).astype(BF16)
    a = _dot(xn, win_ref[:, 0:512])
    h_ref[...] = h
    hn = _rms(h, gffn_ref[...])
    lg = _router_logits(hn, wrt_ref, brt_ref)

    aext_ref[0:HIST_ROWS, :] = jnp.where(j > 0, aext_ref[TL:TL + HIST_ROWS, :], 0.0)
    aext_ref[HIST_ROWS:HIST_ROWS + TL, :] = a
    pool_ref[...] = a[TL - HIST_ROWS:TL, :]
    pos = j * TL + lax.broadcasted_iota(I32, (TL, 1), 0)

    def load_shifted(k, g):
        return aext_ref[HIST_ROWS - k:HIST_ROWS - k + TL, g * POOL_GROUP_DIM:(g + 1) * POOL_GROUP_DIM]

    v = _dot(xn, win_ref[:, 1024:1536])
    pa = _pool_windows(load_shifted, a, pos, wpool_ref, pscale_ref)
    u = _dot(xn, win_ref[:, 512:1024])
    _assign(lg, ltri_ref, run_ref, rec_out_ref, pay_out_ref, counted=s > 0)
    cnt_ref[...] = run_ref[...]
    ga = _dot(xn, win_ref[:, 1536:2560])

    vn = _rms(v, gv_ref[...]).astype(BF16)
    rr = lax.broadcasted_iota(I32, (GMLP_CHUNK, GMLP_CHUNK), 0)
    cc = lax.broadcasted_iota(I32, (GMLP_CHUNK, GMLP_CHUNK), 1)
    for hd in range(GMLP_HEADS):
        cols = slice(hd * GMLP_HEAD_DIM, (hd + 1) * GMLP_HEAD_DIM)
        wsm = jnp.where(cc <= rr, ws_ref[hd], 0.0).astype(BF16)
        chunks = [vn[c * GMLP_CHUNK:(c + 1) * GMLP_CHUNK, cols] for c in range(TL // GMLP_CHUNK)]
        mixed = _dot(wsm, jnp.concatenate(chunks, axis=1))
        for c in range(TL // GMLP_CHUNK):
            rows = slice(c * GMLP_CHUNK, (c + 1) * GMLP_CHUNK)
            s_ref[rows, cols] = (mixed[:, c * GMLP_HEAD_DIM:(c + 1) * GMLP_HEAD_DIM]
                                 + bsp_ref[rows, hd:hd + 1])
    pb = u * s_ref[...]
    gb = _dot(xn, win_ref[:, 2560:3584])

    merged_ref[...] = _merge(pa, pb, ga, gb, wupp_ref, wupg_ref).astype(BF16)


def _mixer_sample_kernel(x_ref, hist_ref, run0_ref, gmix_ref, win_ref, wpool_ref, pscale_ref, gv_ref,
                         ws_ref, bsp_ref, wupp_ref, wupg_ref, wout_ref, gffn_ref, wrt_ref, brt_ref,
                         h_ref, rec_out_ref, pay_out_ref, pool_ref, vn_ref, cnt_ref,
                         aext_ref, ltri_ref, run_ref):
    _init_ltri(ltri_ref)
    run_ref[...] = run0_ref[...]

    x = x_ref[...]
    xn = _rms(x, gmix_ref[...]).astype(BF16)
    a = _dot(xn, win_ref[:, 0:512])
    u = _dot(xn, win_ref[:, 512:1024])
    v = _dot(xn, win_ref[:, 1024:1536])
    ga = _dot(xn, win_ref[:, 1536:2560])
    gb = _dot(xn, win_ref[:, 2560:3584])

    a3 = a.reshape(DEC_BATCH, DEC_SEQ, POOL_WIDTH)
    aext_ref[:, 0:HIST_ROWS, :] = hist_ref[...]
    aext_ref[:, HIST_ROWS:HIST_ROWS + DEC_SEQ, :] = a3
    pool_ref[...] = a3[:, DEC_SEQ - HIST_ROWS:DEC_SEQ, :]
    row = lax.broadcasted_iota(I32, (TL, 1), 0)
    pos = PAST_LEN + row % DEC_SEQ

    def load_shifted(k, g):
        sl = aext_ref[:, HIST_ROWS - k:HIST_ROWS - k + DEC_SEQ, g * POOL_GROUP_DIM:(g + 1) * POOL_GROUP_DIM]
        return sl.reshape(TL, POOL_GROUP_DIM)

    pa = _pool_windows(load_shifted, a, pos, wpool_ref, pscale_ref)

    vnf = _rms(v, gv_ref[...])
    vn_ref[...] = vnf
    vn = vnf.astype(BF16)
    rsel = (lax.broadcasted_iota(I32, (TL, GMLP_CHUNK), 1)
            == lax.broadcasted_iota(I32, (TL, GMLP_CHUNK), 0) % DEC_SEQ)
    rsel_b = jnp.where(rsel, 1.0, 0.0).astype(BF16)
    csel = (lax.broadcasted_iota(I32, (GMLP_CHUNK, TL), 0)
            == lax.broadcasted_iota(I32, (GMLP_CHUNK, TL), 1) % DEC_SEQ)
    csel_b = jnp.where(csel, 1.0, 0.0).astype(BF16)
    rr = lax.broadcasted_iota(I32, (TL, TL), 0)
    cc = lax.broadcasted_iota(I32, (TL, TL), 1)
    keep = (rr // DEC_SEQ == cc // DEC_SEQ) & (cc <= rr)
    s_parts = []
    for hd in range(GMLP_HEADS):
        cols = slice(hd * GMLP_HEAD_DIM, (hd + 1) * GMLP_HEAD_DIM)
        wrow = _dot(rsel_b, ws_ref[hd].astype(BF16)).astype(BF16)
        wfull = _dot(wrow, csel_b)
        wblk = jnp.where(keep, wfull, 0.0).astype(BF16)
        s_parts.append(_dot(wblk, vn[:, cols]) + bsp_ref[:, hd:hd + 1])
    pb = u * jnp.concatenate(s_parts, axis=-1)

    h = x + _dot(_merge(pa, pb, ga, gb, wupp_ref, wupg_ref).astype(BF16), wout_ref[...])
    h_ref[...] = h
    lg = _router_logits(_rms(h, gffn_ref[...]), wrt_ref, brt_ref)
    _assign(lg, ltri_ref, run_ref, rec_out_ref, pay_out_ref)
    cnt_ref[...] = run_ref[...]


def _dispatch_kernel(pad_start_ref, pad_len_ref, nvalid_ref,
                     dp_ref, ds_ref, hp_ref, hs_ref, gffn_ref, xb_ref, row_ref, zero_ref, sems):
    i = pl.program_id(0)
    slot = i % 2
    rows = row_ref.at[slot]

    def fill_and_scatter(h_ref, d_ref, n_rows):
        for c in range(n_rows // SCATTER_CHUNK):
            r0 = c * SCATTER_CHUNK
            hn = _rms(h_ref[r0:r0 + SCATTER_CHUNK, :], gffn_ref[...])
            rows[r0:r0 + SCATTER_CHUNK] = hn.reshape(SCATTER_CHUNK, SUBLANES, LANES)
            for t in range(r0, r0 + SCATTER_CHUNK):
                _row_copy(rows, t, xb_ref, d_ref[0, t], sems.at[slot]).start(priority=t % 2)

    @pl.when(i < ND_PROMPT)
    def _():
        fill_and_scatter(hp_ref, dp_ref, TD)

    @pl.when(i == ND_PROMPT)
    def _():
        fill_and_scatter(hs_ref, ds_ref, TL)

    def drain(s, n_rows):
        def body(t, c):
            _row_copy(row_ref.at[s], t, xb_ref, 0, sems.at[s]).wait()
            return c
        lax.fori_loop(0, n_rows, body, 0, unroll=DMA_UNROLL)

    @pl.when(i > 0)
    def _():
        drain(1 - slot, TD)

    @pl.when(i == ND_PROMPT)
    def _():
        drain(slot, TL)
        zero_ref[...] = jnp.zeros_like(zero_ref)
        bits = [1 << k for k in reversed(range(BM.bit_length() - 1))]

        def pad_copies(fn):
            for e in range(N_BUCKETS):
                off = pad_start_ref[e]
                n = pad_len_ref[e]
                for bit in bits:
                    take = jnp.bitwise_and(n, bit)

                    @pl.when(take != 0)
                    def _(off=off, bit=bit):
                        fn(pltpu.make_async_copy(zero_ref.at[pl.ds(0, bit)], xb_ref.at[pl.ds(off, bit)],
                                                 sems.at[slot]))
                    off = off + take
            for blk in range(NB_MIN, NB_MAX):
                @pl.when(blk >= nvalid_ref[0])
                def _(blk=blk):
                    for half in range(BM // ZERO_ROWS):
                        fn(pltpu.make_async_copy(
                            zero_ref, xb_ref.at[pl.ds(blk * BM + half * ZERO_ROWS, ZERO_ROWS)], sems.at[slot]))

        pad_copies(lambda cp: cp.start())
        pad_copies(lambda cp: cp.wait())


def _experts_kernel(blk_lo_ref, blk_hi_ref, nvalid_ref, xb_ref,
                    wg_lo_ref, wu_lo_ref, wd_lo_ref, wg_hi_ref, wu_hi_ref, wd_hi_ref,
                    yb_ref, wgb_lo_ref, wub_lo_ref, wdb_lo_ref, wgb_hi_ref, wub_hi_ref, wdb_hi_ref,
                    xring_ref, xsems):
    i = pl.program_id(0)
    prev = jnp.maximum(i - 1, 0)
    n_valid = nvalid_ref[0]

    def fetch(b):
        return pltpu.make_async_copy(xb_ref.at[pl.ds(b * BM, BM)], xring_ref.at[b % X_RING], xsems.at[b % X_RING])

    @pl.when(i == 0)
    def _():
        for b in range(X_RING - 1):
            fetch(b).start()

    @pl.when(i + (X_RING - 1) < n_valid)
    def _():
        fetch(i + (X_RING - 1)).start()

    @pl.when((i == 0) | (blk_lo_ref[i] != blk_lo_ref[prev]))
    def _():
        wgb_lo_ref[...] = wg_lo_ref[...].astype(BF16)
        wub_lo_ref[...] = wu_lo_ref[...].astype(BF16)
        wdb_lo_ref[...] = wd_lo_ref[...].astype(BF16)

    @pl.when((i == 0) | (blk_hi_ref[i] != blk_hi_ref[prev]))
    def _():
        wgb_hi_ref[...] = wg_hi_ref[...].astype(BF16)
        wub_hi_ref[...] = wu_hi_ref[...].astype(BF16)
        wdb_hi_ref[...] = wd_hi_ref[...].astype(BF16)

    @pl.when(i < n_valid)
    def _():
        fetch(i).wait()
        x = xring_ref[i % X_RING].reshape(BM, D_MODEL).astype(BF16)

        def hidden(wgb, wub):
            hg = _dot(x, wgb[...])
            return ((hg * jax.nn.sigmoid(hg)) * _dot(x, wub[...])).astype(BF16)

        act_lo = hidden(wgb_lo_ref, wub_lo_ref)
        act_hi = hidden(wgb_hi_ref, wub_hi_ref)
        y_lo = _dot(act_lo, wdb_lo_ref[...])
        y_hi = _dot(act_hi, wdb_hi_ref[...])
        yb_ref[...] = _pack_bf16_pair(y_lo, y_hi).reshape(BM, SUBLANES, LANES)

    @pl.when(i >= n_valid)
    def _():
        yb_ref[...] = jnp.zeros_like(yb_ref)


def _combine_kernel(d_ref, dn_ref, payp_ref, pays_ref, hp_ref, hs_ref, gfin_ref, yb_ref,
                    yp_ref, ys_ref, buf_ref, sems):
    i = pl.program_id(0)
    slot = i % 2

    @pl.when(i == 0)
    def _():
        def body(t, c):
            _row_copy(yb_ref, d_ref[0, t], buf_ref.at[slot], t, sems.at[slot]).start()
            return c
        lax.fori_loop(0, TL, body, 0, unroll=DMA_UNROLL)

    def drain(t, c):
        _row_copy(yb_ref, 0, buf_ref.at[slot], t, sems.at[slot]).wait()
        return c

    lax.fori_loop(0, TL, drain, 0, unroll=DMA_UNROLL)

    def finish_chunk(h_ref, pay_ref, out_ref, r0):
        rows = slice(r0, r0 + DMA_CHUNK)
        y_lo, y_hi = _unpack_bf16_pair(buf_ref.at[slot][rows].reshape(DMA_CHUNK, D_MODEL))
        pay = pay_ref[rows, :]
        moe = y_lo * pay[:, PAY_LO:PAY_LO + 1] + y_hi * pay[:, PAY_HI:PAY_HI + 1]
        out_ref[rows, :] = _rms(h_ref[rows, :] + moe, gfin_ref[...])

    @pl.when(i < NT_PROMPT)
    def _():
        for c in range(TL // DMA_CHUNK):
            r0 = c * DMA_CHUNK
            finish_chunk(hp_ref, payp_ref, yp_ref, r0)
            for t in range(r0, r0 + DMA_CHUNK):
                _row_copy(yb_ref, dn_ref[0, t], buf_ref.at[1 - slot], t, sems.at[1 - slot]).start(priority=t % 2)

    @pl.when(i == NT_PROMPT)
    def _():
        for c in range(TL // DMA_CHUNK):
            finish_chunk(hs_ref, pays_ref, ys_ref, c * DMA_CHUNK)


def _const_spec(shape):
    zeros = (0,) * len(shape)
    return pl.BlockSpec(shape, lambda *_: zeros, pipeline_mode=pl.Buffered(1))


def kernel(x_prompt, x_sample, state_pool, g_mix, w_in, w_pool, pool_scale, g_v, w_spatial, b_spatial,
           w_up_pool, w_up_gmlp, w_out, g_ffn, w_group, b_group, w_router, b_router, w_gate, w_up,
           w_down, g_final):
    gmix = g_mix[0][None, :]
    win = w_in[0].astype(BF16)
    wpool = w_pool[0].astype(BF16)
    pscale = pool_scale[0][None, :]
    gv = g_v[0][None, :]
    ws = w_spatial[0]
    bsp_t = b_spatial[0].T
    wupp = w_up_pool[0].astype(BF16)
    wupg = w_up_gmlp[0].astype(BF16)
    wout = w_out[0].astype(BF16)
    gffn = g_ffn[0][None, :]
    n_route = N_GROUPS + N_EXPERTS
    wrt = jnp.pad(jnp.concatenate([w_group[0], w_router[0]], axis=1), ((0, 0), (0, LANES - n_route)))
    brt = jnp.pad(jnp.concatenate([b_group[0], b_router[0]]), (0, LANES - n_route))[None, :]
    wg = w_gate[0]
    wu = w_up[0]
    wd = w_down[0]
    gfin = g_final[None, :]
    bsp_prompt = jnp.tile(bsp_t, (TL // GMLP_CHUNK, 1))
    bsp_sample = jnp.tile(bsp_t[:DEC_SEQ], (DEC_BATCH, 1))
    hist = jnp.pad(state_pool[0], ((0, 0), (1, 0), (0, 0)))

    weight_specs = [
        _const_spec((1, D_MODEL)),
        _const_spec((D_MODEL, IN_COLS)),
        _const_spec((4, POOL_GROUP_DIM, POOL_GROUP_DIM)),
        _const_spec((1, POOL_WIDTH)),
        _const_spec((1, GMLP_WIDTH)),
        _const_spec((GMLP_HEADS, GMLP_CHUNK, GMLP_CHUNK)),
        _const_spec((TL, GMLP_HEADS)),
        _const_spec((POOL_WIDTH, D_MODEL)),
        _const_spec((GMLP_WIDTH, D_MODEL)),
        _const_spec((D_MODEL, D_MODEL)),
        _const_spec((1, D_MODEL)),
        _const_spec((D_MODEL, LANES)),
        _const_spec((1, LANES)),
    ]
    any_spec = pl.BlockSpec(memory_space=pl.ANY)
    route_scratch = [
        pltpu.VMEM((TL, TL), BF16),
        pltpu.VMEM((1, LANES), F32),
    ]

    def started(s):
        return jnp.minimum(s, NT_PROMPT - 1)

    def finished(s):
        return jnp.maximum(s - 1, 0)

    x_tiles = x_prompt.reshape(NT_PROMPT, TL, D_MODEL)
    h_p, rec_p, pay_p, pool_p, cnt_p = pl.pallas_call(
        _mixer_prompt_kernel,
        grid=(NT_PROMPT + 1,),
        in_specs=[
            pl.BlockSpec((None, TL, D_MODEL), lambda s: (started(s), 0, 0)),
            pl.BlockSpec((None, TL, D_MODEL), lambda s: (finished(s), 0, 0)),
        ] + weight_specs,
        out_specs=[
            pl.BlockSpec((None, TL, D_MODEL), lambda s: (finished(s), 0, 0)),
            pl.BlockSpec((None, REC_ROWS, TL), lambda s: (finished(s), 0, 0)),
            pl.BlockSpec((TL, LANES), lambda s: (finished(s), 0)),
            pl.BlockSpec((None, HIST_ROWS, POOL_WIDTH), lambda s: (started(s) // TILES_PER_SEQ, 0, 0)),
            pl.BlockSpec((1, LANES), lambda s: (0, 0)),
        ],
        out_shape=[
            jax.ShapeDtypeStruct((NT_PROMPT, TL, D_MODEL), F32),
            jax.ShapeDtypeStruct((NT_PROMPT, REC_ROWS, TL), I32),
            jax.ShapeDtypeStruct((T_PROMPT, LANES), F32),
            jax.ShapeDtypeStruct((BATCH, HIST_ROWS, POOL_WIDTH), F32),
            jax.ShapeDtypeStruct((1, LANES), F32),
        ],
        scratch_shapes=[
            pltpu.VMEM((HIST_ROWS + TL, POOL_WIDTH), F32),
            pltpu.VMEM((TL, GMLP_WIDTH), F32),
        ] + route_scratch + [pltpu.VMEM((TL, D_MODEL), BF16)],
        compiler_params=pltpu.CompilerParams(
            dimension_semantics=("arbitrary",), vmem_limit_bytes=VMEM_LIMIT),
        name="mixer_prompt",
    )(x_tiles, x_tiles, gmix, win, wpool, pscale, gv, ws, bsp_prompt, wupp, wupg, wout, gffn, wrt, brt)

    h_s, rec_s, pay_s, pool_s, vn_s, cnt = pl.pallas_call(
        _mixer_sample_kernel,
        grid=(1,),
        in_specs=[
            _const_spec((TL, D_MODEL)),
            _const_spec((DEC_BATCH, HIST_ROWS, POOL_WIDTH)),
            _const_spec((1, LANES)),
        ] + weight_specs,
        out_specs=[
            _const_spec((TL, D_MODEL)),
            pl.BlockSpec((None, REC_ROWS, TL), lambda i: (0, 0, 0)),
            _const_spec((TL, LANES)),
            _const_spec((DEC_BATCH, HIST_ROWS, POOL_WIDTH)),
            _const_spec((TL, GMLP_WIDTH)),
            _const_spec((1, LANES)),
        ],
        out_shape=[
            jax.ShapeDtypeStruct((T_SAMPLE, D_MODEL), F32),
            jax.ShapeDtypeStruct((1, REC_ROWS, TL), I32),
            jax.ShapeDtypeStruct((T_SAMPLE, LANES), F32),
            jax.ShapeDtypeStruct((DEC_BATCH, HIST_ROWS, POOL_WIDTH), F32),
            jax.ShapeDtypeStruct((T_SAMPLE, GMLP_WIDTH), F32),
            jax.ShapeDtypeStruct((1, LANES), F32),
        ],
        scratch_shapes=[
            pltpu.VMEM((DEC_BATCH, HIST_ROWS + DEC_SEQ, POOL_WIDTH), F32),
        ] + route_scratch,
        compiler_params=pltpu.CompilerParams(
            dimension_semantics=("arbitrary",), vmem_limit_bytes=VMEM_LIMIT),
        name="mixer_sample",
    )(x_sample.reshape(T_SAMPLE, D_MODEL), hist, cnt_p, gmix, win, wpool, pscale, gv, ws, bsp_sample,
      wupp, wupg, wout, gffn, wrt, brt)

    counts = cnt[0, :N_BUCKETS].astype(I32)
    nblk = (counts + BM - 1) // BM
    bend = jnp.cumsum(nblk)
    bstart = bend - nblk
    nvalid = bend[-1:]
    step = jnp.minimum(jnp.arange(NB_MAX, dtype=I32), nvalid - 1)
    bucket_ids = jnp.arange(N_BUCKETS, dtype=I32)
    blk_bucket = jnp.minimum(jnp.sum(step[:, None] >= bend[None, :], axis=1), N_BUCKETS - 1)
    expert_lo = jnp.array([g * EXPERTS_PER_GROUP + p for g in range(N_GROUPS) for p in PAIR_LO], I32)
    expert_hi = jnp.array([g * EXPERTS_PER_GROUP + p for g in range(N_GROUPS) for p in PAIR_HI], I32)
    in_bucket = blk_bucket[:, None] == bucket_ids[None, :]
    blk_lo = jnp.sum(jnp.where(in_bucket, expert_lo[None, :], 0), axis=1).astype(I32)
    blk_hi = jnp.sum(jnp.where(in_bucket, expert_hi[None, :], 0), axis=1).astype(I32)
    pad_start = bstart * BM + counts
    pad_len = nblk * BM - counts
    rec = jnp.concatenate([rec_p, rec_s], axis=0)
    first_slot = jnp.sum(jnp.where(rec[:, REC_BUCKET, :, None] == bucket_ids[None, None, :],
                                   (bstart * BM)[None, None, :], 0), axis=-1)
    slot_of = first_slot + rec[:, REC_RANK]
    d_prompt = slot_of[:NT_PROMPT].reshape(ND_PROMPT, 1, TD)
    d_sample = slot_of[NT_PROMPT:].reshape(1, 1, TL)
    d_tiles = slot_of[:, None, :]
    h_moves = h_p.reshape(ND_PROMPT, TD, D_MODEL)

    def move_map(i, *_):
        return (jnp.minimum(i, ND_PROMPT - 1), 0, 0)

    def hp_map(i, *_):
        return (jnp.minimum(i, NT_PROMPT - 1), 0, 0)

    def smem_tile(rows, index_map):
        return pl.BlockSpec((None, 1, rows), index_map, memory_space=pltpu.SMEM)

    xb = pl.pallas_call(
        _dispatch_kernel,
        grid_spec=pltpu.PrefetchScalarGridSpec(
            num_scalar_prefetch=3,
            grid=(ND_PROMPT + 1,),
            in_specs=[
                smem_tile(TD, move_map),
                smem_tile(TL, lambda i, *_: (0, 0, 0)),
                pl.BlockSpec((None, TD, D_MODEL), move_map),
                pl.BlockSpec((TL, D_MODEL), lambda i, *_: (0, 0)),
                pl.BlockSpec((1, D_MODEL), lambda i, *_: (0, 0)),
            ],
            out_specs=any_spec,
            scratch_shapes=[
                pltpu.VMEM((2, TD, SUBLANES, LANES), F32),
                pltpu.VMEM((ZERO_ROWS, SUBLANES, LANES), F32),
                pltpu.SemaphoreType.DMA((2,)),
            ],
        ),
        out_shape=jax.ShapeDtypeStruct((N_SLOTS, SUBLANES, LANES), F32),
        compiler_params=pltpu.CompilerParams(
            dimension_semantics=("arbitrary",), vmem_limit_bytes=VMEM_LIMIT),
        name="dispatch",
    )(pad_start, pad_len, nvalid, d_prompt, d_sample, h_moves, h_s, gffn)

    def lo_map(i, lo, hi, nv):
        return (lo[i], 0, 0)

    def hi_map(i, lo, hi, nv):
        return (hi[i], 0, 0)

    in_proj = (None, D_MODEL, D_EXPERT)
    out_proj = (None, D_EXPERT, D_MODEL)
    yb = pl.pallas_call(
        _experts_kernel,
        grid_spec=pltpu.PrefetchScalarGridSpec(
            num_scalar_prefetch=3,
            grid=(NB_MAX,),
            in_specs=[
                any_spec,
                pl.BlockSpec(in_proj, lo_map), pl.BlockSpec(in_proj, lo_map), pl.BlockSpec(out_proj, lo_map),
                pl.BlockSpec(in_proj, hi_map), pl.BlockSpec(in_proj, hi_map), pl.BlockSpec(out_proj, hi_map),
            ],
            out_specs=pl.BlockSpec((BM, SUBLANES, LANES), lambda i, lo, hi, nv: (i, 0, 0)),
            scratch_shapes=2 * [
                pltpu.VMEM((D_MODEL, D_EXPERT), BF16),
                pltpu.VMEM((D_MODEL, D_EXPERT), BF16),
                pltpu.VMEM((D_EXPERT, D_MODEL), BF16),
            ] + [
                pltpu.VMEM((X_RING, BM, SUBLANES, LANES), F32),
                pltpu.SemaphoreType.DMA((X_RING,)),
            ],
        ),
        out_shape=jax.ShapeDtypeStruct((N_SLOTS, SUBLANES, LANES), I32),
        compiler_params=pltpu.CompilerParams(
            dimension_semantics=("arbitrary",), vmem_limit_bytes=VMEM_LIMIT),
        name="experts",
    )(blk_lo, blk_hi, nvalid, xb, wg, wu, wd, wg, wu, wd)

    def next_tile(i):
        return (jnp.minimum(i + 1, NT_TOTAL - 1), 0, 0)

    y_p, y_s = pl.pallas_call(
        _combine_kernel,
        grid=(NT_TOTAL,),
        in_specs=[
            smem_tile(TL, lambda i: (i, 0, 0)),
            smem_tile(TL, next_tile),
            pl.BlockSpec((TL, LANES), lambda i: (jnp.minimum(i, NT_PROMPT - 1), 0)),
            pl.BlockSpec((TL, LANES), lambda i: (0, 0)),
            pl.BlockSpec((None, TL, D_MODEL), hp_map),
            pl.BlockSpec((TL, D_MODEL), lambda i: (0, 0)),
            pl.BlockSpec((1, D_MODEL), lambda i: (0, 0)),
            any_spec,
        ],
        out_specs=[
            pl.BlockSpec((None, TL, D_MODEL), hp_map),
            pl.BlockSpec((TL, D_MODEL), lambda i: (0, 0)),
        ],
        out_shape=[
            jax.ShapeDtypeStruct((NT_PROMPT, TL, D_MODEL), F32),
            jax.ShapeDtypeStruct((T_SAMPLE, D_MODEL), F32),
        ],
        scratch_shapes=[
            pltpu.VMEM((2, TL, SUBLANES, LANES), I32),
            pltpu.SemaphoreType.DMA((2,)),
        ],
        compiler_params=pltpu.CompilerParams(
            dimension_semantics=("arbitrary",), vmem_limit_bytes=VMEM_LIMIT),
        name="combine",
    )(d_tiles, d_tiles, pay_p, pay_s, h_p, h_s, gfin, yb)

    y_sample = y_s.reshape(DEC_BATCH, DEC_SEQ, D_MODEL)
    new_pool_prompt = pool_p[None, :, 1:, :]
    new_pool_sample = pool_s[None, :, 1:, :]
    new_gmlp_v_sample = vn_s.reshape(1, DEC_BATCH, DEC_SEQ, GMLP_WIDTH)
    y_prompt = y_p.reshape(BATCH, SEQ, D_MODEL)
    return (y_prompt, y_sample, new_pool_prompt, new_pool_sample, new_gmlp_v_sample)
```
